```python
import math
import jax, jax.numpy as jnp
from jax import lax
import numpy as np

D_MODEL = 1024
BATCH = 16
SEQ = 4096
DEPTH = 4

CTX_LEN = 256
GRID_W = 64
N_MOD = 9
D_FF = 2816

POOL_WINDOWS = (2, 4, 8, 16)
POOL_GROUP = 64
POOL_WIDTH = POOL_GROUP * len(POOL_WINDOWS)
MLA_HEADS = 4
MLA_NOPE = 64
MLA_ROPE = 32
MLA_V = 64
MLA_Q_RANK = 384
MLA_KV_RANK = 256
MLA_WIDTH = MLA_HEADS * MLA_V
DIFF_HEADS = 4
DIFF_DIM = 64
DIFF_V = 2 * DIFF_DIM
DIFF_WIDTH = DIFF_HEADS * DIFF_V
MIX_WIDTH = POOL_WIDTH + MLA_WIDTH + DIFF_WIDTH
IN_POOL = POOL_WIDTH
IN_MLA = MLA_Q_RANK + MLA_KV_RANK + MLA_ROPE
IN_DIFF = 3 * DIFF_HEADS * 2 * DIFF_DIM
IN_WIDTH = IN_POOL + IN_MLA + IN_DIFF
ROPE_BASE = 10000.0
Q_BLOCK = 128
EPS = 1e-6

kernel_name = "hymba_pool_mla_diff_macaron"

F32 = jnp.float32


def _rmsnorm(x, g):
    xf = x.astype(F32)
    y = xf * lax.rsqrt(jnp.mean(xf * xf, axis=-1, keepdims=True) + EPS)
    return (y * g.astype(F32)).astype(x.dtype)


def _modulate(n, shift, scale):
    return n * (1.0 + scale) + shift


def _swiglu(x, w1, w3, w2):
    return (jax.nn.silu(x @ w1) * (x @ w3)) @ w2


def _axial_rope(n_tok, rot_dim):
    t = jnp.arange(n_tok)
    row = (t // GRID_W).astype(F32)
    col = (t % GRID_W).astype(F32)
    n_axis = rot_dim // 4
    inv = ROPE_BASE ** (-jnp.arange(n_axis, dtype=F32) / n_axis)
    ang = jnp.concatenate([row[:, None] * inv, col[:, None] * inv], axis=-1)
    return jnp.cos(ang), jnp.sin(ang)


def _rope(x, cos, sin):
    h = x.shape[-1] // 2
    x1 = x[..., :h].astype(F32)
    x2 = x[..., h:].astype(F32)
    c = cos[None, :, None, :]
    s = sin[None, :, None, :]
    return jnp.concatenate([x1 * c - x2 * s, x1 * s + x2 * c], axis=-1).astype(x.dtype)


def _window_mean(u, k, axis):
    n = u.shape[axis]
    lo = k // 2
    hi = k - 1 - lo
    csum = jnp.cumsum(u.astype(F32), axis=axis)
    pad = [(0, 0)] * u.ndim
    pad[axis] = (1, 0)
    pref = jnp.pad(csum, pad)
    idx = jnp.arange(n)
    start = jnp.clip(idx - lo, 0, n)
    end = jnp.clip(idx + hi + 1, 0, n)
    tot = jnp.take(pref, end, axis=axis) - jnp.take(pref, start, axis=axis)
    cnt_shape = [1] * u.ndim
    cnt_shape[axis] = n
    cnt = (end - start).astype(F32).reshape(cnt_shape)
    return (tot / cnt).astype(u.dtype)


def _pool_mixer(u, pool_w, pool_scale, rows):
    b, n, _ = u.shape
    outs = []
    for g, k in enumerate(POOL_WINDOWS):
        ug = u[..., g * POOL_GROUP:(g + 1) * POOL_GROUP]
        if rows is None:
            m = _window_mean(ug, k, 1)
        else:
            ug2 = ug.reshape(b, rows, GRID_W, POOL_GROUP)
            m = _window_mean(_window_mean(ug2, k, 1), k, 2).reshape(b, n, POOL_GROUP)
        outs.append((m - ug) @ pool_w[g])
    return jnp.concatenate(outs, axis=-1) * pool_scale


def _mla_qkv(u, q_norm, w_uq, kv_norm, w_ukv, rope):
    b, n, _ = u.shape
    c_q = _rmsnorm(u[..., :MLA_Q_RANK], q_norm)
    c_kv = _rmsnorm(u[..., MLA_Q_RANK:MLA_Q_RANK + MLA_KV_RANK], kv_norm)
    k_rope = u[..., MLA_Q_RANK + MLA_KV_RANK:][:, :, None, :]
    q = (c_q @ w_uq).reshape(b, n, MLA_HEADS, MLA_NOPE + MLA_ROPE)
    kv = (c_kv @ w_ukv).reshape(b, n, MLA_HEADS, MLA_NOPE + MLA_V)
    q_nope, q_rope = q[..., :MLA_NOPE], q[..., MLA_NOPE:]
    if rope is not None:
        q_rope = _rope(q_rope, *rope)
        k_rope = _rope(k_rope, *rope)
    q = jnp.concatenate([q_nope, q_rope], axis=-1)
    k = jnp.concatenate([kv[..., :MLA_NOPE], jnp.broadcast_to(k_rope, (b, n, MLA_HEADS, MLA_ROPE))], axis=-1)
    v = kv[..., MLA_NOPE:]
    return q, k, v


def _diff_qkv(u, rope):
    b, n, _ = u.shape
    hq = DIFF_HEADS * 2 * DIFF_DIM
    q = u[..., :hq].reshape(b, n, 2 * DIFF_HEADS, DIFF_DIM)
    k = u[..., hq:2 * hq].reshape(b, n, 2 * DIFF_HEADS, DIFF_DIM)
    v = u[..., 2 * hq:].reshape(b, n, DIFF_HEADS, DIFF_V)
    if rope is not None:
        q = _rope(q, *rope)
        k = _rope(k, *rope)
    return q[:, :, 0::2], q[:, :, 1::2], k[:, :, 0::2], k[:, :, 1::2], v


def _sdpa(q, k, v, scale):
    s = jnp.einsum('bqhd,bkhd->bhqk', q, k).astype(F32) * scale
    p = jax.nn.softmax(s, axis=-1).astype(v.dtype)
    return jnp.einsum('bhqk,bkhd->bqhd', p, v)


def _diff_sdpa(q1, q2, k1, k2, v, scale, lam):
    p1 = jax.nn.softmax(jnp.einsum('bqhd,bkhd->bhqk', q1, k1).astype(F32) * scale, axis=-1)
    p2 = jax.nn.softmax(jnp.einsum('bqhd,bkhd->bhqk', q2, k2).astype(F32) * scale, axis=-1)
    return jnp.einsum('bhqk,bkhd->bqhd', (p1 - lam * p2).astype(v.dtype), v)


def _by_query_blocks(fn, qs):
    b, s = qs[0].shape[:2]
    nb = s // Q_BLOCK
    blocks = tuple(q.reshape(b, nb, Q_BLOCK, *q.shape[2:]).swapaxes(0, 1) for q in qs)
    out = lax.map(lambda qb: fn(*qb), blocks)
    return out.swapaxes(0, 1).reshape(b, s, *out.shape[3:])


def _diff_out(o, subln, lam_init):
    o = _rmsnorm(o, subln) * (1.0 - lam_init)
    return o.reshape(o.shape[0], o.shape[1], DIFF_WIDTH)


def _token_mixing(n_lat, n_ctx, w_in, w_out, pool_w, pool_scale, mla_q_norm, mla_w_uq, mla_kv_norm,
                  mla_w_ukv, diff_lambda, diff_subln, lam_init, rope_mla, rope_diff, rows, with_ctx):
    b, s, _ = n_lat.shape
    p_lat = n_lat @ w_in
    p_ctx = n_ctx @ w_in
    s1, s2 = IN_POOL, IN_POOL + IN_MLA
    a_lat = _pool_mixer(p_lat[..., :s1], pool_w, pool_scale, rows)
    qb, kb, vb = _mla_qkv(p_lat[..., s1:s2], mla_q_norm, mla_w_uq, mla_kv_norm, mla_w_ukv, rope_mla)
    qbc, kbc, vbc = _mla_qkv(p_ctx[..., s1:s2], mla_q_norm, mla_w_uq, mla_kv_norm, mla_w_ukv, None)
    kb_all = jnp.concatenate([kbc, kb], axis=1)
    vb_all = jnp.concatenate([vbc, vb], axis=1)
    scale_b = (MLA_NOPE + MLA_ROPE) ** -0.5
    b_lat = _by_query_blocks(lambda q: _sdpa(q, kb_all, vb_all, scale_b), (qb,))
    lq1, lk1, lq2, lk2 = diff_lambda.astype(F32)
    lam = jnp.exp(jnp.sum(lq1 * lk1)) - jnp.exp(jnp.sum(lq2 * lk2)) + lam_init
    q1, q2, k1, k2, vc = _diff_qkv(p_lat[..., s2:], rope_diff)
    q1c, q2c, k1c, k2c, vcc = _diff_qkv(p_ctx[..., s2:], None)
    k1_all = jnp.concatenate([k1c, k1], axis=1)
    k2_all = jnp.concatenate([k2c, k2], axis=1)
    vc_all = jnp.concatenate([vcc, vc], axis=1)
    scale_c = DIFF_DIM ** -0.5
    c_lat = _by_query_blocks(lambda a1, a2: _diff_sdpa(a1, a2, k1_all, k2_all, vc_all, scale_c, lam), (q1, q2))
    y_lat = jnp.concatenate([a_lat, b_lat.reshape(b, s, MLA_WIDTH), _diff_out(c_lat, diff_subln, lam_init)],
                            axis=-1) @ w_out
    if not with_ctx:
        return y_lat, None
    bc, lc, _ = n_ctx.shape
    a_ctx = _pool_mixer(p_ctx[..., :s1], pool_w, pool_scale, None)
    b_ctx = _sdpa(qbc, kbc, vbc, scale_b).reshape(bc, lc, MLA_WIDTH)
    c_ctx = _diff_out(_diff_sdpa(q1c, q2c, k1c, k2c, vcc, scale_c, lam), diff_subln, lam_init)
    y_ctx = jnp.concatenate([a_ctx, b_ctx, c_ctx], axis=-1) @ w_out
    return y_lat, y_ctx


def setup_inputs(seed: int = 0) -> dict:
    key = jax.random.key(seed)
    ks = iter(jax.random.split(key, 32))

    def nrm(shape, scale):
        return jax.random.normal(next(ks), shape, F32) * scale

    def gain(shape):
        return 1.0 + nrm(shape, 0.02)

    L = DEPTH
    return {
        "x": nrm((BATCH, SEQ, D_MODEL), 1.0),
        "c": nrm((BATCH, D_MODEL), 1.0),
        "ctx": nrm((BATCH, CTX_LEN, D_MODEL), 1.0),
        "c_ctx": nrm((D_MODEL,), 1.0),
        "w_mod": nrm((L, D_MODEL, N_MOD * D_MODEL), 0.5 * D_MODEL ** -0.5),
        "b_mod": nrm((L, N_MOD * D_MODEL), 0.01),
        "ffn1_norm": gain((L, D_MODEL)),
        "ffn1_w1": nrm((L, D_MODEL, D_FF), D_MODEL ** -0.5),
        "ffn1_w3": nrm((L, D_MODEL, D_FF), D_MODEL ** -0.5),
        "ffn1_w2": nrm((L, D_FF, D_MODEL), D_FF ** -0.5),
        "mix_norm": gain((L, D_MODEL)),
        "w_in": nrm((L, D_MODEL, IN_WIDTH), D_MODEL ** -0.5),
        "w_out": nrm((L, MIX_WIDTH, D_MODEL), MIX_WIDTH ** -0.5),
        "pool_w": nrm((L, len(POOL_WINDOWS), POOL_GROUP, POOL_GROUP), POOL_GROUP ** -0.5),
        "pool_scale": 1.0 + nrm((L, POOL_WIDTH), 0.1),
        "mla_q_norm": gain((L, MLA_Q_RANK)),
        "mla_w_uq": nrm((L, MLA_Q_RANK, MLA_HEADS * (MLA_NOPE + MLA_ROPE)), MLA_Q_RANK ** -0.5),
        "mla_kv_norm": gain((L, MLA_KV_RANK)),
        "mla_w_ukv": nrm((L, MLA_KV_RANK, MLA_HEADS * (MLA_NOPE + MLA_V)), MLA_KV_RANK ** -0.5),
        "diff_lambda": nrm((L, 4, DIFF_DIM), 0.1),
        "diff_subln": gain((L, DIFF_V)),
        "ffn2_norm": gain((L, D_MODEL)),
        "ffn2_w1": nrm((L, D_MODEL, D_FF), D_MODEL ** -0.5),
        "ffn2_w3": nrm((L, D_MODEL, D_FF), D_MODEL ** -0.5),
        "ffn2_w2": nrm((L, D_FF, D_MODEL), D_FF ** -0.5),
        "final_norm": gain((D_MODEL,)),
    }


def reference(x, c, ctx, c_ctx, w_mod, b_mod, ffn1_norm, ffn1_w1, ffn1_w3, ffn1_w2, mix_norm, w_in, w_out,
              pool_w, pool_scale, mla_q_norm, mla_w_uq, mla_kv_norm, mla_w_ukv, diff_lambda, diff_subln,
              ffn2_norm, ffn2_w1, ffn2_w3, ffn2_w2, final_norm):
    b, seq, d = x.shape
    rows = seq // GRID_W
    rope_mla = _axial_rope(seq, MLA_ROPE)
    rope_diff = _axial_rope(seq, DIFF_DIM)
    sc = jax.nn.silu(c)
    scc = jax.nn.silu(c_ctx)
    h, hc = x, ctx
    for i in range(DEPTH):
        last = i == DEPTH - 1
        m = jnp.split((sc @ w_mod[i] + b_mod[i])[:, None, :], N_MOD, axis=-1)
        mc = jnp.split((scc @ w_mod[i] + b_mod[i])[None, None, :], N_MOD, axis=-1)
        h = h + 0.5 * m[2] * _swiglu(_modulate(_rmsnorm(h, ffn1_norm[i]), m[0], m[1]),
                                     ffn1_w1[i], ffn1_w3[i], ffn1_w2[i])
        hc = hc + 0.5 * mc[2] * _swiglu(_modulate(_rmsnorm(hc, ffn1_norm[i]), mc[0], mc[1]),
                                        ffn1_w1[i], ffn1_w3[i], ffn1_w2[i])
        lam_init = 0.8 - 0.6 * math.exp(-0.3 * i)
        n_lat = _modulate(_rmsnorm(h, mix_norm[i]), m[3], m[4])
        n_ctx = _modulate(_rmsnorm(hc, mix_norm[i]), mc[3], mc[4])
        y, yc = _token_mixing(n_lat, n_ctx, w_in[i], w_out[i], pool_w[i], pool_scale[i], mla_q_norm[i],
                              mla_w_uq[i], mla_kv_norm[i], mla_w_ukv[i], diff_lambda[i], diff_subln[i],
                              lam_init, rope_mla, rope_diff, rows, not last)
        h = h + m[5] * y
        h = h + 0.5 * m[8] * _swiglu(_modulate(_rmsnorm(h, ffn2_norm[i]), m[6], m[7]),
                                     ffn2_w1[i], ffn2_w3[i], ffn2_w2[i])
        if not last:
            hc = hc + mc[5] * yc
            hc = hc + 0.5 * mc[8] * _swiglu(_modulate(_rmsnorm(hc, ffn2_norm[i]), mc[6], mc[7]),
                                            ffn2_w1[i], ffn2_w3[i], ffn2_w2[i])
    return _rmsnorm(h, final_norm)
```

```python
import functools
import math

import jax
import jax.numpy as jnp
from jax import lax
from jax.experimental import pallas as pl
from jax.experimental.pallas import tpu as pltpu

F32 = jnp.float32
BF16 = jnp.bfloat16

GRID_W = 64
N_MOD = 9
POOL_WINDOWS = (2, 4, 8, 16)
POOL_GROUP = 64
POOL_WIDTH = POOL_GROUP * len(POOL_WINDOWS)
POOL_HALF = max(POOL_WINDOWS) // 2
MLA_HEADS = 4
MLA_NOPE = 64
MLA_ROPE = 32
MLA_V = 64
MLA_Q_RANK = 384
MLA_KV_RANK = 256
DIFF_HEADS = 4
DIFF_DIM = 64
DIFF_V = 2 * DIFF_DIM
ROPE_BASE = 10000.0
EPS = 1e-6

LANES = 128
HEAD_PAD = LANES
MOD_ROWS = 24
VMEM_LIMIT = 56 * 1024 * 1024

C_POOL = 0
C_CQ = C_POOL + POOL_WIDTH
C_CKV = C_CQ + MLA_Q_RANK
C_KR = C_CKV + MLA_KV_RANK
C_DQ = C_KR + HEAD_PAD
C_DK = C_DQ + DIFF_HEADS * 2 * DIFF_DIM
C_DV = C_DK + DIFF_HEADS * 2 * DIFF_DIM
C_END = C_DV + DIFF_HEADS * DIFF_V


def _dot(a, b):
    return jnp.dot(a, b, preferred_element_type=F32)


def _dot_nt(a, b):
    return lax.dot_general(a, b, (((1,), (1,)), ((), ())), preferred_element_type=F32)


def _rms(x, g):
    return x * lax.rsqrt(jnp.mean(x * x, axis=-1, keepdims=True) + EPS) * g


def _silu(a):
    return a * (1.0 / (1.0 + jnp.exp(-a)))


def _const_spec(shape):
    return pl.BlockSpec(shape, lambda *_: (0,) * len(shape), pipeline_mode=pl.Buffered(1))


def _params(n_axes):
    return pltpu.CompilerParams(dimension_semantics=("parallel",) * n_axes, vmem_limit_bytes=VMEM_LIMIT)


def _mod_kernel(cc_ref, w_ref, b_ref, o_ref):
    sc = _silu(cc_ref[...])
    o_ref[0] = jnp.dot(sc, w_ref[0], preferred_element_type=F32, precision=lax.Precision.HIGHEST) + b_ref[0]


def _modulation(cc, w_mod, b_mod):
    depth, d, width = w_mod.shape
    bn = 9 * LANES
    return pl.pallas_call(
        _mod_kernel,
        grid=(depth, width // bn),
        in_specs=[
            pl.BlockSpec((MOD_ROWS, d), lambda l, j: (0, 0)),
            pl.BlockSpec((1, d, bn), lambda l, j: (l, 0, j)),
            pl.BlockSpec((1, 1, bn), lambda l, j: (l, 0, j)),
        ],
        out_specs=pl.BlockSpec((1, MOD_ROWS, bn), lambda l, j: (l, 0, j)),
        out_shape=jax.ShapeDtypeStruct((depth, MOD_ROWS, width), F32),
        compiler_params=_params(2),
        name="modulation",
    )(cc, w_mod, b_mod.reshape(depth, 1, width))


def _ffn_kernel(*refs, pre, final, k0, n_chunks):
    it = iter(refs)
    x_ref, m_ref = next(it), next(it)
    if pre:
        a_ref, b_ref, c_ref, wo_ref = next(it), next(it), next(it), next(it)
    g_ref, w1_ref, w3_ref, w2_ref = next(it), next(it), next(it), next(it)
    if final:
        fg_ref = next(it)
    o_ref, hmid_ref = next(it), next(it)

    x = x_ref[0]
    if pre:
        mix = jnp.concatenate([a_ref[0], b_ref[0], c_ref[0]], axis=-1)
        x = x + m_ref[0, 5:6, :] * _dot(mix, wo_ref[...])
    xm = _rms(x, g_ref[...]) * (1.0 + m_ref[0, k0 + 1:k0 + 2, :]) + m_ref[0, k0:k0 + 1, :]
    xb = xm.astype(BF16)
    fc = w1_ref.shape[1] // n_chunks
    for c in range(n_chunks):
        a = _dot(xb, w1_ref[:, c * fc:(c + 1) * fc])
        b = _dot(xb, w3_ref[:, c * fc:(c + 1) * fc])
        hmid_ref[:, c * fc:(c + 1) * fc] = (_silu(a) * b).astype(BF16)
    y = _dot(hmid_ref[...], w2_ref[...])
    out = x + (0.5 * m_ref[0, k0 + 2:k0 + 3, :]) * y
    if final:
        out = _rms(out, fg_ref[...])
    o_ref[0] = out


def _ffn(x, mods, mod_row, g, w1, w3, w2, *, k0, tm, mix=None, final_g=None):
    bsz, n, d = x.shape
    dff = w1.shape[1]
    pre, final = mix is not None, final_g is not None
    tok = lambda w: pl.BlockSpec((1, tm, w), lambda b, i: (b, i, 0))
    mod_spec = pl.BlockSpec((1, N_MOD, d), (lambda b, i: (b, 0, 0)) if mod_row is None else (lambda b, i: (mod_row, 0, 0)))
    args, specs = [x, mods], [tok(d), mod_spec]
    if pre:
        a, bb, c, wo = mix
        args += [a, bb, c, wo]
        specs += [tok(a.shape[-1]), tok(bb.shape[-1]), tok(c.shape[-1]), _const_spec(wo.shape)]
    args += [g.reshape(1, d), w1, w3, w2]
    specs += [_const_spec((1, d)), _const_spec(w1.shape), _const_spec(w3.shape), _const_spec(w2.shape)]
    if final:
        args.append(final_g.reshape(1, d))
        specs.append(_const_spec((1, d)))
    return pl.pallas_call(
        functools.partial(_ffn_kernel, pre=pre, final=final, k0=k0, n_chunks=2),
        grid=(bsz, n // tm),
        in_specs=specs,
        out_specs=tok(d),
        out_shape=jax.ShapeDtypeStruct(x.shape, F32),
        scratch_shapes=[pltpu.VMEM((tm, dff), BF16)],
        compiler_params=_params(2),
        name="ffn_mix" if pre else "ffn",
    )(*args)


def _rope(x, c, sa, sb, shift):
    return x * c + pltpu.roll(x, LANES - shift, 1) * sa + pltpu.roll(x, shift, 1) * sb


def _proj_kernel(x_ref, m_ref, g_ref, wall_ref, qn_ref, kvn_ref, wuq_ref, wk_ref, wv_ref,
                 cm_ref, sam_ref, sbm_ref, cd_ref, sad_ref, sbd_ref,
                 u_ref, qm_ref, km_ref, vm_ref, qd_ref, kd_ref, vd_ref):
    x = x_ref[0]
    n = _rms(x, g_ref[...]) * (1.0 + m_ref[0, 4:5, :]) + m_ref[0, 3:4, :]
    p = _dot(n.astype(BF16), wall_ref[...])
    u_ref[0] = p[:, C_POOL:C_CQ]

    cqn = _rms(p[:, C_CQ:C_CKV], qn_ref[...]).astype(BF16)
    ckvn = _rms(p[:, C_CKV:C_KR], kvn_ref[...]).astype(BF16)
    q = _dot(cqn, wuq_ref[...])
    k = _dot(ckvn, wk_ref[...])
    vm_ref[0] = _dot(ckvn, wv_ref[...]).astype(BF16)
    cm, sam, sbm = cm_ref[...], sam_ref[...], sbm_ref[...]
    half_m = MLA_ROPE // 2
    kr = _rope(p[:, C_KR:C_DQ], cm, sam, sbm, half_m)
    for h in range(MLA_HEADS):
        sl = slice(h * HEAD_PAD, (h + 1) * HEAD_PAD)
        qm_ref[0, :, sl] = _rope(q[:, sl], cm, sam, sbm, half_m).astype(BF16)
        km_ref[0, :, sl] = (k[:, sl] + kr).astype(BF16)

    cd, sad, sbd = cd_ref[...], sad_ref[...], sbd_ref[...]
    half_d = DIFF_DIM // 2
    for h in range(DIFF_HEADS):
        sl = slice(h * HEAD_PAD, (h + 1) * HEAD_PAD)
        qd_ref[0, :, sl] = _rope(p[:, C_DQ + h * HEAD_PAD:C_DQ + (h + 1) * HEAD_PAD], cd, sad, sbd, half_d).astype(BF16)
        kd_ref[0, :, sl] = _rope(p[:, C_DK + h * HEAD_PAD:C_DK + (h + 1) * HEAD_PAD], cd, sad, sbd, half_d).astype(BF16)
    vd_ref[0] = p[:, C_DV:C_END].astype(BF16)


def _proj(x, mods, mod_row, g, wall, qn, kvn, wuq, wk, wv, tables, *, tm):
    bsz, n, d = x.shape
    tok = lambda w: pl.BlockSpec((1, tm, w), lambda b, i: (b, i, 0))
    mod_spec = pl.BlockSpec((1, N_MOD, d), (lambda b, i: (b, 0, 0)) if mod_row is None else (lambda b, i: (mod_row, 0, 0)))
    tab = pl.BlockSpec((tm, LANES), lambda b, i: (i, 0))
    widths = (POOL_WIDTH, MLA_HEADS * HEAD_PAD, MLA_HEADS * HEAD_PAD, MLA_HEADS * MLA_V,
              DIFF_HEADS * HEAD_PAD, DIFF_HEADS * HEAD_PAD, DIFF_HEADS * DIFF_V)
    dtypes = (F32,) + (BF16,) * 6
    return pl.pallas_call(
        _proj_kernel,
        grid=(bsz, n // tm),
        in_specs=[tok(d), mod_spec, _const_spec((1, d)), _const_spec(wall.shape),
                  _const_spec((1, MLA_Q_RANK)), _const_spec((1, MLA_KV_RANK)),
                  _const_spec(wuq.shape), _const_spec(wk.shape), _const_spec(wv.shape)] + [tab] * 6,
        out_specs=[tok(w) for w in widths],
        out_shape=[jax.ShapeDtypeStruct((bsz, n, w), dt) for w, dt in zip(widths, dtypes)],
        compiler_params=_params(2),
        name="mix_in",
    )(x, mods, g.reshape(1, d), wall, qn.reshape(1, -1), kvn.reshape(1, -1), wuq, wk, wv, *tables)


def _nested_window_sums(load, lane):
    a2 = load(-1) + load(0)
    a4 = a2 + load(-2) + load(1)
    a8 = a4 + load(-4) + load(-3) + load(2) + load(3)
    a16 = a8 + load(-8) + load(-7) + load(-6) + load(-5) + load(4) + load(5) + load(6) + load(7)
    return jnp.where(lane < POOL_GROUP, a2, jnp.where(lane < 2 * POOL_GROUP, a4, jnp.where(lane < 3 * POOL_GROUP, a8, a16)))


def _pool_kernel(ul_ref, uc_ref, icg_ref, icx_ref, bd_ref, ps_ref, al_ref, ac_ref,
                 z_ref, y_ref, dl_ref, zc_ref, *, rows):
    gw, pad = GRID_W, POOL_HALF
    stride = gw + 2 * pad
    n_ctx = uc_ref.shape[1]
    lane = lax.broadcasted_iota(jnp.int32, (gw, POOL_WIDTH), 1)

    z_ref[...] = jnp.zeros(z_ref.shape, F32)
    y_ref[...] = jnp.zeros(y_ref.shape, F32)

    def base_of(r):
        return pl.multiple_of((r + pad) * stride + pad, 8)

    def fill(r, carry):
        z_ref[pl.ds(base_of(r), gw), :] = ul_ref[0, pl.ds(pl.multiple_of(r * gw, 8), gw), :]
        return carry

    lax.fori_loop(0, rows, fill, 0)

    def row_pass(r, carry):
        base = base_of(r)
        s = _nested_window_sums(lambda d: z_ref[pl.ds(base + d * stride, gw), :], lane)
        y_ref[pl.ds(base, gw), :] = s * icg_ref[pl.ds(r, 1), :]
        return carry

    lax.fori_loop(0, rows, row_pass, 0)

    def col_pass(r, carry):
        slab = y_ref[pl.ds(pl.multiple_of((r + pad) * stride, 8), stride), :]
        s = _nested_window_sums(lambda d: slab[pad + d:pad + d + gw, :], lane)
        tok = pl.ds(pl.multiple_of(r * gw, 8), gw)
        dl_ref[tok, :] = (s * icg_ref[...] - ul_ref[0, tok, :]).astype(BF16)
        return carry

    lax.fori_loop(0, rows, col_pass, 0)
    al_ref[0] = (_dot(dl_ref[...], bd_ref[...]) * ps_ref[...]).astype(BF16)

    zc_ref[...] = jnp.zeros(zc_ref.shape, F32)
    uc = uc_ref[0]
    zc_ref[pad:pad + n_ctx, :] = uc
    lane_c = lax.broadcasted_iota(jnp.int32, (n_ctx, POOL_WIDTH), 1)
    sc = _nested_window_sums(lambda d: zc_ref[pad + d:pad + d + n_ctx, :], lane_c)
    dc = (sc * icx_ref[...] - uc).astype(BF16)
    ac_ref[0] = (_dot(dc, bd_ref[...]) * ps_ref[...]).astype(BF16)


def _pool(u_lat, u_ctx, icg, icx, bd, ps):
    bsz, s, w = u_lat.shape
    n_ctx = u_ctx.shape[1]
    rows = s // GRID_W
    stride = GRID_W + 2 * POOL_HALF
    padded = (rows + 2 * POOL_HALF) * stride
    return pl.pallas_call(
        functools.partial(_pool_kernel, rows=rows),
        grid=(bsz,),
        in_specs=[pl.BlockSpec((1, s, w), lambda b: (b, 0, 0)), pl.BlockSpec((1, n_ctx, w), lambda b: (b, 0, 0)),
                  _const_spec(icg.shape), _const_spec(icx.shape), _const_spec(bd.shape), _const_spec((1, w))],
        out_specs=[pl.BlockSpec((1, s, w), lambda b: (b, 0, 0)), pl.BlockSpec((1, n_ctx, w), lambda b: (b, 0, 0))],
        out_shape=[jax.ShapeDtypeStruct((bsz, s, w), BF16), jax.ShapeDtypeStruct((bsz, n_ctx, w), BF16)],
        scratch_shapes=[pltpu.VMEM((padded, w), F32), pltpu.VMEM((padded, w), F32),
                        pltpu.VMEM((s, w), BF16), pltpu.VMEM((n_ctx + 2 * POOL_HALF, w), F32)],
        compiler_params=_params(1),
        name="pool",
    )(u_lat, u_ctx, icg, icx, bd, ps.reshape(1, w))


def _softmax_parts(qh, k_refs, sl, scale):
    ss = [_dot_nt(qh, k_ref[0, :, sl]) * scale for k_ref in k_refs]
    m = functools.reduce(jnp.maximum, [jnp.max(s, axis=-1, keepdims=True) for s in ss])
    ps = [jnp.exp(s - m) for s in ss]
    l = functools.reduce(jnp.add, [jnp.sum(p, axis=-1, keepdims=True) for p in ps])
    return ps, l


def _mla_attn_kernel(*refs, nseg, scale):
    q_ref, k_refs, v_refs, o_ref = refs[0], refs[1:1 + nseg], refs[1 + nseg:1 + 2 * nseg], refs[1 + 2 * nseg]
    q = q_ref[0]
    lane = lax.broadcasted_iota(jnp.int32, (q.shape[0], LANES), 1)
    for pair in range(MLA_HEADS // 2):
        vsl = slice(pair * LANES, (pair + 1) * LANES)
        outs = []
        for h in (2 * pair, 2 * pair + 1):
            sl = slice(h * HEAD_PAD, (h + 1) * HEAD_PAD)
            ps, l = _softmax_parts(q[:, sl], k_refs, sl, scale)
            o = functools.reduce(jnp.add, [_dot(p.astype(BF16), v_ref[0, :, vsl]) for p, v_ref in zip(ps, v_refs)])
            outs.append(o / l)
        o_ref[0, :, vsl] = jnp.where(lane < MLA_V, outs[0], outs[1]).astype(BF16)


def _diff_attn_kernel(*refs, nseg, scale, lam_init):
    dl_ref, sub_ref, q_ref = refs[0], refs[1], refs[2]
    k_refs, v_refs, o_ref = refs[3:3 + nseg], refs[3 + nseg:3 + 2 * nseg], refs[3 + 2 * nseg]
    dl = dl_ref[...]
    lam = (jnp.exp(jnp.sum(dl[0:1] * dl[1:2], axis=-1, keepdims=True))
           - jnp.exp(jnp.sum(dl[2:3] * dl[3:4], axis=-1, keepdims=True)) + lam_init)
    q = q_ref[0]
    lane = lax.broadcasted_iota(jnp.int32, (q.shape[0], LANES), 1)
    zero = jnp.zeros((), BF16)
    for h in range(DIFF_HEADS):
        sl = slice(h * HEAD_PAD, (h + 1) * HEAD_PAD)
        qh = q[:, sl]
        p1s, l1 = _softmax_parts(jnp.where(lane < DIFF_DIM, qh, zero), k_refs, sl, scale)
        p2s, l2 = _softmax_parts(jnp.where(lane >= DIFF_DIM, qh, zero), k_refs, sl, scale)
        r1, r2 = 1.0 / l1, lam / l2
        o = functools.reduce(jnp.add, [_dot((p1 * r1 - p2 * r2).astype(BF16), v_ref[0, :, sl])
                                       for p1, p2, v_ref in zip(p1s, p2s, v_refs)])
        o_ref[0, :, sl] = (_rms(o, sub_ref[...]) * (1.0 - lam_init)).astype(BF16)


def _attention(kernel, name, q, ks, vs, extra, *, tq):
    bsz, nq, wq = q.shape
    nseg = len(ks)
    whole = lambda a: pl.BlockSpec((1,) + a.shape[1:], lambda b, i: (b, 0, 0))
    wv = vs[0].shape[-1]
    return pl.pallas_call(
        functools.partial(kernel, nseg=nseg),
        grid=(bsz, nq // tq),
        in_specs=[_const_spec(e.shape) for e in extra] + [pl.BlockSpec((1, tq, wq), lambda b, i: (b, i, 0))]
        + [whole(k) for k in ks] + [whole(v) for v in vs],
        out_specs=pl.BlockSpec((1, tq, wv), lambda b, i: (b, i, 0)),
        out_shape=jax.ShapeDtypeStruct((bsz, nq, wv), BF16),
        compiler_params=_params(2),
        name=name,
    )(*extra, q, *ks, *vs)


def _axial_angles(n_tok, rot_dim):
    t = jnp.arange(n_tok)
    row = (t // GRID_W).astype(F32)
    col = (t % GRID_W).astype(F32)
    n_axis = rot_dim // 4
    inv = ROPE_BASE ** (-jnp.arange(n_axis, dtype=F32) / n_axis)
    return jnp.concatenate([row[:, None] * inv, col[:, None] * inv], axis=-1)


def _rope_tables(n_tok, n_ctx):
    am, ad = _axial_angles(n_tok, MLA_ROPE), _axial_angles(n_tok, DIFF_DIM)
    one = lambda w: jnp.ones((n_tok, w), F32)
    zero = lambda w: jnp.zeros((n_tok, w), F32)
    cm = jnp.concatenate([one(MLA_NOPE), jnp.cos(am), jnp.cos(am), one(HEAD_PAD - MLA_NOPE - MLA_ROPE)], axis=-1)
    sam = jnp.concatenate([zero(MLA_NOPE), -jnp.sin(am), zero(HEAD_PAD - MLA_NOPE - MLA_ROPE // 2)], axis=-1)
    sbm = jnp.concatenate([zero(MLA_NOPE + MLA_ROPE // 2), jnp.sin(am), zero(HEAD_PAD - MLA_NOPE - MLA_ROPE)], axis=-1)
    cd = jnp.tile(jnp.cos(ad), (1, 4))
    sad = jnp.tile(jnp.concatenate([-jnp.sin(ad), zero(DIFF_DIM // 2)], axis=-1), (1, 2))
    sbd = jnp.tile(jnp.concatenate([zero(DIFF_DIM // 2), jnp.sin(ad)], axis=-1), (1, 2))
    lat = (cm, sam, sbm, cd, sad, sbd)
    ident = (jnp.ones((n_ctx, LANES), F32),) + (jnp.zeros((n_ctx, LANES), F32),) * 2
    return lat, ident + ident


def _inv_counts(n):
    idx = jnp.arange(n)
    cols = []
    for k in POOL_WINDOWS:
        lo, hi = k // 2, k - 1 - k // 2
        cnt = (jnp.clip(idx + hi + 1, 0, n) - jnp.clip(idx - lo, 0, n)).astype(F32)
        cols.append(jnp.broadcast_to((1.0 / cnt)[:, None], (n, POOL_GROUP)))
    return jnp.concatenate(cols, axis=-1)


def _layer_weights(w_in, w_out, pool_w, mla_w_uq, mla_w_ukv):
    d = w_in.shape[0]
    s1, s2 = POOL_WIDTH, POOL_WIDTH + MLA_Q_RANK + MLA_KV_RANK + MLA_ROPE
    kr_pad = jnp.zeros((d, HEAD_PAD), F32).at[:, MLA_NOPE:MLA_NOPE + MLA_ROPE].set(w_in[:, s2 - MLA_ROPE:s2])
    wall = jnp.concatenate([w_in[:, :s2 - MLA_ROPE], kr_pad, w_in[:, s2:]], axis=-1).astype(BF16)
    qd = MLA_NOPE + MLA_ROPE
    wuq = jnp.pad(mla_w_uq.reshape(MLA_Q_RANK, MLA_HEADS, qd), ((0, 0), (0, 0), (0, HEAD_PAD - qd)))
    wuq = wuq.reshape(MLA_Q_RANK, MLA_HEADS * HEAD_PAD).astype(BF16)
    ukv = mla_w_ukv.reshape(MLA_KV_RANK, MLA_HEADS, MLA_NOPE + MLA_V)
    wk = jnp.pad(ukv[:, :, :MLA_NOPE], ((0, 0), (0, 0), (0, HEAD_PAD - MLA_NOPE)))
    wk = wk.reshape(MLA_KV_RANK, MLA_HEADS * HEAD_PAD).astype(BF16)
    wv = ukv[:, :, MLA_NOPE:].reshape(MLA_KV_RANK, MLA_HEADS * MLA_V).astype(BF16)
    bd = jax.scipy.linalg.block_diag(*[pool_w[g] for g in range(len(POOL_WINDOWS))]).astype(BF16)
    return wall, wuq, wk, wv, bd, w_out.astype(BF16)


def kernel(x, c, ctx, c_ctx, w_mod, b_mod, ffn1_norm, ffn1_w1, ffn1_w3, ffn1_w2, mix_norm, w_in, w_out, pool_w, pool_scale, mla_q_norm, mla_w_uq, mla_kv_norm, mla_w_ukv, diff_lambda, diff_subln, ffn2_norm, ffn2_w1, ffn2_w3, ffn2_w2, final_norm):
    bsz, seq, d = x.shape
    n_ctx = ctx.shape[1]
    depth = w_mod.shape[0]
    assert bsz + 1 <= MOD_ROWS and seq % GRID_W == 0
    tm, tq = 512, 256
    ctx_row = bsz

    cc = jnp.zeros((MOD_ROWS, d), F32).at[:bsz].set(c).at[ctx_row].set(c_ctx)
    mods_all = _modulation(cc, w_mod, b_mod).reshape(depth, MOD_ROWS, N_MOD, d)

    tab_lat, tab_ctx = _rope_tables(seq, n_ctx)
    icg, icx = _inv_counts(GRID_W), _inv_counts(n_ctx)
    scale_b = (MLA_NOPE + MLA_ROPE) ** -0.5
    scale_c = DIFF_DIM ** -0.5
    bf = lambda w: w.astype(BF16)

    h, hc = x, ctx
    for i in range(depth):
        last = i == depth - 1
        mods = mods_all[i]
        lam_init = 0.8 - 0.6 * math.exp(-0.3 * i)
        wall, wuq, wk, wv, bd, wo = _layer_weights(w_in[i], w_out[i], pool_w[i], mla_w_uq[i], mla_w_ukv[i])
        f1 = (ffn1_norm[i], bf(ffn1_w1[i]), bf(ffn1_w3[i]), bf(ffn1_w2[i]))
        f2 = (ffn2_norm[i], bf(ffn2_w1[i]), bf(ffn2_w3[i]), bf(ffn2_w2[i]))

        h = _ffn(h, mods, None, *f1, k0=0, tm=tm)
        hc = _ffn(hc, mods, ctx_row, *f1, k0=0, tm=n_ctx)

        pw = (mix_norm[i], wall, mla_q_norm[i], mla_kv_norm[i], wuq, wk, wv)
        u, qm, km, vm, qd, kd, vd = _proj(h, mods, None, *pw, tab_lat, tm=tm)
        uc, qmc, kmc, vmc, qdc, kdc, vdc = _proj(hc, mods, ctx_row, *pw, tab_ctx, tm=n_ctx)

        a, ac = _pool(u, uc, icg, icx, bd, pool_scale[i])
        mla = functools.partial(_mla_attn_kernel, scale=scale_b)
        dif = functools.partial(_diff_attn_kernel, scale=scale_c, lam_init=lam_init)
        dextra = (diff_lambda[i], diff_subln[i].reshape(1, DIFF_V))
        b = _attention(mla, "mla_attn", qm, (km, kmc), (vm, vmc), (), tq=tq)
        cdiff = _attention(dif, "diff_attn", qd, (kd, kdc), (vd, vdc), dextra, tq=tq)
        h = _ffn(h, mods, None, *f2, k0=6, tm=tm, mix=(a, b, cdiff, wo),
                 final_g=final_norm if last else None)
        if not last:
            bc = _attention(mla, "mla_attn_ctx", qmc, (kmc,), (vmc,), (), tq=n_ctx)
            cc_ = _attention(dif, "diff_attn_ctx", qdc, (kdc,), (vdc,), dextra, tq=n_ctx)
            hc = _ffn(hc, mods, ctx_row, *f2, k0=6, tm=n_ctx, mix=(ac, bc, cc_, wo))
    return h
```

```python
import functools
import math

import jax
import jax.numpy as jnp
from jax import lax
from jax.experimental import pallas as pl
from jax.experimental.pallas import tpu as pltpu

F32 = jnp.float32
BF16 = jnp.bfloat16

GRID_W = 64
N_MOD = 9
POOL_WINDOWS = (2, 4, 8, 16)
POOL_GROUP = 64
POOL_WIDTH = POOL_GROUP * len(POOL_WINDOWS)
POOL_HALF = max(POOL_WINDOWS) // 2
MLA_HEADS = 4
MLA_NOPE = 64
MLA_ROPE = 32
MLA_V = 64
MLA_Q_RANK = 384
MLA_KV_RANK = 256
DIFF_HEADS = 4
DIFF_DIM = 64
DIFF_V = 2 * DIFF_DIM
ROPE_BASE = 10000.0
EPS = 1e-6

LANES = 128
HEAD_PAD = LANES
MOD_ROWS = 24
VMEM_LIMIT = 56 * 1024 * 1024

C_POOL = 0
C_CQ = C_POOL + POOL_WIDTH
C_CKV = C_CQ + MLA_Q_RANK
C_KR = C_CKV + MLA_KV_RANK
C_DQ = C_KR + HEAD_PAD
C_DK = C_DQ + DIFF_HEADS * 2 * DIFF_DIM
C_DV = C_DK + DIFF_HEADS * 2 * DIFF_DIM

LOG2E = math.log2(math.e)
MLA_QSCALE = (MLA_NOPE + MLA_ROPE) ** -0.5 * LOG2E
DIFF_QSCALE = DIFF_DIM ** -0.5 * LOG2E


def _dot(a, b):
    return jnp.dot(a, b, preferred_element_type=F32)


def _dot_nt(a, b):
    return lax.dot_general(a, b, (((1,), (1,)), ((), ())), preferred_element_type=F32)


def _rms(x, g):
    return x * lax.rsqrt(jnp.mean(x * x, axis=-1, keepdims=True) + EPS) * g


def _silu(a):
    return a * (1.0 / (1.0 + jnp.exp(-a)))


def _const_spec(shape):
    return pl.BlockSpec(shape, lambda *_: (0,) * len(shape), pipeline_mode=pl.Buffered(1))


def _params(n_axes):
    return pltpu.CompilerParams(dimension_semantics=("parallel",) * n_axes, vmem_limit_bytes=VMEM_LIMIT)


def _mod_kernel(cc_ref, w_ref, b_ref, o_ref):
    sc = _silu(cc_ref[...])
    o_ref[0] = jnp.dot(sc, w_ref[0], preferred_element_type=F32, precision=lax.Precision.HIGHEST) + b_ref[0]


def _modulation(cc, w_mod, b_mod):
    depth, d, width = w_mod.shape
    bn = 9 * LANES
    return pl.pallas_call(
        _mod_kernel,
        grid=(depth, width // bn),
        in_specs=[
            pl.BlockSpec((MOD_ROWS, d), lambda l, j: (0, 0)),
            pl.BlockSpec((1, d, bn), lambda l, j: (l, 0, j)),
            pl.BlockSpec((1, 1, bn), lambda l, j: (l, 0, j)),
        ],
        out_specs=pl.BlockSpec((1, MOD_ROWS, bn), lambda l, j: (l, 0, j)),
        out_shape=jax.ShapeDtypeStruct((depth, MOD_ROWS, width), F32),
        compiler_params=_params(2),
        name="modulation",
    )(cc, w_mod, b_mod.reshape(depth, 1, width))


def _ffn_kernel(*refs, pre, final, k0, n_chunks):
    it = iter(refs)
    x_ref, m_ref = next(it), next(it)
    if pre:
        a_ref, b_ref, c_ref, wo_ref = next(it), next(it), next(it), next(it)
    g_ref, w1_ref, w3_ref, w2_ref = next(it), next(it), next(it), next(it)
    if final:
        fg_ref = next(it)
    o_ref, hmid_ref = next(it), next(it)

    x = x_ref[0]
    if pre:
        mix = jnp.concatenate([a_ref[0], b_ref[0], c_ref[0]], axis=-1)
        x = x + m_ref[0, 5:6, :] * _dot(mix, wo_ref[...])
    xm = _rms(x, g_ref[...]) * (1.0 + m_ref[0, k0 + 1:k0 + 2, :]) + m_ref[0, k0:k0 + 1, :]
    xb = xm.astype(BF16)
    fc = w1_ref.shape[1] // n_chunks
    for c in range(n_chunks):
        a = _dot(xb, w1_ref[:, c * fc:(c + 1) * fc])
        b = _dot(xb, w3_ref[:, c * fc:(c + 1) * fc])
        hmid_ref[:, c * fc:(c + 1) * fc] = (_silu(a) * b).astype(BF16)
    y = _dot(hmid_ref[...], w2_ref[...])
    out = x + (0.5 * m_ref[0, k0 + 2:k0 + 3, :]) * y
    if final:
        out = _rms(out, fg_ref[...])
    o_ref[0] = out


def _ffn(x, mods, mod_row, g, w1, w3, w2, *, k0, tm, mix=None, final_g=None):
    bsz, n, d = x.shape
    dff = w1.shape[1]
    pre, final = mix is not None, final_g is not None
    tok = lambda w: pl.BlockSpec((1, tm, w), lambda b, i: (b, i, 0))
    mod_spec = pl.BlockSpec((1, N_MOD, d), (lambda b, i: (b, 0, 0)) if mod_row is None else (lambda b, i: (mod_row, 0, 0)))
    args, specs = [x, mods], [tok(d), mod_spec]
    if pre:
        a, bb, c, wo = mix
        args += [a, bb, c, wo]
        specs += [tok(a.shape[-1]), tok(bb.shape[-1]), tok(c.shape[-1]), _const_spec(wo.shape)]
    args += [g.reshape(1, d), w1, w3, w2]
    specs += [_const_spec((1, d)), _const_spec(w1.shape), _const_spec(w3.shape), _const_spec(w2.shape)]
    if final:
        args.append(final_g.reshape(1, d))
        specs.append(_const_spec((1, d)))
    return pl.pallas_call(
        functools.partial(_ffn_kernel, pre=pre, final=final, k0=k0, n_chunks=2),
        grid=(bsz, n // tm),
        in_specs=specs,
        out_specs=tok(d),
        out_shape=jax.ShapeDtypeStruct(x.shape, F32),
        scratch_shapes=[pltpu.VMEM((tm, dff), BF16)],
        compiler_params=_params(2),
        name="ffn_mix" if pre else "ffn",
    )(*args)


def _rope(x, c, sa, sb, shift):
    return x * c + pltpu.roll(x, LANES - shift, 1) * sa + pltpu.roll(x, shift, 1) * sb


def _proj_kernel(x_ref, m_ref, g_ref, wall_ref, wdvt_ref, qn_ref, kvn_ref, wuq_ref, wk_ref, wvt_ref,
                 cm_ref, sam_ref, sbm_ref, cd_ref, sad_ref, sbd_ref,
                 u_ref, qm_ref, km_ref, vmt_ref, qd_ref, kd_ref, vdt_ref):
    x = x_ref[0]
    n = _rms(x, g_ref[...]) * (1.0 + m_ref[0, 4:5, :]) + m_ref[0, 3:4, :]
    nb = n.astype(BF16)
    p = _dot(nb, wall_ref[...])
    u_ref[0] = p[:, C_POOL:C_CQ]

    cqn = _rms(p[:, C_CQ:C_CKV], qn_ref[...]).astype(BF16)
    ckvn = _rms(p[:, C_CKV:C_KR], kvn_ref[...]).astype(BF16)
    q = _dot(cqn, wuq_ref[...])
    k = _dot(ckvn, wk_ref[...])
    vmt_ref[0, 0] = _dot_nt(wvt_ref[...], ckvn).astype(BF16)
    cm, sam, sbm = cm_ref[...], sam_ref[...], sbm_ref[...]
    half_m = MLA_ROPE // 2
    kr = _rope(p[:, C_KR:C_DQ], cm, sam, sbm, half_m)
    for h in range(MLA_HEADS):
        sl = slice(h * HEAD_PAD, (h + 1) * HEAD_PAD)
        qm_ref[0, :, sl] = (_rope(q[:, sl], cm, sam, sbm, half_m) * MLA_QSCALE).astype(BF16)
        km_ref[0, :, sl] = (k[:, sl] + kr).astype(BF16)

    cd, sad, sbd = cd_ref[...], sad_ref[...], sbd_ref[...]
    half_d = DIFF_DIM // 2
    for h in range(DIFF_HEADS):
        sl = slice(h * HEAD_PAD, (h + 1) * HEAD_PAD)
        qd = _rope(p[:, C_DQ + h * HEAD_PAD:C_DQ + (h + 1) * HEAD_PAD], cd, sad, sbd, half_d)
        qd_ref[0, :, sl] = (qd * DIFF_QSCALE).astype(BF16)
        kd_ref[0, :, sl] = _rope(p[:, C_DK + h * HEAD_PAD:C_DK + (h + 1) * HEAD_PAD], cd, sad, sbd, half_d).astype(BF16)
    vdt_ref[0, 0] = _dot_nt(wdvt_ref[...], nb).astype(BF16)


def _proj(x, mods, mod_row, g, wall, wdvt, qn, kvn, wuq, wk, wvt, tables, *, tm):
    bsz, n, d = x.shape
    tok = lambda w: pl.BlockSpec((1, tm, w), lambda b, i: (b, i, 0))
    vt = lambda r: pl.BlockSpec((1, 1, r, tm), lambda b, i: (b, i, 0, 0))
    mod_spec = pl.BlockSpec((1, N_MOD, d), (lambda b, i: (b, 0, 0)) if mod_row is None else (lambda b, i: (mod_row, 0, 0)))
    tab = pl.BlockSpec((tm, LANES), lambda b, i: (i, 0))
    wm, wd = MLA_HEADS * HEAD_PAD, DIFF_HEADS * HEAD_PAD
    rm, rd = MLA_HEADS * MLA_V, DIFF_HEADS * DIFF_V
    tok_sds = lambda w, dt: jax.ShapeDtypeStruct((bsz, n, w), dt)
    vt_sds = lambda r: jax.ShapeDtypeStruct((bsz, n // tm, r, tm), BF16)
    return pl.pallas_call(
        _proj_kernel,
        grid=(bsz, n // tm),
        in_specs=[tok(d), mod_spec, _const_spec((1, d)), _const_spec(wall.shape), _const_spec(wdvt.shape),
                  _const_spec((1, MLA_Q_RANK)), _const_spec((1, MLA_KV_RANK)),
                  _const_spec(wuq.shape), _const_spec(wk.shape), _const_spec(wvt.shape)] + [tab] * 6,
        out_specs=[tok(POOL_WIDTH), tok(wm), tok(wm), vt(rm), tok(wd), tok(wd), vt(rd)],
        out_shape=[tok_sds(POOL_WIDTH, F32), tok_sds(wm, BF16), tok_sds(wm, BF16), vt_sds(rm),
                   tok_sds(wd, BF16), tok_sds(wd, BF16), vt_sds(rd)],
        compiler_params=_params(2),
        name="mix_in",
    )(x, mods, g.reshape(1, d), wall, wdvt, qn.reshape(1, -1), kvn.reshape(1, -1), wuq, wk, wvt, *tables)


def _nested_window_sums(load, lane):
    a2 = load(-1) + load(0)
    a4 = a2 + load(-2) + load(1)
    a8 = a4 + load(-4) + load(-3) + load(2) + load(3)
    a16 = a8 + load(-8) + load(-7) + load(-6) + load(-5) + load(4) + load(5) + load(6) + load(7)
    return jnp.where(lane < POOL_GROUP, a2, jnp.where(lane < 2 * POOL_GROUP, a4, jnp.where(lane < 3 * POOL_GROUP, a8, a16)))


def _pool_kernel(ul_ref, uc_ref, icg_ref, icx_ref, bd_ref, ps_ref, al_ref, ac_ref,
                 z_ref, y_ref, dl_ref, zc_ref, *, rows):
    gw, pad = GRID_W, POOL_HALF
    stride = gw + 2 * pad
    n_ctx = uc_ref.shape[1]
    lane = lax.broadcasted_iota(jnp.int32, (gw, POOL_WIDTH), 1)

    z_ref[...] = jnp.zeros(z_ref.shape, F32)
    y_ref[...] = jnp.zeros(y_ref.shape, F32)

    def base_of(r):
        return pl.multiple_of((r + pad) * stride + pad, 8)

    def fill(r, carry):
        z_ref[pl.ds(base_of(r), gw), :] = ul_ref[0, pl.ds(pl.multiple_of(r * gw, 8), gw), :]
        return carry

    lax.fori_loop(0, rows, fill, 0)

    def row_pass(r, carry):
        base = base_of(r)
        s = _nested_window_sums(lambda d: z_ref[pl.ds(base + d * stride, gw), :], lane)
        y_ref[pl.ds(base, gw), :] = s * icg_ref[pl.ds(r, 1), :]
        return carry

    lax.fori_loop(0, rows, row_pass, 0)

    def col_pass(r, carry):
        slab = y_ref[pl.ds(pl.multiple_of((r + pad) * stride, 8), stride), :]
        s = _nested_window_sums(lambda d: slab[pad + d:pad + d + gw, :], lane)
        tok = pl.ds(pl.multiple_of(r * gw, 8), gw)
        dl_ref[tok, :] = (s * icg_ref[...] - ul_ref[0, tok, :]).astype(BF16)
        return carry

    lax.fori_loop(0, rows, col_pass, 0)
    al_ref[0] = (_dot(dl_ref[...], bd_ref[...]) * ps_ref[...]).astype(BF16)

    zc_ref[...] = jnp.zeros(zc_ref.shape, F32)
    uc = uc_ref[0]
    zc_ref[pad:pad + n_ctx, :] = uc
    lane_c = lax.broadcasted_iota(jnp.int32, (n_ctx, POOL_WIDTH), 1)
    sc = _nested_window_sums(lambda d: zc_ref[pad + d:pad + d + n_ctx, :], lane_c)
    dc = (sc * icx_ref[...] - uc).astype(BF16)
    ac_ref[0] = (_dot(dc, bd_ref[...]) * ps_ref[...]).astype(BF16)


def _pool(u_lat, u_ctx, icg, icx, bd, ps):
    bsz, s, w = u_lat.shape
    n_ctx = u_ctx.shape[1]
    rows = s // GRID_W
    stride = GRID_W + 2 * POOL_HALF
    padded = (rows + 2 * POOL_HALF) * stride
    return pl.pallas_call(
        functools.partial(_pool_kernel, rows=rows),
        grid=(bsz,),
        in_specs=[pl.BlockSpec((1, s, w), lambda b: (b, 0, 0)), pl.BlockSpec((1, n_ctx, w), lambda b: (b, 0, 0)),
                  _const_spec(icg.shape), _const_spec(icx.shape), _const_spec(bd.shape), _const_spec((1, w))],
        out_specs=[pl.BlockSpec((1, s, w), lambda b: (b, 0, 0)), pl.BlockSpec((1, n_ctx, w), lambda b: (b, 0, 0))],
        out_shape=[jax.ShapeDtypeStruct((bsz, s, w), BF16), jax.ShapeDtypeStruct((bsz, n_ctx, w), BF16)],
        scratch_shapes=[pltpu.VMEM((padded, w), F32), pltpu.VMEM((padded, w), F32),
                        pltpu.VMEM((s, w), BF16), pltpu.VMEM((n_ctx + 2 * POOL_HALF, w), F32)],
        compiler_params=_params(1),
        name="pool",
    )(u_lat, u_ctx, icg, icx, bd, ps.reshape(1, w))


CHUNK_UNROLL = 8


def _attn_units(n_units, dv, q_of, k_refs, vt_refs, s_ref, mx_ref, l_ref, acc_ref, finish):
    n = s_ref.shape[2]
    offs, off = [], 0
    for k_ref in k_refs:
        offs.append(off)
        off += k_ref.shape[1]

    def for_chunks(fn):
        for k_ref, vt_ref, o in zip(k_refs, vt_refs, offs):
            n_chunks, kc = vt_ref.shape[1], vt_ref.shape[3]
            if n_chunks == 1:
                fn(k_ref, vt_ref, 0, slice(0, kc), slice(o, o + kc))
            else:
                def body(c, carry):
                    r = pl.multiple_of(c * kc, kc)
                    fn(k_ref, vt_ref, c, pl.ds(r, kc), pl.ds(o + r, kc))
                    return carry

                lax.fori_loop(0, n_chunks, body, 0, unroll=CHUNK_UNROLL)

    def scores(u, slot):
        ksl = slice(u * HEAD_PAD, (u + 1) * HEAD_PAD)

        def fn(k_ref, vt_ref, c, krows, srows):
            st = _dot_nt(k_ref[0, krows, ksl], q_of(u))
            s_ref[slot, srows, :] = st
            mx_ref[slot] = jnp.maximum(mx_ref[slot], jnp.max(st.reshape(-1, 8, n), axis=0))

        return fn

    def probs(u, slot, m):
        vsl = slice(u * dv, (u + 1) * dv)

        def fn(k_ref, vt_ref, c, krows, srows):
            p = jnp.exp2(s_ref[slot, srows, :] - m)
            l_ref[...] += jnp.sum(p.reshape(-1, 8, n), axis=0)
            acc_ref[...] += _dot(vt_ref[0, c, vsl, :], p.astype(BF16))

        return fn

    def both(f, g):
        def fn(*a):
            f(*a)
            g(*a)

        return fn

    neg_inf = jnp.full(mx_ref.shape[1:], -jnp.inf, F32)
    mx_ref[0] = neg_inf
    for_chunks(scores(0, 0))
    for u in range(n_units):
        slot = u % 2
        m = jnp.max(mx_ref[slot], axis=0, keepdims=True)
        l_ref[...] = jnp.zeros(l_ref.shape, F32)
        acc_ref[...] = jnp.zeros(acc_ref.shape, F32)
        if u + 1 < n_units:
            mx_ref[1 - slot] = neg_inf
            for_chunks(both(scores(u + 1, 1 - slot), probs(u, slot, m)))
        else:
            for_chunks(probs(u, slot, m))
        finish(u, acc_ref[...], jnp.sum(l_ref[...], axis=0, keepdims=True))


def _mla_attn_kernel(*refs, nseg):
    q_ref, k_refs, vt_refs = refs[0], refs[1:1 + nseg], refs[1 + nseg:1 + 2 * nseg]
    o_ref, s_ref, mx_ref, l_ref, acc_ref, ot_ref = refs[1 + 2 * nseg:]

    def finish(h, acc, l):
        ot_ref[h * MLA_V:(h + 1) * MLA_V, :] = acc * (1.0 / l)

    _attn_units(MLA_HEADS, MLA_V, lambda h: q_ref[0, :, h * HEAD_PAD:(h + 1) * HEAD_PAD],
                k_refs, vt_refs, s_ref, mx_ref, l_ref, acc_ref, finish)
    o_ref[0] = ot_ref[...].T.astype(BF16)


def _diff_attn_kernel(*refs, nseg, lam_init):
    dl_ref, sub_ref, q_ref = refs[0], refs[1], refs[2]
    k_refs, vt_refs = refs[3:3 + nseg], refs[3 + nseg:3 + 2 * nseg]
    o_ref, s_ref, mx_ref, l_ref, acc_ref, qu_ref = refs[3 + 2 * nseg:]
    tq = q_ref.shape[1]
    dl = dl_ref[...]
    lam = (jnp.exp(jnp.sum(dl[0:1] * dl[1:2], axis=-1, keepdims=True))
           - jnp.exp(jnp.sum(dl[2:3] * dl[3:4], axis=-1, keepdims=True)) + lam_init)
    lane = lax.broadcasted_iota(jnp.int32, (tq, LANES), 1)
    zero = jnp.zeros((), BF16)
    for h in range(DIFF_HEADS):
        qh = q_ref[0, :, h * HEAD_PAD:(h + 1) * HEAD_PAD]
        qu_ref[h, :tq, :] = jnp.where(lane < DIFF_DIM, qh, zero)
        qu_ref[h, tq:, :] = jnp.where(lane >= DIFF_DIM, qh, zero)

    def finish(h, acc, l):
        o = acc[:, :tq] * (1.0 / l[:, :tq]) - acc[:, tq:] * (lam / l[:, tq:])
        on = o * lax.rsqrt(jnp.mean(o * o, axis=0, keepdims=True) + EPS)
        o_ref[0, :, h * DIFF_V:(h + 1) * DIFF_V] = (on.T * sub_ref[...] * (1.0 - lam_init)).astype(BF16)

    _attn_units(DIFF_HEADS, DIFF_V, lambda h: qu_ref[h], k_refs, vt_refs, s_ref, mx_ref, l_ref, acc_ref, finish)


def _attention(kernel, name, q, ks, vts, extra, *, tq, dv, maps_per_head):
    bsz, nq, wq = q.shape
    nseg = len(ks)
    n_heads = wq // HEAD_PAD
    whole = lambda a: pl.BlockSpec((1,) + a.shape[1:], lambda b, i: (b,) + (0,) * (a.ndim - 1))
    w_out = vts[0].shape[2]
    n_keys = sum(k.shape[1] for k in ks)
    n = tq * maps_per_head
    scratch = [pltpu.VMEM((2, n_keys, n), F32), pltpu.VMEM((2, 8, n), F32), pltpu.VMEM((8, n), F32),
               pltpu.VMEM((dv, n), F32)]
    scratch.append(pltpu.VMEM((n_heads, n, HEAD_PAD), BF16) if maps_per_head > 1 else pltpu.VMEM((w_out, n), F32))
    return pl.pallas_call(
        functools.partial(kernel, nseg=nseg),
        grid=(bsz, nq // tq),
        in_specs=[_const_spec(e.shape) for e in extra] + [pl.BlockSpec((1, tq, wq), lambda b, i: (b, i, 0))]
        + [whole(k) for k in ks] + [whole(v) for v in vts],
        out_specs=pl.BlockSpec((1, tq, w_out), lambda b, i: (b, i, 0)),
        out_shape=jax.ShapeDtypeStruct((bsz, nq, w_out), BF16),
        scratch_shapes=scratch,
        compiler_params=_params(2),
        name=name,
    )(*extra, q, *ks, *vts)


def _axial_angles(n_tok, rot_dim):
    t = jnp.arange(n_tok)
    row = (t // GRID_W).astype(F32)
    col = (t % GRID_W).astype(F32)
    n_axis = rot_dim // 4
    inv = ROPE_BASE ** (-jnp.arange(n_axis, dtype=F32) / n_axis)
    return jnp.concatenate([row[:, None] * inv, col[:, None] * inv], axis=-1)


def _rope_tables(n_tok, n_ctx):
    am, ad = _axial_angles(n_tok, MLA_ROPE), _axial_angles(n_tok, DIFF_DIM)
    one = lambda w: jnp.ones((n_tok, w), F32)
    zero = lambda w: jnp.zeros((n_tok, w), F32)
    cm = jnp.concatenate([one(MLA_NOPE), jnp.cos(am), jnp.cos(am), one(HEAD_PAD - MLA_NOPE - MLA_ROPE)], axis=-1)
    sam = jnp.concatenate([zero(MLA_NOPE), -jnp.sin(am), zero(HEAD_PAD - MLA_NOPE - MLA_ROPE // 2)], axis=-1)
    sbm = jnp.concatenate([zero(MLA_NOPE + MLA_ROPE // 2), jnp.sin(am), zero(HEAD_PAD - MLA_NOPE - MLA_ROPE)], axis=-1)
    cd = jnp.tile(jnp.cos(ad), (1, 4))
    sad = jnp.tile(jnp.concatenate([-jnp.sin(ad), zero(DIFF_DIM // 2)], axis=-1), (1, 2))
    sbd = jnp.tile(jnp.concatenate([zero(DIFF_DIM // 2), jnp.sin(ad)], axis=-1), (1, 2))
    lat = (cm, sam, sbm, cd, sad, sbd)
    ident = (jnp.ones((n_ctx, LANES), F32),) + (jnp.zeros((n_ctx, LANES), F32),) * 2
    return lat, ident + ident


def _inv_counts(n):
    idx = jnp.arange(n)
    cols = []
    for k in POOL_WINDOWS:
        lo, hi = k // 2, k - 1 - k // 2
        cnt = (jnp.clip(idx + hi + 1, 0, n) - jnp.clip(idx - lo, 0, n)).astype(F32)
        cols.append(jnp.broadcast_to((1.0 / cnt)[:, None], (n, POOL_GROUP)))
    return jnp.concatenate(cols, axis=-1)


def _layer_weights(w_in, w_out, pool_w, mla_w_uq, mla_w_ukv):
    d = w_in.shape[0]
    s1, s2 = POOL_WIDTH, POOL_WIDTH + MLA_Q_RANK + MLA_KV_RANK + MLA_ROPE
    kr_pad = jnp.zeros((d, HEAD_PAD), F32).at[:, MLA_NOPE:MLA_NOPE + MLA_ROPE].set(w_in[:, s2 - MLA_ROPE:s2])
    s3 = s2 + 2 * DIFF_HEADS * 2 * DIFF_DIM
    wall = jnp.concatenate([w_in[:, :s2 - MLA_ROPE], kr_pad, w_in[:, s2:s3]], axis=-1).astype(BF16)
    wdvt = w_in[:, s3:].T.astype(BF16)
    qd = MLA_NOPE + MLA_ROPE
    wuq = jnp.pad(mla_w_uq.reshape(MLA_Q_RANK, MLA_HEADS, qd), ((0, 0), (0, 0), (0, HEAD_PAD - qd)))
    wuq = wuq.reshape(MLA_Q_RANK, MLA_HEADS * HEAD_PAD).astype(BF16)
    ukv = mla_w_ukv.reshape(MLA_KV_RANK, MLA_HEADS, MLA_NOPE + MLA_V)
    wk = jnp.pad(ukv[:, :, :MLA_NOPE], ((0, 0), (0, 0), (0, HEAD_PAD - MLA_NOPE)))
    wk = wk.reshape(MLA_KV_RANK, MLA_HEADS * HEAD_PAD).astype(BF16)
    wvt = ukv[:, :, MLA_NOPE:].reshape(MLA_KV_RANK, MLA_HEADS * MLA_V).T.astype(BF16)
    bd = jax.scipy.linalg.block_diag(*[pool_w[g] for g in range(len(POOL_WINDOWS))]).astype(BF16)
    return wall, wdvt, wuq, wk, wvt, bd, w_out.astype(BF16)


def kernel(x, c, ctx, c_ctx, w_mod, b_mod, ffn1_norm, ffn1_w1, ffn1_w3, ffn1_w2, mix_norm, w_in, w_out, pool_w, pool_scale, mla_q_norm, mla_w_uq, mla_kv_norm, mla_w_ukv, diff_lambda, diff_subln, ffn2_norm, ffn2_w1, ffn2_w3, ffn2_w2, final_norm):
    bsz, seq, d = x.shape
    n_ctx = ctx.shape[1]
    depth = w_mod.shape[0]
    assert bsz + 1 <= MOD_ROWS and seq % GRID_W == 0
    tm, tq = 512, 256
    ctx_row = bsz

    cc = jnp.zeros((MOD_ROWS, d), F32).at[:bsz].set(c).at[ctx_row].set(c_ctx)
    mods_all = _modulation(cc, w_mod, b_mod).reshape(depth, MOD_ROWS, N_MOD, d)

    tab_lat, tab_ctx = _rope_tables(seq, n_ctx)
    icg, icx = _inv_counts(GRID_W), _inv_counts(n_ctx)
    bf = lambda w: w.astype(BF16)
    mla_args = dict(dv=MLA_V, maps_per_head=1)
    dif_args = dict(dv=DIFF_V, maps_per_head=2)

    h, hc = x, ctx
    for i in range(depth):
        last = i == depth - 1
        mods = mods_all[i]
        lam_init = 0.8 - 0.6 * math.exp(-0.3 * i)
        wall, wdvt, wuq, wk, wvt, bd, wo = _layer_weights(w_in[i], w_out[i], pool_w[i], mla_w_uq[i], mla_w_ukv[i])
        f1 = (ffn1_norm[i], bf(ffn1_w1[i]), bf(ffn1_w3[i]), bf(ffn1_w2[i]))
        f2 = (ffn2_norm[i], bf(ffn2_w1[i]), bf(ffn2_w3[i]), bf(ffn2_w2[i]))

        h = _ffn(h, mods, None, *f1, k0=0, tm=tm)
        hc = _ffn(hc, mods, ctx_row, *f1, k0=0, tm=n_ctx)

        pw = (mix_norm[i], wall, wdvt, mla_q_norm[i], mla_kv_norm[i], wuq, wk, wvt)
        u, qm, km, vm, qd, kd, vd = _proj(h, mods, None, *pw, tab_lat, tm=tm)
        uc, qmc, kmc, vmc, qdc, kdc, vdc = _proj(hc, mods, ctx_row, *pw, tab_ctx, tm=n_ctx)

        a, ac = _pool(u, uc, icg, icx, bd, pool_scale[i])
        dif = functools.partial(_diff_attn_kernel, lam_init=lam_init)
        dextra = (diff_lambda[i], diff_subln[i].reshape(1, DIFF_V))
        b = _attention(_mla_attn_kernel, "mla_attn", qm, (km, kmc), (vm, vmc), (), tq=2 * tq, **mla_args)
        cdiff = _attention(dif, "diff_attn", qd, (kd, kdc), (vd, vdc), dextra, tq=tq, **dif_args)
        h = _ffn(h, mods, None, *f2, k0=6, tm=tm, mix=(a, b, cdiff, wo),
                 final_g=final_norm if last else None)
        if not last:
            bc = _attention(_mla_attn_kernel, "mla_attn_ctx", qmc, (kmc,), (vmc,), (), tq=n_ctx, **mla_args)
            cc_ = _attention(dif, "diff_attn_ctx", qdc, (kdc,), (vdc,), dextra, tq=n_ctx, **dif_args)
            hc = _ffn(hc, mods, ctx_row, *f2, k0=6, tm=n_ctx, mix=(ac, bc, cc_, wo))
    return h
```

```python
import functools
import math

import jax
import jax.numpy as jnp
from jax import lax
from jax.experimental import pallas as pl
from jax.experimental.pallas import tpu as pltpu

F32 = jnp.float32
BF16 = jnp.bfloat16

GRID_W = 64
N_MOD = 9
POOL_WINDOWS = (2, 4, 8, 16)
POOL_GROUP = 64
POOL_WIDTH = POOL_GROUP * len(POOL_WINDOWS)
POOL_HALF = max(POOL_WINDOWS) // 2
MLA_HEADS = 4
MLA_NOPE = 64
MLA_ROPE = 32
MLA_V = 64
MLA_Q_RANK = 384
MLA_KV_RANK = 256
DIFF_HEADS = 4
DIFF_DIM = 64
DIFF_V = 2 * DIFF_DIM
ROPE_BASE = 10000.0
EPS = 1e-6

LANES = 128
HEAD_PAD = LANES
SUM_ROWS = 16
MLA_VT_ROWS = MLA_V + SUM_ROWS
MOD_ROWS = 24
VMEM_LIMIT = 56 * 1024 * 1024

C_POOL = 0
C_CQ = C_POOL + POOL_WIDTH
C_CKV = C_CQ + MLA_Q_RANK
C_KR = C_CKV + MLA_KV_RANK
C_DQ = C_KR + HEAD_PAD
C_DK = C_DQ + DIFF_HEADS * 2 * DIFF_DIM
C_DV = C_DK + DIFF_HEADS * 2 * DIFF_DIM

LOG2E = math.log2(math.e)
MLA_QSCALE = (MLA_NOPE + MLA_ROPE) ** -0.5 * LOG2E
DIFF_QSCALE = DIFF_DIM ** -0.5 * LOG2E


def _dot(a, b):
    return jnp.dot(a, b, preferred_element_type=F32)


def _dot_nt(a, b):
    return lax.dot_general(a, b, (((1,), (1,)), ((), ())), preferred_element_type=F32)


def _rms(x, g):
    return x * lax.rsqrt(jnp.mean(x * x, axis=-1, keepdims=True) + EPS) * g


def _silu(a):
    return a * (1.0 / (1.0 + jnp.exp(-a)))


def _const_spec(shape):
    return pl.BlockSpec(shape, lambda *_: (0,) * len(shape), pipeline_mode=pl.Buffered(1))


def _params(n_axes):
    return pltpu.CompilerParams(dimension_semantics=("parallel",) * n_axes, vmem_limit_bytes=VMEM_LIMIT)


def _mod_kernel(cc_ref, w_ref, b_ref, o_ref):
    sc = _silu(cc_ref[...])
    o_ref[0] = jnp.dot(sc, w_ref[0], preferred_element_type=F32, precision=lax.Precision.HIGHEST) + b_ref[0]


def _modulation(cc, w_mod, b_mod):
    depth, d, width = w_mod.shape
    bn = 9 * LANES
    return pl.pallas_call(
        _mod_kernel,
        grid=(depth, width // bn),
        in_specs=[
            pl.BlockSpec((MOD_ROWS, d), lambda l, j: (0, 0)),
            pl.BlockSpec((1, d, bn), lambda l, j: (l, 0, j)),
            pl.BlockSpec((1, 1, bn), lambda l, j: (l, 0, j)),
        ],
        out_specs=pl.BlockSpec((1, MOD_ROWS, bn), lambda l, j: (l, 0, j)),
        out_shape=jax.ShapeDtypeStruct((depth, MOD_ROWS, width), F32),
        compiler_params=_params(2),
        name="modulation",
    )(cc, w_mod, b_mod.reshape(depth, 1, width))


MXU_COLS = 256
FFN_CHUNK = MXU_COLS


def _ffn_kernel(*refs, pre, final, k0):
    it = iter(refs)
    x_ref, m_ref = next(it), next(it)
    if pre:
        a_ref, b_ref, c_ref, wo_ref = next(it), next(it), next(it), next(it)
    g_ref, w1_ref, w3_ref, w2_ref = next(it), next(it), next(it), next(it)
    if final:
        fg_ref = next(it)
    o_ref, hmid_ref = next(it), next(it)

    x = x_ref[0]
    if pre:
        mix = jnp.concatenate([a_ref[0], b_ref[0], c_ref[0]], axis=-1)
        x = x + m_ref[0, 5:6, :] * _dot(mix, wo_ref[...])
    xm = _rms(x, g_ref[...]) * (1.0 + m_ref[0, k0 + 1:k0 + 2, :]) + m_ref[0, k0:k0 + 1, :]
    xb = xm.astype(BF16)
    dff = w1_ref.shape[1]
    for lo in range(0, dff, FFN_CHUNK):
        cols = slice(lo, min(lo + FFN_CHUNK, dff))
        a = _dot(xb, w1_ref[:, cols])
        b = _dot(xb, w3_ref[:, cols])
        hmid_ref[:, cols] = (_silu(a) * b).astype(BF16)
    y = _dot(hmid_ref[...], w2_ref[...])
    out = x + (0.5 * m_ref[0, k0 + 2:k0 + 3, :]) * y
    if final:
        out = _rms(out, fg_ref[...])
    o_ref[0] = out


def _ffn(x, mods, mod_row, g, w1, w3, w2, *, k0, tm, mix=None, final_g=None):
    bsz, n, d = x.shape
    dff = w1.shape[1]
    pre, final = mix is not None, final_g is not None
    tok = lambda w: pl.BlockSpec((1, tm, w), lambda b, i: (b, i, 0))
    mod_spec = pl.BlockSpec((1, N_MOD, d), (lambda b, i: (b, 0, 0)) if mod_row is None else (lambda b, i: (mod_row, 0, 0)))
    args, specs = [x, mods], [tok(d), mod_spec]
    if pre:
        a, bb, c, wo = mix
        args += [a, bb, c, wo]
        specs += [tok(a.shape[-1]), tok(bb.shape[-1]), tok(c.shape[-1]), _const_spec(wo.shape)]
    args += [g.reshape(1, d), w1, w3, w2]
    specs += [_const_spec((1, d)), _const_spec(w1.shape), _const_spec(w3.shape), _const_spec(w2.shape)]
    if final:
        args.append(final_g.reshape(1, d))
        specs.append(_const_spec((1, d)))
    return pl.pallas_call(
        functools.partial(_ffn_kernel, pre=pre, final=final, k0=k0),
        grid=(bsz, n // tm),
        in_specs=specs,
        out_specs=tok(d),
        out_shape=jax.ShapeDtypeStruct(x.shape, F32),
        scratch_shapes=[pltpu.VMEM((tm, dff), BF16)],
        compiler_params=_params(2),
        name="ffn_mix" if pre else "ffn",
    )(*args)


def _rope(x, c, sa, sb, shift):
    return x * c + pltpu.roll(x, LANES - shift, 1) * sa + pltpu.roll(x, shift, 1) * sb


def _proj_kernel(x_ref, m_ref, g_ref, wall_ref, wdvt_ref, qn_ref, kvn_ref, wuq_ref, wk_ref, wvt_ref,
                 cm_ref, sam_ref, sbm_ref, cd_ref, sad_ref, sbd_ref,
                 u_ref, qm_ref, km_ref, vmt_ref, qd_ref, kd_ref, vdt_ref):
    x = x_ref[0]
    n = _rms(x, g_ref[...]) * (1.0 + m_ref[0, 4:5, :]) + m_ref[0, 3:4, :]
    nb = n.astype(BF16)
    p = _dot(nb, wall_ref[...])
    u_ref[0] = p[:, C_POOL:C_CQ]

    cqn = _rms(p[:, C_CQ:C_CKV], qn_ref[...]).astype(BF16)
    ckvn = _rms(p[:, C_CKV:C_KR], kvn_ref[...]).astype(BF16)
    q = _dot(cqn, wuq_ref[...])
    k = _dot(ckvn, wk_ref[...])
    vt = _dot_nt(wvt_ref[...], ckvn)
    ones = jnp.ones((SUM_ROWS, vt.shape[1]), BF16)
    for h in range(MLA_HEADS):
        vmt_ref[0, 0, h * MLA_VT_ROWS:h * MLA_VT_ROWS + MLA_V, :] = vt[h * MLA_V:(h + 1) * MLA_V].astype(BF16)
        vmt_ref[0, 0, h * MLA_VT_ROWS + MLA_V:(h + 1) * MLA_VT_ROWS, :] = ones
    cm, sam, sbm = cm_ref[...], sam_ref[...], sbm_ref[...]
    half_m = MLA_ROPE // 2
    kr = _rope(p[:, C_KR:C_DQ], cm, sam, sbm, half_m)
    for h in range(MLA_HEADS):
        sl = slice(h * HEAD_PAD, (h + 1) * HEAD_PAD)
        qm_ref[0, :, sl] = (_rope(q[:, sl], cm, sam, sbm, half_m) * MLA_QSCALE).astype(BF16)
        km_ref[0, :, sl] = (k[:, sl] + kr).astype(BF16)

    cd, sad, sbd = cd_ref[...], sad_ref[...], sbd_ref[...]
    half_d = DIFF_DIM // 2
    for h in range(DIFF_HEADS):
        sl = slice(h * HEAD_PAD, (h + 1) * HEAD_PAD)
        qd = _rope(p[:, C_DQ + h * HEAD_PAD:C_DQ + (h + 1) * HEAD_PAD], cd, sad, sbd, half_d)
        qd_ref[0, :, sl] = (qd * DIFF_QSCALE).astype(BF16)
        kd_ref[0, :, sl] = _rope(p[:, C_DK + h * HEAD_PAD:C_DK + (h + 1) * HEAD_PAD], cd, sad, sbd, half_d).astype(BF16)
    vdt_ref[0, 0] = _dot_nt(wdvt_ref[...], nb).astype(BF16)


def _proj(x, mods, mod_row, g, wall, wdvt, qn, kvn, wuq, wk, wvt, tables, *, tm):
    bsz, n, d = x.shape
    tok = lambda w: pl.BlockSpec((1, tm, w), lambda b, i: (b, i, 0))
    vt = lambda r: pl.BlockSpec((1, 1, r, tm), lambda b, i: (b, i, 0, 0))
    mod_spec = pl.BlockSpec((1, N_MOD, d), (lambda b, i: (b, 0, 0)) if mod_row is None else (lambda b, i: (mod_row, 0, 0)))
    tab = pl.BlockSpec((tm, LANES), lambda b, i: (i, 0))
    wm, wd = MLA_HEADS * HEAD_PAD, DIFF_HEADS * HEAD_PAD
    rm, rd = MLA_HEADS * MLA_VT_ROWS, DIFF_HEADS * DIFF_V
    tok_sds = lambda w, dt: jax.ShapeDtypeStruct((bsz, n, w), dt)
    vt_sds = lambda r: jax.ShapeDtypeStruct((bsz, n // tm, r, tm), BF16)
    return pl.pallas_call(
        _proj_kernel,
        grid=(bsz, n // tm),
        in_specs=[tok(d), mod_spec, _const_spec((1, d)), _const_spec(wall.shape), _const_spec(wdvt.shape),
                  _const_spec((1, MLA_Q_RANK)), _const_spec((1, MLA_KV_RANK)),
                  _const_spec(wuq.shape), _const_spec(wk.shape), _const_spec(wvt.shape)] + [tab] * 6,
        out_specs=[tok(POOL_WIDTH), tok(wm), tok(wm), vt(rm), tok(wd), tok(wd), vt(rd)],
        out_shape=[tok_sds(POOL_WIDTH, F32), tok_sds(wm, BF16), tok_sds(wm, BF16), vt_sds(rm),
                   tok_sds(wd, BF16), tok_sds(wd, BF16), vt_sds(rd)],
        compiler_params=_params(2),
        name="mix_in",
    )(x, mods, g.reshape(1, d), wall, wdvt, qn.reshape(1, -1), kvn.reshape(1, -1), wuq, wk, wvt, *tables)


def _nested_window_sums(load, lane):
    a2 = load(-1) + load(0)
    a4 = a2 + load(-2) + load(1)
    a8 = a4 + load(-4) + load(-3) + load(2) + load(3)
    a16 = a8 + load(-8) + load(-7) + load(-6) + load(-5) + load(4) + load(5) + load(6) + load(7)
    return jnp.where(lane < POOL_GROUP, a2, jnp.where(lane < 2 * POOL_GROUP, a4, jnp.where(lane < 3 * POOL_GROUP, a8, a16)))


def _window_sums_1d(slab, n, lane):
    size = slab.shape[0]
    ahead = lambda x, k: pltpu.roll(x, size - k, 0)
    behind = lambda x, k: pltpu.roll(x, k, 0)[POOL_HALF:POOL_HALF + n]
    p2 = slab + ahead(slab, 1)
    p4 = p2 + ahead(p2, 2)
    p8 = p4 + ahead(p4, 4)
    p16 = p8 + ahead(p8, 8)
    return jnp.where(lane < POOL_GROUP, behind(p2, 1),
                     jnp.where(lane < 2 * POOL_GROUP, behind(p4, 2),
                               jnp.where(lane < 3 * POOL_GROUP, behind(p8, 4), p16[:n])))


def _pool_kernel(ul_ref, uc_ref, icg_ref, icx_ref, bd_ref, ps_ref, al_ref, ac_ref,
                 z_ref, y_ref, dl_ref, zc_ref, *, rows):
    gw, pad = GRID_W, POOL_HALF
    stride = gw + 2 * pad
    n_ctx = uc_ref.shape[1]
    lane = lax.broadcasted_iota(jnp.int32, (gw, POOL_WIDTH), 1)
    zero_rows = jnp.zeros((pad * gw, POOL_WIDTH), F32)
    zero_pad = jnp.zeros((pad, POOL_WIDTH), F32)

    z_ref[0:pad * gw, :] = zero_rows
    z_ref[pad * gw:(rows + pad) * gw, :] = ul_ref[0]
    z_ref[(rows + pad) * gw:(rows + 2 * pad) * gw, :] = zero_rows

    def row_pass(r, carry):
        base = pl.multiple_of((r + pad) * gw, 8)
        s = _nested_window_sums(lambda d: z_ref[pl.ds(base + d * gw, gw), :], lane)
        yb = pl.multiple_of(r * stride, 8)
        y_ref[pl.ds(yb, pad), :] = zero_pad
        y_ref[pl.ds(yb + pad, gw), :] = s * icg_ref[pl.ds(r, 1), :]
        y_ref[pl.ds(yb + pad + gw, pad), :] = zero_pad
        return carry

    lax.fori_loop(0, rows, row_pass, 0)

    def col_pass(r, carry):
        slab = y_ref[pl.ds(pl.multiple_of(r * stride, 8), stride), :]
        s = _window_sums_1d(slab, gw, lane)
        tok = pl.ds(pl.multiple_of(r * gw, 8), gw)
        dl_ref[tok, :] = (s * icg_ref[...] - ul_ref[0, tok, :]).astype(BF16)
        return carry

    lax.fori_loop(0, rows, col_pass, 0)
    al_ref[0] = (_dot(dl_ref[...], bd_ref[...]) * ps_ref[...]).astype(BF16)

    uc = uc_ref[0]
    zc_ref[0:pad, :] = zero_pad
    zc_ref[pad:pad + n_ctx, :] = uc
    zc_ref[pad + n_ctx:pad + n_ctx + pad, :] = zero_pad
    lane_c = lax.broadcasted_iota(jnp.int32, (n_ctx, POOL_WIDTH), 1)
    sc = _window_sums_1d(zc_ref[...], n_ctx, lane_c)
    dc = (sc * icx_ref[...] - uc).astype(BF16)
    ac_ref[0] = (_dot(dc, bd_ref[...]) * ps_ref[...]).astype(BF16)


def _pool(u_lat, u_ctx, icg, icx, bd, ps):
    bsz, s, w = u_lat.shape
    n_ctx = u_ctx.shape[1]
    rows = s // GRID_W
    stride = GRID_W + 2 * POOL_HALF
    return pl.pallas_call(
        functools.partial(_pool_kernel, rows=rows),
        grid=(bsz,),
        in_specs=[pl.BlockSpec((1, s, w), lambda b: (b, 0, 0)), pl.BlockSpec((1, n_ctx, w), lambda b: (b, 0, 0)),
                  _const_spec(icg.shape), _const_spec(icx.shape), _const_spec(bd.shape), _const_spec((1, w))],
        out_specs=[pl.BlockSpec((1, s, w), lambda b: (b, 0, 0)), pl.BlockSpec((1, n_ctx, w), lambda b: (b, 0, 0))],
        out_shape=[jax.ShapeDtypeStruct((bsz, s, w), BF16), jax.ShapeDtypeStruct((bsz, n_ctx, w), BF16)],
        scratch_shapes=[pltpu.VMEM(((rows + 2 * POOL_HALF) * GRID_W, w), F32), pltpu.VMEM((rows * stride, w), F32),
                        pltpu.VMEM((s, w), BF16), pltpu.VMEM((n_ctx + 2 * POOL_HALF, w), F32)],
        compiler_params=_params(1),
        name="pool",
    )(u_lat, u_ctx, icg, icx, bd, ps.reshape(1, w))


def _attn_units(heads, vrows, q_of, k_refs, vt_refs, s_refs, mx_refs, l_ref, acc_ref, finish, unroll):
    n_units = len(heads)
    n = s_refs[0].shape[1]
    offs, off = [], 0
    for k_ref in k_refs:
        offs.append(off)
        off += k_ref.shape[1]

    def for_chunks(fn):
        for k_ref, vt_ref, o in zip(k_refs, vt_refs, offs):
            n_chunks, kc = vt_ref.shape[1], vt_ref.shape[3]
            if n_chunks == 1:
                fn(k_ref, vt_ref, 0, slice(0, kc), slice(o, o + kc))
            else:
                def body(c, carry):
                    r = pl.multiple_of(c * kc, kc)
                    fn(k_ref, vt_ref, c, pl.ds(r, kc), pl.ds(o + r, kc))
                    return carry

                lax.fori_loop(0, n_chunks, body, 0, unroll=unroll)

    def scores(u, slot):
        ksl = slice(heads[u] * HEAD_PAD, (heads[u] + 1) * HEAD_PAD)

        def fn(k_ref, vt_ref, c, krows, srows):
            st = _dot_nt(k_ref[0, krows, ksl], q_of(u))
            s_refs[slot][srows, :] = st
            mx_refs[slot][...] = jnp.maximum(mx_refs[slot][...], jnp.max(st.reshape(-1, 8, n), axis=0))

        return fn

    def probs(u, slot, m):
        vsl = slice(heads[u] * vrows, (heads[u] + 1) * vrows)

        def fn(k_ref, vt_ref, c, krows, srows):
            p = jnp.exp2(s_refs[slot][srows, :] - m)
            if l_ref is not None:
                l_ref[...] += jnp.sum(p.reshape(-1, 8, n), axis=0)
            acc_ref[...] += _dot(vt_ref[0, c, vsl, :], p.astype(BF16))

        return fn

    def both(f, g):
        def fn(*a):
            f(*a)
            g(*a)

        return fn

    neg_inf = jnp.full(mx_refs[0].shape, -jnp.inf, F32)
    mx_refs[0][...] = neg_inf
    for_chunks(scores(0, 0))
    for u in range(n_units):
        slot = u % 2
        m = jnp.max(mx_refs[slot][...], axis=0, keepdims=True)
        if l_ref is not None:
            l_ref[...] = jnp.zeros(l_ref.shape, F32)
        acc_ref[...] = jnp.zeros(acc_ref.shape, F32)
        if u + 1 < n_units:
            mx_refs[1 - slot][...] = neg_inf
            for_chunks(both(scores(u + 1, 1 - slot), probs(u, slot, m)))
        else:
            for_chunks(probs(u, slot, m))
        finish(u, acc_ref[...], None if l_ref is None else jnp.sum(l_ref[...], axis=0, keepdims=True))


def _mla_attn_kernel(*refs, nseg, sub, unroll):
    q_ref, k_refs, vt_refs = refs[0], refs[1:1 + nseg], refs[1 + nseg:1 + 2 * nseg]
    o_ref, s0_ref, s1_ref, mx0_ref, mx1_ref, acc_ref, ot_ref = refs[1 + 2 * nseg:]
    units = [(sb, h) for sb in range(q_ref.shape[1] // sub) for h in range(MLA_HEADS)]

    def q_of(u):
        sb, h = units[u]
        return q_ref[0, sb * sub:(sb + 1) * sub, h * HEAD_PAD:(h + 1) * HEAD_PAD]

    def finish(u, acc, l):
        sb, h = units[u]
        ot_ref[h * MLA_V:(h + 1) * MLA_V, sb * sub:(sb + 1) * sub] = acc[:MLA_V] * (1.0 / acc[MLA_V:MLA_V + 1])

    _attn_units([h for _, h in units], MLA_VT_ROWS, q_of, k_refs, vt_refs,
                (s0_ref, s1_ref), (mx0_ref, mx1_ref), None, acc_ref, finish, unroll)
    o_ref[0] = ot_ref[...].T.astype(BF16)


def _diff_attn_kernel(*refs, nseg, sub, unroll, lam_init):
    dl_ref, sub_ref, q_ref = refs[0], refs[1], refs[2]
    k_refs, vt_refs = refs[3:3 + nseg], refs[3 + nseg:3 + 2 * nseg]
    o_ref, s0_ref, s1_ref, mx0_ref, mx1_ref, l_ref, acc_ref, qu_ref = refs[3 + 2 * nseg:]
    units = [(sb, h) for sb in range(q_ref.shape[1] // sub) for h in range(DIFF_HEADS)]
    dl = dl_ref[...]
    lam = (jnp.exp(jnp.sum(dl[0:1] * dl[1:2], axis=-1, keepdims=True))
           - jnp.exp(jnp.sum(dl[2:3] * dl[3:4], axis=-1, keepdims=True)) + lam_init)
    lane = lax.broadcasted_iota(jnp.int32, (sub, LANES), 1)
    zero = jnp.zeros((), BF16)
    for u, (sb, h) in enumerate(units):
        qh = q_ref[0, sb * sub:(sb + 1) * sub, h * HEAD_PAD:(h + 1) * HEAD_PAD]
        qu_ref[u, :sub, :] = jnp.where(lane < DIFF_DIM, qh, zero)
        qu_ref[u, sub:, :] = jnp.where(lane >= DIFF_DIM, qh, zero)

    def finish(u, acc, l):
        sb, h = units[u]
        o = acc[:, :sub] * (1.0 / l[:, :sub]) - acc[:, sub:] * (lam / l[:, sub:])
        on = o * lax.rsqrt(jnp.mean(o * o, axis=0, keepdims=True) + EPS)
        o_ref[0, sb * sub:(sb + 1) * sub, h * DIFF_V:(h + 1) * DIFF_V] = (
            on.T * sub_ref[...] * (1.0 - lam_init)).astype(BF16)

    _attn_units([h for _, h in units], DIFF_V, lambda u: qu_ref[u], k_refs, vt_refs,
                (s0_ref, s1_ref), (mx0_ref, mx1_ref), l_ref, acc_ref, finish, unroll)


def _attention(kernel, name, q, ks, vts, extra, *, tq, sub, unroll, w_out, maps_per_head):
    bsz, nq, wq = q.shape
    nseg = len(ks)
    n_heads = wq // HEAD_PAD
    whole = lambda a: pl.BlockSpec((1,) + a.shape[1:], lambda b, i: (b,) + (0,) * (a.ndim - 1))
    vrows = vts[0].shape[2] // n_heads
    n_keys = sum(k.shape[1] for k in ks)
    n = sub * maps_per_head
    scratch = [pltpu.VMEM((n_keys, n), F32)] * 2 + [pltpu.VMEM((8, n), F32)] * 2
    if maps_per_head > 1:
        scratch += [pltpu.VMEM((8, n), F32), pltpu.VMEM((vrows, n), F32),
                    pltpu.VMEM((n_heads * tq // sub, n, HEAD_PAD), BF16)]
    else:
        scratch += [pltpu.VMEM((vrows, n), F32), pltpu.VMEM((w_out, tq), F32)]
    return pl.pallas_call(
        functools.partial(kernel, nseg=nseg, sub=sub, unroll=unroll),
        grid=(bsz, nq // tq),
        in_specs=[_const_spec(e.shape) for e in extra] + [pl.BlockSpec((1, tq, wq), lambda b, i: (b, i, 0))]
        + [whole(k) for k in ks] + [whole(v) for v in vts],
        out_specs=pl.BlockSpec((1, tq, w_out), lambda b, i: (b, i, 0)),
        out_shape=jax.ShapeDtypeStruct((bsz, nq, w_out), BF16),
        scratch_shapes=scratch,
        compiler_params=_params(2),
        name=name,
    )(*extra, q, *ks, *vts)


def _axial_angles(n_tok, rot_dim):
    t = jnp.arange(n_tok)
    row = (t // GRID_W).astype(F32)
    col = (t % GRID_W).astype(F32)
    n_axis = rot_dim // 4
    inv = ROPE_BASE ** (-jnp.arange(n_axis, dtype=F32) / n_axis)
    return jnp.concatenate([row[:, None] * inv, col[:, None] * inv], axis=-1)


def _rope_tables(n_tok, n_ctx):
    am, ad = _axial_angles(n_tok, MLA_ROPE), _axial_angles(n_tok, DIFF_DIM)
    one = lambda w: jnp.ones((n_tok, w), F32)
    zero = lambda w: jnp.zeros((n_tok, w), F32)
    cm = jnp.concatenate([one(MLA_NOPE), jnp.cos(am), jnp.cos(am), one(HEAD_PAD - MLA_NOPE - MLA_ROPE)], axis=-1)
    sam = jnp.concatenate([zero(MLA_NOPE), -jnp.sin(am), zero(HEAD_PAD - MLA_NOPE - MLA_ROPE // 2)], axis=-1)
    sbm = jnp.concatenate([zero(MLA_NOPE + MLA_ROPE // 2), jnp.sin(am), zero(HEAD_PAD - MLA_NOPE - MLA_ROPE)], axis=-1)
    cd = jnp.tile(jnp.cos(ad), (1, 4))
    sad = jnp.tile(jnp.concatenate([-jnp.sin(ad), zero(DIFF_DIM // 2)], axis=-1), (1, 2))
    sbd = jnp.tile(jnp.concatenate([zero(DIFF_DIM // 2), jnp.sin(ad)], axis=-1), (1, 2))
    lat = (cm, sam, sbm, cd, sad, sbd)
    ident = (jnp.ones((n_ctx, LANES), F32),) + (jnp.zeros((n_ctx, LANES), F32),) * 2
    return lat, ident + ident


def _inv_counts(n):
    idx = jnp.arange(n)
    cols = []
    for k in POOL_WINDOWS:
        lo, hi = k // 2, k - 1 - k // 2
        cnt = (jnp.clip(idx + hi + 1, 0, n) - jnp.clip(idx - lo, 0, n)).astype(F32)
        cols.append(jnp.broadcast_to((1.0 / cnt)[:, None], (n, POOL_GROUP)))
    return jnp.concatenate(cols, axis=-1)


def _layer_weights(w_in, w_out, pool_w, mla_w_uq, mla_w_ukv):
    d = w_in.shape[0]
    s1, s2 = POOL_WIDTH, POOL_WIDTH + MLA_Q_RANK + MLA_KV_RANK + MLA_ROPE
    kr_pad = jnp.zeros((d, HEAD_PAD), F32).at[:, MLA_NOPE:MLA_NOPE + MLA_ROPE].set(w_in[:, s2 - MLA_ROPE:s2])
    s3 = s2 + 2 * DIFF_HEADS * 2 * DIFF_DIM
    wall = jnp.concatenate([w_in[:, :s2 - MLA_ROPE], kr_pad, w_in[:, s2:s3]], axis=-1).astype(BF16)
    wdvt = w_in[:, s3:].T.astype(BF16)
    qd = MLA_NOPE + MLA_ROPE
    wuq = jnp.pad(mla_w_uq.reshape(MLA_Q_RANK, MLA_HEADS, qd), ((0, 0), (0, 0), (0, HEAD_PAD - qd)))
    wuq = wuq.reshape(MLA_Q_RANK, MLA_HEADS * HEAD_PAD).astype(BF16)
    ukv = mla_w_ukv.reshape(MLA_KV_RANK, MLA_HEADS, MLA_NOPE + MLA_V)
    wk = jnp.pad(ukv[:, :, :MLA_NOPE], ((0, 0), (0, 0), (0, HEAD_PAD - MLA_NOPE)))
    wk = wk.reshape(MLA_KV_RANK, MLA_HEADS * HEAD_PAD).astype(BF16)
    wvt = ukv[:, :, MLA_NOPE:].reshape(MLA_KV_RANK, MLA_HEADS * MLA_V).T.astype(BF16)
    bd = jax.scipy.linalg.block_diag(*[pool_w[g] for g in range(len(POOL_WINDOWS))]).astype(BF16)
    return wall, wdvt, wuq, wk, wvt, bd, w_out.astype(BF16)


def kernel(x, c, ctx, c_ctx, w_mod, b_mod, ffn1_norm, ffn1_w1, ffn1_w3, ffn1_w2, mix_norm, w_in, w_out, pool_w, pool_scale, mla_q_norm, mla_w_uq, mla_kv_norm, mla_w_ukv, diff_lambda, diff_subln, ffn2_norm, ffn2_w1, ffn2_w3, ffn2_w2, final_norm):
    bsz, seq, d = x.shape
    n_ctx = ctx.shape[1]
    depth = w_mod.shape[0]
    assert bsz + 1 <= MOD_ROWS and seq % GRID_W == 0
    tm = 512
    ctx_row = bsz

    cc = jnp.zeros((MOD_ROWS, d), F32).at[:bsz].set(c).at[ctx_row].set(c_ctx)
    mods_all = _modulation(cc, w_mod, b_mod).reshape(depth, MOD_ROWS, N_MOD, d)

    tab_lat, tab_ctx = _rope_tables(seq, n_ctx)
    icg, icx = _inv_counts(GRID_W), _inv_counts(n_ctx)
    bf = lambda w: w.astype(BF16)
    mla_args = dict(w_out=MLA_HEADS * MLA_V, maps_per_head=1)
    dif_args = dict(w_out=DIFF_HEADS * DIFF_V, maps_per_head=2)

    h, hc = x, ctx
    for i in range(depth):
        last = i == depth - 1
        mods = mods_all[i]
        lam_init = 0.8 - 0.6 * math.exp(-0.3 * i)
        wall, wdvt, wuq, wk, wvt, bd, wo = _layer_weights(w_in[i], w_out[i], pool_w[i], mla_w_uq[i], mla_w_ukv[i])
        f1 = (ffn1_norm[i], bf(ffn1_w1[i]), bf(ffn1_w3[i]), bf(ffn1_w2[i]))
        f2 = (ffn2_norm[i], bf(ffn2_w1[i]), bf(ffn2_w3[i]), bf(ffn2_w2[i]))

        h = _ffn(h, mods, None, *f1, k0=0, tm=tm)
        hc = _ffn(hc, mods, ctx_row, *f1, k0=0, tm=n_ctx)

        pw = (mix_norm[i], wall, wdvt, mla_q_norm[i], mla_kv_norm[i], wuq, wk, wvt)
        u, qm, km, vm, qd, kd, vd = _proj(h, mods, None, *pw, tab_lat, tm=tm)
        uc, qmc, kmc, vmc, qdc, kdc, vdc = _proj(hc, mods, ctx_row, *pw, tab_ctx, tm=n_ctx)

        a, ac = _pool(u, uc, icg, icx, bd, pool_scale[i])
        dif = functools.partial(_diff_attn_kernel, lam_init=lam_init)
        dextra = (diff_lambda[i], diff_subln[i].reshape(1, DIFF_V))
        b = _attention(_mla_attn_kernel, "mla_attn", qm, (km, kmc), (vm, vmc), (),
                       tq=1024, sub=512, unroll=4, **mla_args)
        cdiff = _attention(dif, "diff_attn", qd, (kd, kdc), (vd, vdc), dextra,
                           tq=512, sub=256, unroll=8, **dif_args)
        h = _ffn(h, mods, None, *f2, k0=6, tm=tm, mix=(a, b, cdiff, wo),
                 final_g=final_norm if last else None)
        if not last:
            bc = _attention(_mla_attn_kernel, "mla_attn_ctx", qmc, (kmc,), (vmc,), (),
                            tq=n_ctx, sub=n_ctx, unroll=1, **mla_args)
            cc_ = _attention(dif, "diff_attn_ctx", qdc, (kdc,), (vdc,), dextra,
                             tq=n_ctx, sub=n_ctx, unroll=1, **dif_args)
            hc = _ffn(hc, mods, ctx_row, *f2, k0=6, tm=n_ctx, mix=(ac, bc, cc_, wo))
    return h
```

```python
import functools
import math

import jax
import jax.numpy as jnp
from jax import lax
from jax.experimental import pallas as pl
from jax.experimental.pallas import tpu as pltpu

F32 = jnp.float32
BF16 = jnp.bfloat16

GRID_W = 64
N_MOD = 9
POOL_WINDOWS = (2, 4, 8, 16)
POOL_GROUP = 64
POOL_WIDTH = POOL_GROUP * len(POOL_WINDOWS)
POOL_HALF = max(POOL_WINDOWS) // 2
MLA_HEADS = 4
MLA_NOPE = 64
MLA_ROPE = 32
MLA_V = 64
MLA_Q_RANK = 384
MLA_KV_RANK = 256
DIFF_HEADS = 4
DIFF_DIM = 64
DIFF_V = 2 * DIFF_DIM
ROPE_BASE = 10000.0
EPS = 1e-6

LANES = 128
HEAD_PAD = LANES
SUM_ROWS = 16
MLA_VT_ROWS = MLA_V + SUM_ROWS
MOD_ROWS = 24
VMEM_LIMIT = 56 * 1024 * 1024

C_POOL = 0
C_CQ = C_POOL + POOL_WIDTH
C_CKV = C_CQ + MLA_Q_RANK
C_KR = C_CKV + MLA_KV_RANK
C_DQ = C_KR + HEAD_PAD
C_DK = C_DQ + DIFF_HEADS * 2 * DIFF_DIM
C_DV = C_DK + DIFF_HEADS * 2 * DIFF_DIM

LOG2E = math.log2(math.e)
MLA_QSCALE = (MLA_NOPE + MLA_ROPE) ** -0.5 * LOG2E
DIFF_QSCALE = DIFF_DIM ** -0.5 * LOG2E


def _dot(a, b):
    return jnp.dot(a, b, preferred_element_type=F32)


def _dot_nt(a, b):
    return lax.dot_general(a, b, (((1,), (1,)), ((), ())), preferred_element_type=F32)


def _rms(x, g):
    return x * lax.rsqrt(jnp.mean(x * x, axis=-1, keepdims=True) + EPS) * g


def _silu(a):
    return a * (1.0 / (1.0 + jnp.exp(-a)))


def _const_spec(shape):
    return pl.BlockSpec(shape, lambda *_: (0,) * len(shape), pipeline_mode=pl.Buffered(1))


def _params(n_axes):
    return pltpu.CompilerParams(dimension_semantics=("parallel",) * n_axes, vmem_limit_bytes=VMEM_LIMIT)


def _mod_kernel(cc_ref, w_ref, b_ref, o_ref):
    sc = _silu(cc_ref[...])
    o_ref[0] = jnp.dot(sc, w_ref[0], preferred_element_type=F32, precision=lax.Precision.HIGHEST) + b_ref[0]


def _modulation(cc, w_mod, b_mod):
    depth, d, width = w_mod.shape
    bn = 9 * LANES
    return pl.pallas_call(
        _mod_kernel,
        grid=(depth, width // bn),
        in_specs=[
            pl.BlockSpec((MOD_ROWS, d), lambda l, j: (0, 0)),
            pl.BlockSpec((1, d, bn), lambda l, j: (l, 0, j)),
            pl.BlockSpec((1, 1, bn), lambda l, j: (l, 0, j)),
        ],
        out_specs=pl.BlockSpec((1, MOD_ROWS, bn), lambda l, j: (l, 0, j)),
        out_shape=jax.ShapeDtypeStruct((depth, MOD_ROWS, width), F32),
        compiler_params=_params(2),
        name="modulation",
    )(cc, w_mod, b_mod.reshape(depth, 1, width))


MXU_COLS = 256
FFN_CHUNK = MXU_COLS


def _ffn_kernel(*refs, pre, final, k0):
    it = iter(refs)
    x_ref, m_ref = next(it), next(it)
    if pre:
        a_ref, b_ref, c_ref, wo_ref = next(it), next(it), next(it), next(it)
    g_ref, w1_ref, w3_ref, w2_ref = next(it), next(it), next(it), next(it)
    if final:
        fg_ref = next(it)
    o_ref, hmid_ref = next(it), next(it)

    x = x_ref[0]
    if pre:
        mix = jnp.concatenate([a_ref[0], b_ref[0], c_ref[0]], axis=-1)
        x = x + m_ref[0, 5:6, :] * _dot(mix, wo_ref[...])
    xm = _rms(x, g_ref[...]) * (1.0 + m_ref[0, k0 + 1:k0 + 2, :]) + m_ref[0, k0:k0 + 1, :]
    xb = xm.astype(BF16)
    dff = w1_ref.shape[1]
    for lo in range(0, dff, FFN_CHUNK):
        cols = slice(lo, min(lo + FFN_CHUNK, dff))
        a = _dot(xb, w1_ref[:, cols])
        b = _dot(xb, w3_ref[:, cols])
        hmid_ref[:, cols] = (_silu(a) * b).astype(BF16)
    y = _dot(hmid_ref[...], w2_ref[...])
    out = x + (0.5 * m_ref[0, k0 + 2:k0 + 3, :]) * y
    if final:
        out = _rms(out, fg_ref[...])
    o_ref[0] = out


def _ffn(x, mods, mod_row, g, w1, w3, w2, *, k0, tm, mix=None, final_g=None):
    bsz, n, d = x.shape
    dff = w1.shape[1]
    pre, final = mix is not None, final_g is not None
    tok = lambda w: pl.BlockSpec((1, tm, w), lambda b, i: (b, i, 0))
    mod_spec = pl.BlockSpec((1, N_MOD, d), (lambda b, i: (b, 0, 0)) if mod_row is None else (lambda b, i: (mod_row, 0, 0)))
    args, specs = [x, mods], [tok(d), mod_spec]
    if pre:
        a, bb, c, wo = mix
        args += [a, bb, c, wo]
        specs += [tok(a.shape[-1]), tok(bb.shape[-1]), tok(c.shape[-1]), _const_spec(wo.shape)]
    args += [g.reshape(1, d), w1, w3, w2]
    specs += [_const_spec((1, d)), _const_spec(w1.shape), _const_spec(w3.shape), _const_spec(w2.shape)]
    if final:
        args.append(final_g.reshape(1, d))
        specs.append(_const_spec((1, d)))
    return pl.pallas_call(
        functools.partial(_ffn_kernel, pre=pre, final=final, k0=k0),
        grid=(bsz, n // tm),
        in_specs=specs,
        out_specs=tok(d),
        out_shape=jax.ShapeDtypeStruct(x.shape, F32),
        scratch_shapes=[pltpu.VMEM((tm, dff), BF16)],
        compiler_params=_params(2),
        name="ffn_mix" if pre else "ffn",
    )(*args)


def _rope(x, c, sa, sb, shift):
    return x * c + pltpu.roll(x, LANES - shift, 1) * sa + pltpu.roll(x, shift, 1) * sb


def _proj_kernel(x_ref, m_ref, g_ref, wall_ref, wdvt_ref, qn_ref, kvn_ref, wuq_ref, wk_ref, wvt_ref,
                 cm_ref, sam_ref, sbm_ref, cd_ref, sad_ref, sbd_ref,
                 u_ref, qm_ref, km_ref, vmt_ref, qd_ref, kd_ref, vdt_ref):
    x = x_ref[0]
    n = _rms(x, g_ref[...]) * (1.0 + m_ref[0, 4:5, :]) + m_ref[0, 3:4, :]
    nb = n.astype(BF16)
    p = _dot(nb, wall_ref[...])
    u_ref[0] = p[:, C_POOL:C_CQ]

    cqn = _rms(p[:, C_CQ:C_CKV], qn_ref[...]).astype(BF16)
    ckvn = _rms(p[:, C_CKV:C_KR], kvn_ref[...]).astype(BF16)
    q = _dot(cqn, wuq_ref[...])
    k = _dot(ckvn, wk_ref[...])
    vt = _dot_nt(wvt_ref[...], ckvn)
    ones = jnp.ones((SUM_ROWS, vt.shape[1]), BF16)
    for h in range(MLA_HEADS):
        vmt_ref[0, 0, h * MLA_VT_ROWS:h * MLA_VT_ROWS + MLA_V, :] = vt[h * MLA_V:(h + 1) * MLA_V].astype(BF16)
        vmt_ref[0, 0, h * MLA_VT_ROWS + MLA_V:(h + 1) * MLA_VT_ROWS, :] = ones
    cm, sam, sbm = cm_ref[...], sam_ref[...], sbm_ref[...]
    half_m = MLA_ROPE // 2
    kr = _rope(p[:, C_KR:C_DQ], cm, sam, sbm, half_m)
    for h in range(MLA_HEADS):
        sl = slice(h * HEAD_PAD, (h + 1) * HEAD_PAD)
        qm_ref[0, :, sl] = (_rope(q[:, sl], cm, sam, sbm, half_m) * MLA_QSCALE).astype(BF16)
        km_ref[0, :, sl] = (k[:, sl] + kr).astype(BF16)

    cd, sad, sbd = cd_ref[...], sad_ref[...], sbd_ref[...]
    half_d = DIFF_DIM // 2
    for h in range(DIFF_HEADS):
        sl = slice(h * HEAD_PAD, (h + 1) * HEAD_PAD)
        qd = _rope(p[:, C_DQ + h * HEAD_PAD:C_DQ + (h + 1) * HEAD_PAD], cd, sad, sbd, half_d)
        qd_ref[0, :, sl] = (qd * DIFF_QSCALE).astype(BF16)
        kd_ref[0, :, sl] = _rope(p[:, C_DK + h * HEAD_PAD:C_DK + (h + 1) * HEAD_PAD], cd, sad, sbd, half_d).astype(BF16)
    vdt_ref[0, 0] = _dot_nt(wdvt_ref[...], nb).astype(BF16)


def _proj(x, mods, mod_row, g, wall, wdvt, qn, kvn, wuq, wk, wvt, tables, *, tm):
    bsz, n, d = x.shape
    tok = lambda w: pl.BlockSpec((1, tm, w), lambda b, i: (b, i, 0))
    vt = lambda r: pl.BlockSpec((1, 1, r, tm), lambda b, i: (b, i, 0, 0))
    mod_spec = pl.BlockSpec((1, N_MOD, d), (lambda b, i: (b, 0, 0)) if mod_row is None else (lambda b, i: (mod_row, 0, 0)))
    tab = pl.BlockSpec((tm, LANES), lambda b, i: (i, 0))
    wm, wd = MLA_HEADS * HEAD_PAD, DIFF_HEADS * HEAD_PAD
    rm, rd = MLA_HEADS * MLA_VT_ROWS, DIFF_HEADS * DIFF_V
    tok_sds = lambda w, dt: jax.ShapeDtypeStruct((bsz, n, w), dt)
    vt_sds = lambda r: jax.ShapeDtypeStruct((bsz, n // tm, r, tm), BF16)
    return pl.pallas_call(
        _proj_kernel,
        grid=(bsz, n // tm),
        in_specs=[tok(d), mod_spec, _const_spec((1, d)), _const_spec(wall.shape), _const_spec(wdvt.shape),
                  _const_spec((1, MLA_Q_RANK)), _const_spec((1, MLA_KV_RANK)),
                  _const_spec(wuq.shape), _const_spec(wk.shape), _const_spec(wvt.shape)] + [tab] * 6,
        out_specs=[tok(POOL_WIDTH), tok(wm), tok(wm), vt(rm), tok(wd), tok(wd), vt(rd)],
        out_shape=[tok_sds(POOL_WIDTH, F32), tok_sds(wm, BF16), tok_sds(wm, BF16), vt_sds(rm),
                   tok_sds(wd, BF16), tok_sds(wd, BF16), vt_sds(rd)],
        compiler_params=_params(2),
        name="mix_in",
    )(x, mods, g.reshape(1, d), wall, wdvt, qn.reshape(1, -1), kvn.reshape(1, -1), wuq, wk, wvt, *tables)


def _nested_window_sums(load, lane):
    a2 = load(-1) + load(0)
    a4 = a2 + load(-2) + load(1)
    a8 = a4 + load(-4) + load(-3) + load(2) + load(3)
    a16 = a8 + load(-8) + load(-7) + load(-6) + load(-5) + load(4) + load(5) + load(6) + load(7)
    return jnp.where(lane < POOL_GROUP, a2, jnp.where(lane < 2 * POOL_GROUP, a4, jnp.where(lane < 3 * POOL_GROUP, a8, a16)))


def _window_sums_1d(slab, n, lane):
    size = slab.shape[0]
    ahead = lambda x, k: pltpu.roll(x, size - k, 0)
    behind = lambda x, k: pltpu.roll(x, k, 0)[POOL_HALF:POOL_HALF + n]
    p2 = slab + ahead(slab, 1)
    p4 = p2 + ahead(p2, 2)
    p8 = p4 + ahead(p4, 4)
    p16 = p8 + ahead(p8, 8)
    return jnp.where(lane < POOL_GROUP, behind(p2, 1),
                     jnp.where(lane < 2 * POOL_GROUP, behind(p4, 2),
                               jnp.where(lane < 3 * POOL_GROUP, behind(p8, 4), p16[:n])))


def _pool_kernel(ul_ref, uc_ref, icg_ref, icx_ref, bd_ref, ps_ref, al_ref, ac_ref,
                 z_ref, y_ref, dl_ref, zc_ref, *, rows):
    gw, pad = GRID_W, POOL_HALF
    stride = gw + 2 * pad
    n_ctx = uc_ref.shape[1]
    lane = lax.broadcasted_iota(jnp.int32, (gw, POOL_WIDTH), 1)
    zero_rows = jnp.zeros((pad * gw, POOL_WIDTH), F32)
    zero_pad = jnp.zeros((pad, POOL_WIDTH), F32)

    z_ref[0:pad * gw, :] = zero_rows
    z_ref[pad * gw:(rows + pad) * gw, :] = ul_ref[0]
    z_ref[(rows + pad) * gw:(rows + 2 * pad) * gw, :] = zero_rows

    def row_pass(r, carry):
        base = pl.multiple_of((r + pad) * gw, 8)
        s = _nested_window_sums(lambda d: z_ref[pl.ds(base + d * gw, gw), :], lane)
        yb = pl.multiple_of(r * stride, 8)
        y_ref[pl.ds(yb, pad), :] = zero_pad
        y_ref[pl.ds(yb + pad, gw), :] = s * icg_ref[pl.ds(r, 1), :]
        y_ref[pl.ds(yb + pad + gw, pad), :] = zero_pad
        return carry

    lax.fori_loop(0, rows, row_pass, 0)

    def col_pass(r, carry):
        slab = y_ref[pl.ds(pl.multiple_of(r * stride, 8), stride), :]
        s = _window_sums_1d(slab, gw, lane)
        tok = pl.ds(pl.multiple_of(r * gw, 8), gw)
        dl_ref[tok, :] = (s * icg_ref[...] - ul_ref[0, tok, :]).astype(BF16)
        return carry

    lax.fori_loop(0, rows, col_pass, 0)
    al_ref[0] = (_dot(dl_ref[...], bd_ref[...]) * ps_ref[...]).astype(BF16)

    uc = uc_ref[0]
    zc_ref[0:pad, :] = zero_pad
    zc_ref[pad:pad + n_ctx, :] = uc
    zc_ref[pad + n_ctx:pad + n_ctx + pad, :] = zero_pad
    lane_c = lax.broadcasted_iota(jnp.int32, (n_ctx, POOL_WIDTH), 1)
    sc = _window_sums_1d(zc_ref[...], n_ctx, lane_c)
    dc = (sc * icx_ref[...] - uc).astype(BF16)
    ac_ref[0] = (_dot(dc, bd_ref[...]) * ps_ref[...]).astype(BF16)


def _pool(u_lat, u_ctx, icg, icx, bd, ps):
    bsz, s, w = u_lat.shape
    n_ctx = u_ctx.shape[1]
    rows = s // GRID_W
    stride = GRID_W + 2 * POOL_HALF
    return pl.pallas_call(
        functools.partial(_pool_kernel, rows=rows),
        grid=(bsz,),
        in_specs=[pl.BlockSpec((1, s, w), lambda b: (b, 0, 0)), pl.BlockSpec((1, n_ctx, w), lambda b: (b, 0, 0)),
                  _const_spec(icg.shape), _const_spec(icx.shape), _const_spec(bd.shape), _const_spec((1, w))],
        out_specs=[pl.BlockSpec((1, s, w), lambda b: (b, 0, 0)), pl.BlockSpec((1, n_ctx, w), lambda b: (b, 0, 0))],
        out_shape=[jax.ShapeDtypeStruct((bsz, s, w), BF16), jax.ShapeDtypeStruct((bsz, n_ctx, w), BF16)],
        scratch_shapes=[pltpu.VMEM(((rows + 2 * POOL_HALF) * GRID_W, w), F32), pltpu.VMEM((rows * stride, w), F32),
                        pltpu.VMEM((s, w), BF16), pltpu.VMEM((n_ctx + 2 * POOL_HALF, w), F32)],
        compiler_params=_params(1),
        name="pool",
    )(u_lat, u_ctx, icg, icx, bd, ps.reshape(1, w))


def _attn_units(once_ref, heads, vrows, q_of, k_refs, vt_refs, s_refs, mx_refs, l_ref, acc_ref, finish, unroll):
    n_units = len(heads)
    n = s_refs[0].shape[1]
    offs, off = [], 0
    for k_ref in k_refs:
        offs.append(off)
        off += k_ref.shape[1]

    def for_chunks(fn):
        if once_ref is not None:
            def whole(t, carry):
                for k_ref, vt_ref, o in zip(k_refs, vt_refs, offs):
                    n_chunks, kc = vt_ref.shape[1], vt_ref.shape[3]
                    for c in range(n_chunks):
                        fn(k_ref, vt_ref, c, slice(c * kc, (c + 1) * kc), slice(o + c * kc, o + (c + 1) * kc))
                return carry

            lax.fori_loop(0, once_ref[0], whole, 0)
            return
        for k_ref, vt_ref, o in zip(k_refs, vt_refs, offs):
            n_chunks, kc = vt_ref.shape[1], vt_ref.shape[3]
            if n_chunks == 1:
                fn(k_ref, vt_ref, 0, slice(0, kc), slice(o, o + kc))
            else:
                def body(c, carry):
                    r = pl.multiple_of(c * kc, kc)
                    fn(k_ref, vt_ref, c, pl.ds(r, kc), pl.ds(o + r, kc))
                    return carry

                lax.fori_loop(0, n_chunks, body, 0, unroll=unroll)

    def scores(u, slot):
        ksl = slice(heads[u] * HEAD_PAD, (heads[u] + 1) * HEAD_PAD)

        def fn(k_ref, vt_ref, c, krows, srows):
            st = _dot_nt(k_ref[0, krows, ksl], q_of(u))
            s_refs[slot][srows, :] = st
            mx_refs[slot][...] = jnp.maximum(mx_refs[slot][...], jnp.max(st.reshape(-1, 8, n), axis=0))

        return fn

    def probs(u, slot, m):
        vsl = slice(heads[u] * vrows, (heads[u] + 1) * vrows)

        def fn(k_ref, vt_ref, c, krows, srows):
            p = jnp.exp2(s_refs[slot][srows, :] - m)
            if l_ref is not None:
                l_ref[...] += jnp.sum(p.reshape(-1, 8, n), axis=0)
            acc_ref[...] += _dot(vt_ref[0, c, vsl, :], p.astype(BF16))

        return fn

    def both(f, g):
        def fn(*a):
            f(*a)
            g(*a)

        return fn

    neg_inf = jnp.full(mx_refs[0].shape, -jnp.inf, F32)
    mx_refs[0][...] = neg_inf
    for_chunks(scores(0, 0))
    for u in range(n_units):
        slot = u % 2
        m = jnp.max(mx_refs[slot][...], axis=0, keepdims=True)
        if l_ref is not None:
            l_ref[...] = jnp.zeros(l_ref.shape, F32)
        acc_ref[...] = jnp.zeros(acc_ref.shape, F32)
        if u + 1 < n_units:
            mx_refs[1 - slot][...] = neg_inf
            for_chunks(both(scores(u + 1, 1 - slot), probs(u, slot, m)))
        else:
            for_chunks(probs(u, slot, m))
        finish(u, acc_ref[...], None if l_ref is None else jnp.sum(l_ref[...], axis=0, keepdims=True))


def _mla_attn_kernel(*refs, nseg, sub, unroll):
    once_ref, refs = (refs[0], refs[1:]) if unroll is None else (None, refs)
    q_ref, k_refs, vt_refs = refs[0], refs[1:1 + nseg], refs[1 + nseg:1 + 2 * nseg]
    o_ref, s0_ref, s1_ref, mx0_ref, mx1_ref, acc_ref, ot_ref = refs[1 + 2 * nseg:]
    units = [(sb, h) for sb in range(q_ref.shape[1] // sub) for h in range(MLA_HEADS)]

    def q_of(u):
        sb, h = units[u]
        return q_ref[0, sb * sub:(sb + 1) * sub, h * HEAD_PAD:(h + 1) * HEAD_PAD]

    def finish(u, acc, l):
        sb, h = units[u]
        ot_ref[h * MLA_V:(h + 1) * MLA_V, sb * sub:(sb + 1) * sub] = acc[:MLA_V] * (1.0 / acc[MLA_V:MLA_V + 1])

    _attn_units(once_ref, [h for _, h in units], MLA_VT_ROWS, q_of, k_refs, vt_refs,
                (s0_ref, s1_ref), (mx0_ref, mx1_ref), None, acc_ref, finish, unroll)
    o_ref[0] = ot_ref[...].T.astype(BF16)


def _diff_attn_kernel(*refs, nseg, sub, unroll, lam_init):
    once_ref, refs = (refs[0], refs[1:]) if unroll is None else (None, refs)
    dl_ref, sub_ref, q_ref = refs[0], refs[1], refs[2]
    k_refs, vt_refs = refs[3:3 + nseg], refs[3 + nseg:3 + 2 * nseg]
    o_ref, s0_ref, s1_ref, mx0_ref, mx1_ref, l_ref, acc_ref, qu_ref = refs[3 + 2 * nseg:]
    units = [(sb, h) for sb in range(q_ref.shape[1] // sub) for h in range(DIFF_HEADS)]
    dl = dl_ref[...]
    lam = (jnp.exp(jnp.sum(dl[0:1] * dl[1:2], axis=-1, keepdims=True))
           - jnp.exp(jnp.sum(dl[2:3] * dl[3:4], axis=-1, keepdims=True)) + lam_init)
    lane = lax.broadcasted_iota(jnp.int32, (sub, LANES), 1)
    zero = jnp.zeros((), BF16)
    for u, (sb, h) in enumerate(units):
        qh = q_ref[0, sb * sub:(sb + 1) * sub, h * HEAD_PAD:(h + 1) * HEAD_PAD]
        qu_ref[u, :sub, :] = jnp.where(lane < DIFF_DIM, qh, zero)
        qu_ref[u, sub:, :] = jnp.where(lane >= DIFF_DIM, qh, zero)

    def finish(u, acc, l):
        sb, h = units[u]
        o = acc[:, :sub] * (1.0 / l[:, :sub]) - acc[:, sub:] * (lam / l[:, sub:])
        on = o * lax.rsqrt(jnp.mean(o * o, axis=0, keepdims=True) + EPS)
        o_ref[0, sb * sub:(sb + 1) * sub, h * DIFF_V:(h + 1) * DIFF_V] = (
            on.T * sub_ref[...] * (1.0 - lam_init)).astype(BF16)

    _attn_units(once_ref, [h for _, h in units], DIFF_V, lambda u: qu_ref[u], k_refs, vt_refs,
                (s0_ref, s1_ref), (mx0_ref, mx1_ref), l_ref, acc_ref, finish, unroll)


def _attention(kernel, name, q, ks, vts, extra, *, tq, sub, unroll, w_out, maps_per_head):
    bsz, nq, wq = q.shape
    nseg = len(ks)
    n_heads = wq // HEAD_PAD
    whole = lambda a: pl.BlockSpec((1,) + a.shape[1:], lambda b, i: (b,) + (0,) * (a.ndim - 1))
    vrows = vts[0].shape[2] // n_heads
    n_keys = sum(k.shape[1] for k in ks)
    n = sub * maps_per_head
    scratch = [pltpu.VMEM((n_keys, n), F32)] * 2 + [pltpu.VMEM((8, n), F32)] * 2
    if maps_per_head > 1:
        scratch += [pltpu.VMEM((8, n), F32), pltpu.VMEM((vrows, n), F32),
                    pltpu.VMEM((n_heads * tq // sub, n, HEAD_PAD), BF16)]
    else:
        scratch += [pltpu.VMEM((vrows, n), F32), pltpu.VMEM((w_out, tq), F32)]
    once = [] if unroll is not None else [jnp.ones((1,), jnp.int32)]
    once_spec = [] if unroll is not None else [pl.BlockSpec(memory_space=pltpu.SMEM)]
    return pl.pallas_call(
        functools.partial(kernel, nseg=nseg, sub=sub, unroll=unroll),
        grid=(bsz, nq // tq),
        in_specs=once_spec + [_const_spec(e.shape) for e in extra]
        + [pl.BlockSpec((1, tq, wq), lambda b, i: (b, i, 0))] + [whole(k) for k in ks] + [whole(v) for v in vts],
        out_specs=pl.BlockSpec((1, tq, w_out), lambda b, i: (b, i, 0)),
        out_shape=jax.ShapeDtypeStruct((bsz, nq, w_out), BF16),
        scratch_shapes=scratch,
        compiler_params=_params(2),
        name=name,
    )(*once, *extra, q, *ks, *vts)


def _axial_angles(n_tok, rot_dim):
    t = jnp.arange(n_tok)
    row = (t // GRID_W).astype(F32)
    col = (t % GRID_W).astype(F32)
    n_axis = rot_dim // 4
    inv = ROPE_BASE ** (-jnp.arange(n_axis, dtype=F32) / n_axis)
    return jnp.concatenate([row[:, None] * inv, col[:, None] * inv], axis=-1)


def _rope_tables(n_tok, n_ctx):
    am, ad = _axial_angles(n_tok, MLA_ROPE), _axial_angles(n_tok, DIFF_DIM)
    one = lambda w: jnp.ones((n_tok, w), F32)
    zero = lambda w: jnp.zeros((n_tok, w), F32)
    cm = jnp.concatenate([one(MLA_NOPE), jnp.cos(am), jnp.cos(am), one(HEAD_PAD - MLA_NOPE - MLA_ROPE)], axis=-1)
    sam = jnp.concatenate([zero(MLA_NOPE), -jnp.sin(am), zero(HEAD_PAD - MLA_NOPE - MLA_ROPE // 2)], axis=-1)
    sbm = jnp.concatenate([zero(MLA_NOPE + MLA_ROPE // 2), jnp.sin(am), zero(HEAD_PAD - MLA_NOPE - MLA_ROPE)], axis=-1)
    cd = jnp.tile(jnp.cos(ad), (1, 4))
    sad = jnp.tile(jnp.concatenate([-jnp.sin(ad), zero(DIFF_DIM // 2)], axis=-1), (1, 2))
    sbd = jnp.tile(jnp.concatenate([zero(DIFF_DIM // 2), jnp.sin(ad)], axis=-1), (1, 2))
    lat = (cm, sam, sbm, cd, sad, sbd)
    ident = (jnp.ones((n_ctx, LANES), F32),) + (jnp.zeros((n_ctx, LANES), F32),) * 2
    return lat, ident + ident


def _inv_counts(n):
    idx = jnp.arange(n)
    cols = []
    for k in POOL_WINDOWS:
        lo, hi = k // 2, k - 1 - k // 2
        cnt = (jnp.clip(idx + hi + 1, 0, n) - jnp.clip(idx - lo, 0, n)).astype(F32)
        cols.append(jnp.broadcast_to((1.0 / cnt)[:, None], (n, POOL_GROUP)))
    return jnp.concatenate(cols, axis=-1)


def _layer_weights(w_in, w_out, pool_w, mla_w_uq, mla_w_ukv):
    d = w_in.shape[0]
    s1, s2 = POOL_WIDTH, POOL_WIDTH + MLA_Q_RANK + MLA_KV_RANK + MLA_ROPE
    kr_pad = jnp.zeros((d, HEAD_PAD), F32).at[:, MLA_NOPE:MLA_NOPE + MLA_ROPE].set(w_in[:, s2 - MLA_ROPE:s2])
    s3 = s2 + 2 * DIFF_HEADS * 2 * DIFF_DIM
    wall = jnp.concatenate([w_in[:, :s2 - MLA_ROPE], kr_pad, w_in[:, s2:s3]], axis=-1).astype(BF16)
    wdvt = w_in[:, s3:].T.astype(BF16)
    qd = MLA_NOPE + MLA_ROPE
    wuq = jnp.pad(mla_w_uq.reshape(MLA_Q_RANK, MLA_HEADS, qd), ((0, 0), (0, 0), (0, HEAD_PAD - qd)))
    wuq = wuq.reshape(MLA_Q_RANK, MLA_HEADS * HEAD_PAD).astype(BF16)
    ukv = mla_w_ukv.reshape(MLA_KV_RANK, MLA_HEADS, MLA_NOPE + MLA_V)
    wk = jnp.pad(ukv[:, :, :MLA_NOPE], ((0, 0), (0, 0), (0, HEAD_PAD - MLA_NOPE)))
    wk = wk.reshape(MLA_KV_RANK, MLA_HEADS * HEAD_PAD).astype(BF16)
    wvt = ukv[:, :, MLA_NOPE:].reshape(MLA_KV_RANK, MLA_HEADS * MLA_V).T.astype(BF16)
    bd = jax.scipy.linalg.block_diag(*[pool_w[g] for g in range(len(POOL_WINDOWS))]).astype(BF16)
    return wall, wdvt, wuq, wk, wvt, bd, w_out.astype(BF16)


def kernel(x, c, ctx, c_ctx, w_mod, b_mod, ffn1_norm, ffn1_w1, ffn1_w3, ffn1_w2, mix_norm, w_in, w_out, pool_w, pool_scale, mla_q_norm, mla_w_uq, mla_kv_norm, mla_w_ukv, diff_lambda, diff_subln, ffn2_norm, ffn2_w1, ffn2_w3, ffn2_w2, final_norm):
    bsz, seq, d = x.shape
    n_ctx = ctx.shape[1]
    depth = w_mod.shape[0]
    assert bsz + 1 <= MOD_ROWS and seq % GRID_W == 0
    tm = 512
    ctx_row = bsz

    cc = jnp.zeros((MOD_ROWS, d), F32).at[:bsz].set(c).at[ctx_row].set(c_ctx)
    mods_all = _modulation(cc, w_mod, b_mod).reshape(depth, MOD_ROWS, N_MOD, d)

    tab_lat, tab_ctx = _rope_tables(seq, n_ctx)
    icg, icx = _inv_counts(GRID_W), _inv_counts(n_ctx)
    bf = lambda w: w.astype(BF16)
    mla_args = dict(w_out=MLA_HEADS * MLA_V, maps_per_head=1)
    dif_args = dict(w_out=DIFF_HEADS * DIFF_V, maps_per_head=2)

    h, hc = x, ctx
    for i in range(depth):
        last = i == depth - 1
        mods = mods_all[i]
        lam_init = 0.8 - 0.6 * math.exp(-0.3 * i)
        wall, wdvt, wuq, wk, wvt, bd, wo = _layer_weights(w_in[i], w_out[i], pool_w[i], mla_w_uq[i], mla_w_ukv[i])
        f1 = (ffn1_norm[i], bf(ffn1_w1[i]), bf(ffn1_w3[i]), bf(ffn1_w2[i]))
        f2 = (ffn2_norm[i], bf(ffn2_w1[i]), bf(ffn2_w3[i]), bf(ffn2_w2[i]))

        h = _ffn(h, mods, None, *f1, k0=0, tm=tm)
        hc = _ffn(hc, mods, ctx_row, *f1, k0=0, tm=n_ctx)

        pw = (mix_norm[i], wall, wdvt, mla_q_norm[i], mla_kv_norm[i], wuq, wk, wvt)
        u, qm, km, vm, qd, kd, vd = _proj(h, mods, None, *pw, tab_lat, tm=tm)
        uc, qmc, kmc, vmc, qdc, kdc, vdc = _proj(hc, mods, ctx_row, *pw, tab_ctx, tm=n_ctx)

        a, ac = _pool(u, uc, icg, icx, bd, pool_scale[i])
        dif = functools.partial(_diff_attn_kernel, lam_init=lam_init)
        dextra = (diff_lambda[i], diff_subln[i].reshape(1, DIFF_V))
        b = _attention(_mla_attn_kernel, "mla_attn", qm, (km, kmc), (vm, vmc), (),
                       tq=2048, sub=512, unroll=None, **mla_args)
        cdiff = _attention(dif, "diff_attn", qd, (kd, kdc), (vd, vdc), dextra,
                           tq=1024, sub=256, unroll=None, **dif_args)
        h = _ffn(h, mods, None, *f2, k0=6, tm=tm, mix=(a, b, cdiff, wo),
                 final_g=final_norm if last else None)
        if not last:
            bc = _attention(_mla_attn_kernel, "mla_attn_ctx", qmc, (kmc,), (vmc,), (),
                            tq=n_ctx, sub=n_ctx, unroll=1, **mla_args)
            cc_ = _attention(dif, "diff_attn_ctx", qdc, (kdc,), (vdc,), dextra,
                             tq=n_ctx, sub=n_ctx, unroll=1, **dif_args)
            hc = _ffn(hc, mods, ctx_row, *f2, k0=6, tm=n_ctx, mix=(ac, bc, cc_, wo))
    return h
```

```python
import functools
import math

import jax
import jax.numpy as jnp
from jax import lax
from jax.experimental import pallas as pl
from jax.experimental.pallas import tpu as pltpu

F32 = jnp.float32
BF16 = jnp.bfloat16

GRID_W = 64
N_MOD = 9
POOL_WINDOWS = (2, 4, 8, 16)
POOL_GROUP = 64
POOL_WIDTH = POOL_GROUP * len(POOL_WINDOWS)
POOL_HALF = max(POOL_WINDOWS) // 2
MLA_HEADS = 4
MLA_NOPE = 64
MLA_ROPE = 32
MLA_V = 64
MLA_Q_RANK = 384
MLA_KV_RANK = 256
DIFF_HEADS = 4
DIFF_DIM = 64
DIFF_V = 2 * DIFF_DIM
ROPE_BASE = 10000.0
EPS = 1e-6

LANES = 128
HEAD_PAD = LANES
SUM_ROWS = 16
MLA_VT_ROWS = MLA_V + SUM_ROWS
MOD_ROWS = 24
VMEM_LIMIT = 56 * 1024 * 1024

C_POOL = 0
C_CQ = C_POOL + POOL_WIDTH
C_CKV = C_CQ + MLA_Q_RANK
C_KR = C_CKV + MLA_KV_RANK
C_DQ = C_KR + HEAD_PAD
C_DK = C_DQ + DIFF_HEADS * 2 * DIFF_DIM
C_DV = C_DK + DIFF_HEADS * 2 * DIFF_DIM

LOG2E = math.log2(math.e)
MLA_QSCALE = (MLA_NOPE + MLA_ROPE) ** -0.5 * LOG2E
DIFF_QSCALE = DIFF_DIM ** -0.5 * LOG2E


def _dot(a, b):
    return jnp.dot(a, b, preferred_element_type=F32)


def _dot_nt(a, b):
    return lax.dot_general(a, b, (((1,), (1,)), ((), ())), preferred_element_type=F32)


def _rms(x, g):
    return x * lax.rsqrt(jnp.mean(x * x, axis=-1, keepdims=True) + EPS) * g


def _silu(a):
    return a * (1.0 / (1.0 + jnp.exp(-a)))


def _const_spec(shape):
    return pl.BlockSpec(shape, lambda *_: (0,) * len(shape), pipeline_mode=pl.Buffered(1))


def _params(n_axes):
    return pltpu.CompilerParams(dimension_semantics=("parallel",) * n_axes, vmem_limit_bytes=VMEM_LIMIT)


def _mod_kernel(cc_ref, w_ref, b_ref, o_ref):
    sc = _silu(cc_ref[...])
    o_ref[0] = jnp.dot(sc, w_ref[0], preferred_element_type=F32, precision=lax.Precision.HIGHEST) + b_ref[0]


def _modulation(cc, w_mod, b_mod):
    depth, d, width = w_mod.shape
    bn = 9 * LANES
    return pl.pallas_call(
        _mod_kernel,
        grid=(depth, width // bn),
        in_specs=[
            pl.BlockSpec((MOD_ROWS, d), lambda l, j: (0, 0)),
            pl.BlockSpec((1, d, bn), lambda l, j: (l, 0, j)),
            pl.BlockSpec((1, 1, bn), lambda l, j: (l, 0, j)),
        ],
        out_specs=pl.BlockSpec((1, MOD_ROWS, bn), lambda l, j: (l, 0, j)),
        out_shape=jax.ShapeDtypeStruct((depth, MOD_ROWS, width), F32),
        compiler_params=_params(2),
        name="modulation",
    )(cc, w_mod, b_mod.reshape(depth, 1, width))


MXU_COLS = 256
FFN_CHUNK = MXU_COLS


def _ffn_kernel(*refs, pre, final, k0):
    it = iter(refs)
    x_ref, m_ref = next(it), next(it)
    if pre:
        a_ref, b_ref, c_ref, wo_ref = next(it), next(it), next(it), next(it)
    g_ref, w1_ref, w3_ref, w2_ref = next(it), next(it), next(it), next(it)
    if final:
        fg_ref = next(it)
    o_ref, hmid_ref = next(it), next(it)

    x = x_ref[0]
    if pre:
        mix = jnp.concatenate([a_ref[0], b_ref[0], c_ref[0]], axis=-1)
        x = x + m_ref[0, 5:6, :] * _dot(mix, wo_ref[...])
    xm = _rms(x, g_ref[...]) * (1.0 + m_ref[0, k0 + 1:k0 + 2, :]) + m_ref[0, k0:k0 + 1, :]
    xb = xm.astype(BF16)
    dff = w1_ref.shape[1]
    for lo in range(0, dff, FFN_CHUNK):
        cols = slice(lo, min(lo + FFN_CHUNK, dff))
        a = _dot(xb, w1_ref[:, cols])
        b = _dot(xb, w3_ref[:, cols])
        hmid_ref[:, cols] = (_silu(a) * b).astype(BF16)
    y = _dot(hmid_ref[...], w2_ref[...])
    out = x + (0.5 * m_ref[0, k0 + 2:k0 + 3, :]) * y
    if final:
        out = _rms(out, fg_ref[...])
    o_ref[0] = out


def _ffn(x, mods, mod_row, g, w1, w3, w2, *, k0, tm, mix=None, final_g=None):
    bsz, n, d = x.shape
    dff = w1.shape[1]
    pre, final = mix is not None, final_g is not None
    tok = lambda w: pl.BlockSpec((1, tm, w), lambda b, i: (b, i, 0))
    mod_spec = pl.BlockSpec((1, N_MOD, d), (lambda b, i: (b, 0, 0)) if mod_row is None else (lambda b, i: (mod_row, 0, 0)))
    args, specs = [x, mods], [tok(d), mod_spec]
    if pre:
        a, bb, c, wo = mix
        args += [a, bb, c, wo]
        specs += [tok(a.shape[-1]), tok(bb.shape[-1]), tok(c.shape[-1]), _const_spec(wo.shape)]
    args += [g.reshape(1, d), w1, w3, w2]
    specs += [_const_spec((1, d)), _const_spec(w1.shape), _const_spec(w3.shape), _const_spec(w2.shape)]
    if final:
        args.append(final_g.reshape(1, d))
        specs.append(_const_spec((1, d)))
    return pl.pallas_call(
        functools.partial(_ffn_kernel, pre=pre, final=final, k0=k0),
        grid=(bsz, n // tm),
        in_specs=specs,
        out_specs=tok(d),
        out_shape=jax.ShapeDtypeStruct(x.shape, F32),
        scratch_shapes=[pltpu.VMEM((tm, dff), BF16)],
        compiler_params=_params(2),
        name="ffn_mix" if pre else "ffn",
    )(*args)


def _rope(x, c, sa, sb, shift):
    return x * c + pltpu.roll(x, LANES - shift, 1) * sa + pltpu.roll(x, shift, 1) * sb


def _proj_kernel(x_ref, m_ref, g_ref, wall_ref, wdvt_ref, qn_ref, kvn_ref, wuq_ref, wk_ref, wvt_ref,
                 cm_ref, sam_ref, sbm_ref, cd_ref, sad_ref, sbd_ref,
                 u_ref, qm_ref, km_ref, vmt_ref, qd_ref, kd_ref, vdt_ref):
    x = x_ref[0]
    n = _rms(x, g_ref[...]) * (1.0 + m_ref[0, 4:5, :]) + m_ref[0, 3:4, :]
    nb = n.astype(BF16)
    proj = lambda lo, hi: _dot(nb, wall_ref[:, lo:hi])

    pm = proj(C_CQ, C_DQ)
    cqn = _rms(pm[:, :C_CKV - C_CQ], qn_ref[...]).astype(BF16)
    ckvn = _rms(pm[:, C_CKV - C_CQ:C_KR - C_CQ], kvn_ref[...]).astype(BF16)
    q = _dot(cqn, wuq_ref[...])
    k = _dot(ckvn, wk_ref[...])
    vt = _dot_nt(wvt_ref[...], ckvn)
    ones = jnp.ones((SUM_ROWS, vt.shape[1]), BF16)
    for h in range(MLA_HEADS):
        vmt_ref[0, 0, h * MLA_VT_ROWS:h * MLA_VT_ROWS + MLA_V, :] = vt[h * MLA_V:(h + 1) * MLA_V].astype(BF16)
        vmt_ref[0, 0, h * MLA_VT_ROWS + MLA_V:(h + 1) * MLA_VT_ROWS, :] = ones
    cm, sam, sbm = cm_ref[...], sam_ref[...], sbm_ref[...]
    half_m = MLA_ROPE // 2
    kr = _rope(pm[:, C_KR - C_CQ:], cm, sam, sbm, half_m)
    for h in range(MLA_HEADS):
        sl = slice(h * HEAD_PAD, (h + 1) * HEAD_PAD)
        qm_ref[0, :, sl] = (_rope(q[:, sl], cm, sam, sbm, half_m) * MLA_QSCALE).astype(BF16)
        km_ref[0, :, sl] = (k[:, sl] + kr).astype(BF16)

    cd, sad, sbd = cd_ref[...], sad_ref[...], sbd_ref[...]
    half_d = DIFF_DIM // 2
    for col0, out_ref, scale in ((C_DQ, qd_ref, DIFF_QSCALE), (C_DK, kd_ref, None)):
        for h0 in range(0, DIFF_HEADS, 2):
            pp = proj(col0 + h0 * HEAD_PAD, col0 + (h0 + 2) * HEAD_PAD)
            for h in (h0, h0 + 1):
                r = _rope(pp[:, (h - h0) * HEAD_PAD:(h - h0 + 1) * HEAD_PAD], cd, sad, sbd, half_d)
                out_ref[0, :, h * HEAD_PAD:(h + 1) * HEAD_PAD] = (r if scale is None else r * scale).astype(BF16)
    u_ref[0] = proj(C_POOL, C_CQ)
    vdt_ref[0, 0] = _dot_nt(wdvt_ref[...], nb).astype(BF16)


def _proj(x, mods, mod_row, g, wall, wdvt, qn, kvn, wuq, wk, wvt, tables, *, tm):
    bsz, n, d = x.shape
    tok = lambda w: pl.BlockSpec((1, tm, w), lambda b, i: (b, i, 0))
    vt = lambda r: pl.BlockSpec((1, 1, r, tm), lambda b, i: (b, i, 0, 0))
    mod_spec = pl.BlockSpec((1, N_MOD, d), (lambda b, i: (b, 0, 0)) if mod_row is None else (lambda b, i: (mod_row, 0, 0)))
    tab = pl.BlockSpec((tm, LANES), lambda b, i: (i, 0))
    wm, wd = MLA_HEADS * HEAD_PAD, DIFF_HEADS * HEAD_PAD
    rm, rd = MLA_HEADS * MLA_VT_ROWS, DIFF_HEADS * DIFF_V
    tok_sds = lambda w, dt: jax.ShapeDtypeStruct((bsz, n, w), dt)
    vt_sds = lambda r: jax.ShapeDtypeStruct((bsz, n // tm, r, tm), BF16)
    return pl.pallas_call(
        _proj_kernel,
        grid=(bsz, n // tm),
        in_specs=[tok(d), mod_spec, _const_spec((1, d)), _const_spec(wall.shape), _const_spec(wdvt.shape),
                  _const_spec((1, MLA_Q_RANK)), _const_spec((1, MLA_KV_RANK)),
                  _const_spec(wuq.shape), _const_spec(wk.shape), _const_spec(wvt.shape)] + [tab] * 6,
        out_specs=[tok(POOL_WIDTH), tok(wm), tok(wm), vt(rm), tok(wd), tok(wd), vt(rd)],
        out_shape=[tok_sds(POOL_WIDTH, F32), tok_sds(wm, BF16), tok_sds(wm, BF16), vt_sds(rm),
                   tok_sds(wd, BF16), tok_sds(wd, BF16), vt_sds(rd)],
        compiler_params=_params(2),
        name="mix_in",
    )(x, mods, g.reshape(1, d), wall, wdvt, qn.reshape(1, -1), kvn.reshape(1, -1), wuq, wk, wvt, *tables)


def _nested_window_sums(load, lane):
    a2 = load(-1) + load(0)
    a4 = a2 + load(-2) + load(1)
    a8 = a4 + load(-4) + load(-3) + load(2) + load(3)
    a16 = a8 + load(-8) + load(-7) + load(-6) + load(-5) + load(4) + load(5) + load(6) + load(7)
    return jnp.where(lane < POOL_GROUP, a2, jnp.where(lane < 2 * POOL_GROUP, a4, jnp.where(lane < 3 * POOL_GROUP, a8, a16)))


def _window_sums_1d(slab, n, lane):
    size = slab.shape[0]
    ahead = lambda x, k: pltpu.roll(x, size - k, 0)
    behind = lambda x, k: pltpu.roll(x, k, 0)[POOL_HALF:POOL_HALF + n]
    p2 = slab + ahead(slab, 1)
    p4 = p2 + ahead(p2, 2)
    p8 = p4 + ahead(p4, 4)
    p16 = p8 + ahead(p8, 8)
    return jnp.where(lane < POOL_GROUP, behind(p2, 1),
                     jnp.where(lane < 2 * POOL_GROUP, behind(p4, 2),
                               jnp.where(lane < 3 * POOL_GROUP, behind(p8, 4), p16[:n])))


def _pool_kernel(ul_ref, uc_ref, icg_ref, icx_ref, bd_ref, ps_ref, al_ref, ac_ref,
                 z_ref, y_ref, dl_ref, zc_ref, *, rows):
    gw, pad = GRID_W, POOL_HALF
    stride = gw + 2 * pad
    n_ctx = uc_ref.shape[1]
    lane = lax.broadcasted_iota(jnp.int32, (gw, POOL_WIDTH), 1)
    zero_rows = jnp.zeros((pad * gw, POOL_WIDTH), F32)
    zero_pad = jnp.zeros((pad, POOL_WIDTH), F32)

    z_ref[0:pad * gw, :] = zero_rows
    z_ref[pad * gw:(rows + pad) * gw, :] = ul_ref[0]
    z_ref[(rows + pad) * gw:(rows + 2 * pad) * gw, :] = zero_rows

    def row_pass(r, carry):
        base = pl.multiple_of((r + pad) * gw, 8)
        s = _nested_window_sums(lambda d: z_ref[pl.ds(base + d * gw, gw), :], lane)
        yb = pl.multiple_of(r * stride, 8)
        y_ref[pl.ds(yb, pad), :] = zero_pad
        y_ref[pl.ds(yb + pad, gw), :] = s * icg_ref[pl.ds(r, 1), :]
        y_ref[pl.ds(yb + pad + gw, pad), :] = zero_pad
        return carry

    lax.fori_loop(0, rows, row_pass, 0)

    def col_pass(r, carry):
        slab = y_ref[pl.ds(pl.multiple_of(r * stride, 8), stride), :]
        s = _window_sums_1d(slab, gw, lane)
        tok = pl.ds(pl.multiple_of(r * gw, 8), gw)
        dl_ref[tok, :] = (s * icg_ref[...] - ul_ref[0, tok, :]).astype(BF16)
        return carry

    lax.fori_loop(0, rows, col_pass, 0)
    al_ref[0] = (_dot(dl_ref[...], bd_ref[...]) * ps_ref[...]).astype(BF16)

    uc = uc_ref[0]
    zc_ref[0:pad, :] = zero_pad
    zc_ref[pad:pad + n_ctx, :] = uc
    zc_ref[pad + n_ctx:pad + n_ctx + pad, :] = zero_pad
    lane_c = lax.broadcasted_iota(jnp.int32, (n_ctx, POOL_WIDTH), 1)
    sc = _window_sums_1d(zc_ref[...], n_ctx, lane_c)
    dc = (sc * icx_ref[...] - uc).astype(BF16)
    ac_ref[0] = (_dot(dc, bd_ref[...]) * ps_ref[...]).astype(BF16)


def _pool(u_lat, u_ctx, icg, icx, bd, ps):
    bsz, s, w = u_lat.shape
    n_ctx = u_ctx.shape[1]
    rows = s // GRID_W
    stride = GRID_W + 2 * POOL_HALF
    return pl.pallas_call(
        functools.partial(_pool_kernel, rows=rows),
        grid=(bsz,),
        in_specs=[pl.BlockSpec((1, s, w), lambda b: (b, 0, 0)), pl.BlockSpec((1, n_ctx, w), lambda b: (b, 0, 0)),
                  _const_spec(icg.shape), _const_spec(icx.shape), _const_spec(bd.shape), _const_spec((1, w))],
        out_specs=[pl.BlockSpec((1, s, w), lambda b: (b, 0, 0)), pl.BlockSpec((1, n_ctx, w), lambda b: (b, 0, 0))],
        out_shape=[jax.ShapeDtypeStruct((bsz, s, w), BF16), jax.ShapeDtypeStruct((bsz, n_ctx, w), BF16)],
        scratch_shapes=[pltpu.VMEM(((rows + 2 * POOL_HALF) * GRID_W, w), F32), pltpu.VMEM((rows * stride, w), F32),
                        pltpu.VMEM((s, w), BF16), pltpu.VMEM((n_ctx + 2 * POOL_HALF, w), F32)],
        compiler_params=_params(1),
        name="pool",
    )(u_lat, u_ctx, icg, icx, bd, ps.reshape(1, w))


def _attn_units(once_ref, heads, vrows, qt_of, k_refs, vt_refs, s_refs, mx_refs, l_ref, acc_ref, finish, unroll):
    n_units = len(heads)
    n = s_refs[0].shape[1]
    offs, off = [], 0
    for k_ref in k_refs:
        offs.append(off)
        off += k_ref.shape[1]

    def for_chunks(fn):
        if once_ref is not None:
            def whole(t, carry):
                for k_ref, vt_ref, o in zip(k_refs, vt_refs, offs):
                    n_chunks, kc = vt_ref.shape[1], vt_ref.shape[3]
                    for c in range(n_chunks):
                        fn(k_ref, vt_ref, c, slice(c * kc, (c + 1) * kc), slice(o + c * kc, o + (c + 1) * kc))
                return carry

            lax.fori_loop(0, once_ref[0], whole, 0)
            return
        for k_ref, vt_ref, o in zip(k_refs, vt_refs, offs):
            n_chunks, kc = vt_ref.shape[1], vt_ref.shape[3]
            if n_chunks == 1:
                fn(k_ref, vt_ref, 0, slice(0, kc), slice(o, o + kc))
            else:
                def body(c, carry):
                    r = pl.multiple_of(c * kc, kc)
                    fn(k_ref, vt_ref, c, pl.ds(r, kc), pl.ds(o + r, kc))
                    return carry

                lax.fori_loop(0, n_chunks, body, 0, unroll=unroll)

    def scores(u, slot):
        ksl = slice(heads[u] * HEAD_PAD, (heads[u] + 1) * HEAD_PAD)

        def fn(k_ref, vt_ref, c, krows, srows):
            st = _dot(k_ref[0, krows, ksl], qt_of(u))
            s_refs[slot][srows, :] = st
            mx_refs[slot][...] = jnp.maximum(mx_refs[slot][...], jnp.max(st.reshape(-1, 8, n), axis=0))

        return fn

    def probs(u, slot, m):
        vsl = slice(heads[u] * vrows, (heads[u] + 1) * vrows)

        def fn(k_ref, vt_ref, c, krows, srows):
            p = jnp.exp2(s_refs[slot][srows, :] - m)
            if l_ref is not None:
                l_ref[...] += jnp.sum(p.reshape(-1, 8, n), axis=0)
            acc_ref[...] += _dot(vt_ref[0, c, vsl, :], p.astype(BF16))

        return fn

    def both(f, g):
        def fn(*a):
            f(*a)
            g(*a)

        return fn

    neg_inf = jnp.full(mx_refs[0].shape, -jnp.inf, F32)
    mx_refs[0][...] = neg_inf
    for_chunks(scores(0, 0))
    for u in range(n_units):
        slot = u % 2
        m = jnp.max(mx_refs[slot][...], axis=0, keepdims=True)
        if l_ref is not None:
            l_ref[...] = jnp.zeros(l_ref.shape, F32)
        acc_ref[...] = jnp.zeros(acc_ref.shape, F32)
        if u + 1 < n_units:
            mx_refs[1 - slot][...] = neg_inf
            for_chunks(both(scores(u + 1, 1 - slot), probs(u, slot, m)))
        else:
            for_chunks(probs(u, slot, m))
        finish(u, acc_ref[...], None if l_ref is None else jnp.sum(l_ref[...], axis=0, keepdims=True))


def _mla_attn_kernel(*refs, nseg, sub, unroll):
    once_ref, refs = (refs[0], refs[1:]) if unroll is None else (None, refs)
    q_ref, k_refs, vt_refs = refs[0], refs[1:1 + nseg], refs[1 + nseg:1 + 2 * nseg]
    o_ref, s0_ref, s1_ref, mx0_ref, mx1_ref, acc_ref, ot_ref, qt_ref = refs[1 + 2 * nseg:]
    units = [(sb, h) for sb in range(q_ref.shape[1] // sub) for h in range(MLA_HEADS)]
    for u, (sb, h) in enumerate(units):
        qh = q_ref[0, sb * sub:(sb + 1) * sub, h * HEAD_PAD:(h + 1) * HEAD_PAD]
        qt_ref[u] = qh.astype(F32).T.astype(BF16)

    def finish(u, acc, l):
        sb, h = units[u]
        ot_ref[h * MLA_V:(h + 1) * MLA_V, sb * sub:(sb + 1) * sub] = acc[:MLA_V] * (1.0 / acc[MLA_V:MLA_V + 1])

    _attn_units(once_ref, [h for _, h in units], MLA_VT_ROWS, lambda u: qt_ref[u], k_refs, vt_refs,
                (s0_ref, s1_ref), (mx0_ref, mx1_ref), None, acc_ref, finish, unroll)
    o_ref[0] = ot_ref[...].T.astype(BF16)


def _diff_attn_kernel(*refs, nseg, sub, unroll, lam_init):
    once_ref, refs = (refs[0], refs[1:]) if unroll is None else (None, refs)
    dl_ref, sub_ref, q_ref = refs[0], refs[1], refs[2]
    k_refs, vt_refs = refs[3:3 + nseg], refs[3 + nseg:3 + 2 * nseg]
    o_ref, s0_ref, s1_ref, mx0_ref, mx1_ref, l_ref, acc_ref, qu_ref = refs[3 + 2 * nseg:]
    units = [(sb, h) for sb in range(q_ref.shape[1] // sub) for h in range(DIFF_HEADS)]
    dl = dl_ref[...]
    lam = (jnp.exp(jnp.sum(dl[0:1] * dl[1:2], axis=-1, keepdims=True))
           - jnp.exp(jnp.sum(dl[2:3] * dl[3:4], axis=-1, keepdims=True)) + lam_init)
    dim = lax.broadcasted_iota(jnp.int32, (HEAD_PAD, sub), 0)
    for u, (sb, h) in enumerate(units):
        qt = q_ref[0, sb * sub:(sb + 1) * sub, h * HEAD_PAD:(h + 1) * HEAD_PAD].astype(F32).T
        qu_ref[u, :, :sub] = jnp.where(dim < DIFF_DIM, qt, 0.0).astype(BF16)
        qu_ref[u, :, sub:] = jnp.where(dim >= DIFF_DIM, qt, 0.0).astype(BF16)

    def finish(u, acc, l):
        sb, h = units[u]
        o = acc[:, :sub] * (1.0 / l[:, :sub]) - acc[:, sub:] * (lam / l[:, sub:])
        on = o * lax.rsqrt(jnp.mean(o * o, axis=0, keepdims=True) + EPS)
        o_ref[0, sb * sub:(sb + 1) * sub, h * DIFF_V:(h + 1) * DIFF_V] = (
            on.T * sub_ref[...] * (1.0 - lam_init)).astype(BF16)

    _attn_units(once_ref, [h for _, h in units], DIFF_V, lambda u: qu_ref[u], k_refs, vt_refs,
                (s0_ref, s1_ref), (mx0_ref, mx1_ref), l_ref, acc_ref, finish, unroll)


def _attention(kernel, name, q, ks, vts, extra, *, tq, sub, unroll, w_out, maps_per_head):
    bsz, nq, wq = q.shape
    nseg = len(ks)
    n_heads = wq // HEAD_PAD
    whole = lambda a: pl.BlockSpec((1,) + a.shape[1:], lambda b, i: (b,) + (0,) * (a.ndim - 1))
    vrows = vts[0].shape[2] // n_heads
    n_keys = sum(k.shape[1] for k in ks)
    n = sub * maps_per_head
    scratch = [pltpu.VMEM((n_keys, n), F32)] * 2 + [pltpu.VMEM((8, n), F32)] * 2
    qt_scratch = pltpu.VMEM((n_heads * tq // sub, HEAD_PAD, n), BF16)
    if maps_per_head > 1:
        scratch += [pltpu.VMEM((8, n), F32), pltpu.VMEM((vrows, n), F32), qt_scratch]
    else:
        scratch += [pltpu.VMEM((vrows, n), F32), pltpu.VMEM((w_out, tq), F32), qt_scratch]
    once = [] if unroll is not None else [jnp.ones((1,), jnp.int32)]
    once_spec = [] if unroll is not None else [pl.BlockSpec(memory_space=pltpu.SMEM)]
    return pl.pallas_call(
        functools.partial(kernel, nseg=nseg, sub=sub, unroll=unroll),
        grid=(bsz, nq // tq),
        in_specs=once_spec + [_const_spec(e.shape) for e in extra]
        + [pl.BlockSpec((1, tq, wq), lambda b, i: (b, i, 0))] + [whole(k) for k in ks] + [whole(v) for v in vts],
        out_specs=pl.BlockSpec((1, tq, w_out), lambda b, i: (b, i, 0)),
        out_shape=jax.ShapeDtypeStruct((bsz, nq, w_out), BF16),
        scratch_shapes=scratch,
        compiler_params=_params(2),
        name=name,
    )(*once, *extra, q, *ks, *vts)


def _axial_angles(n_tok, rot_dim):
    t = jnp.arange(n_tok)
    row = (t // GRID_W).astype(F32)
    col = (t % GRID_W).astype(F32)
    n_axis = rot_dim // 4
    inv = ROPE_BASE ** (-jnp.arange(n_axis, dtype=F32) / n_axis)
    return jnp.concatenate([row[:, None] * inv, col[:, None] * inv], axis=-1)


def _rope_tables(n_tok, n_ctx):
    am, ad = _axial_angles(n_tok, MLA_ROPE), _axial_angles(n_tok, DIFF_DIM)
    one = lambda w: jnp.ones((n_tok, w), F32)
    zero = lambda w: jnp.zeros((n_tok, w), F32)
    cm = jnp.concatenate([one(MLA_NOPE), jnp.cos(am), jnp.cos(am), one(HEAD_PAD - MLA_NOPE - MLA_ROPE)], axis=-1)
    sam = jnp.concatenate([zero(MLA_NOPE), -jnp.sin(am), zero(HEAD_PAD - MLA_NOPE - MLA_ROPE // 2)], axis=-1)
    sbm = jnp.concatenate([zero(MLA_NOPE + MLA_ROPE // 2), jnp.sin(am), zero(HEAD_PAD - MLA_NOPE - MLA_ROPE)], axis=-1)
    cd = jnp.tile(jnp.cos(ad), (1, 4))
    sad = jnp.tile(jnp.concatenate([-jnp.sin(ad), zero(DIFF_DIM // 2)], axis=-1), (1, 2))
    sbd = jnp.tile(jnp.concatenate([zero(DIFF_DIM // 2), jnp.sin(ad)], axis=-1), (1, 2))
    lat = (cm, sam, sbm, cd, sad, sbd)
    ident = (jnp.ones((n_ctx, LANES), F32),) + (jnp.zeros((n_ctx, LANES), F32),) * 2
    return lat, ident + ident


def _inv_counts(n):
    idx = jnp.arange(n)
    cols = []
    for k in POOL_WINDOWS:
        lo, hi = k // 2, k - 1 - k // 2
        cnt = (jnp.clip(idx + hi + 1, 0, n) - jnp.clip(idx - lo, 0, n)).astype(F32)
        cols.append(jnp.broadcast_to((1.0 / cnt)[:, None], (n, POOL_GROUP)))
    return jnp.concatenate(cols, axis=-1)


def _layer_weights(w_in, w_out, pool_w, mla_w_uq, mla_w_ukv):
    d = w_in.shape[0]
    s1, s2 = POOL_WIDTH, POOL_WIDTH + MLA_Q_RANK + MLA_KV_RANK + MLA_ROPE
    kr_pad = jnp.zeros((d, HEAD_PAD), F32).at[:, MLA_NOPE:MLA_NOPE + MLA_ROPE].set(w_in[:, s2 - MLA_ROPE:s2])
    s3 = s2 + 2 * DIFF_HEADS * 2 * DIFF_DIM
    wall = jnp.concatenate([w_in[:, :s2 - MLA_ROPE], kr_pad, w_in[:, s2:s3]], axis=-1).astype(BF16)
    wdvt = w_in[:, s3:].T.astype(BF16)
    qd = MLA_NOPE + MLA_ROPE
    wuq = jnp.pad(mla_w_uq.reshape(MLA_Q_RANK, MLA_HEADS, qd), ((0, 0), (0, 0), (0, HEAD_PAD - qd)))
    wuq = wuq.reshape(MLA_Q_RANK, MLA_HEADS * HEAD_PAD).astype(BF16)
    ukv = mla_w_ukv.reshape(MLA_KV_RANK, MLA_HEADS, MLA_NOPE + MLA_V)
    wk = jnp.pad(ukv[:, :, :MLA_NOPE], ((0, 0), (0, 0), (0, HEAD_PAD - MLA_NOPE)))
    wk = wk.reshape(MLA_KV_RANK, MLA_HEADS * HEAD_PAD).astype(BF16)
    wvt = ukv[:, :, MLA_NOPE:].reshape(MLA_KV_RANK, MLA_HEADS * MLA_V).T.astype(BF16)
    bd = jax.scipy.linalg.block_diag(*[pool_w[g] for g in range(len(POOL_WINDOWS))]).astype(BF16)
    return wall, wdvt, wuq, wk, wvt, bd, w_out.astype(BF16)


def kernel(x, c, ctx, c_ctx, w_mod, b_mod, ffn1_norm, ffn1_w1, ffn1_w3, ffn1_w2, mix_norm, w_in, w_out, pool_w, pool_scale, mla_q_norm, mla_w_uq, mla_kv_norm, mla_w_ukv, diff_lambda, diff_subln, ffn2_norm, ffn2_w1, ffn2_w3, ffn2_w2, final_norm):
    bsz, seq, d = x.shape
    n_ctx = ctx.shape[1]
    depth = w_mod.shape[0]
    assert bsz + 1 <= MOD_ROWS and seq % GRID_W == 0
    tm = 512
    ctx_row = bsz

    cc = jnp.zeros((MOD_ROWS, d), F32).at[:bsz].set(c).at[ctx_row].set(c_ctx)
    mods_all = _modulation(cc, w_mod, b_mod).reshape(depth, MOD_ROWS, N_MOD, d)

    tab_lat, tab_ctx = _rope_tables(seq, n_ctx)
    icg, icx = _inv_counts(GRID_W), _inv_counts(n_ctx)
    bf = lambda w: w.astype(BF16)
    mla_args = dict(w_out=MLA_HEADS * MLA_V, maps_per_head=1)
    dif_args = dict(w_out=DIFF_HEADS * DIFF_V, maps_per_head=2)

    h, hc = x, ctx
    for i in range(depth):
        last = i == depth - 1
        mods = mods_all[i]
        lam_init = 0.8 - 0.6 * math.exp(-0.3 * i)
        wall, wdvt, wuq, wk, wvt, bd, wo = _layer_weights(w_in[i], w_out[i], pool_w[i], mla_w_uq[i], mla_w_ukv[i])
        f1 = (ffn1_norm[i], bf(ffn1_w1[i]), bf(ffn1_w3[i]), bf(ffn1_w2[i]))
        f2 = (ffn2_norm[i], bf(ffn2_w1[i]), bf(ffn2_w3[i]), bf(ffn2_w2[i]))

        h = _ffn(h, mods, None, *f1, k0=0, tm=tm)
        hc = _ffn(hc, mods, ctx_row, *f1, k0=0, tm=n_ctx)

        pw = (mix_norm[i], wall, wdvt, mla_q_norm[i], mla_kv_norm[i], wuq, wk, wvt)
        u, qm, km, vm, qd, kd, vd = _proj(h, mods, None, *pw, tab_lat, tm=tm)
        uc, qmc, kmc, vmc, qdc, kdc, vdc = _proj(hc, mods, ctx_row, *pw, tab_ctx, tm=n_ctx)

        a, ac = _pool(u, uc, icg, icx, bd, pool_scale[i])
        dif = functools.partial(_diff_attn_kernel, lam_init=lam_init)
        dextra = (diff_lambda[i], diff_subln[i].reshape(1, DIFF_V))
        b = _attention(_mla_attn_kernel, "mla_attn", qm, (km, kmc), (vm, vmc), (),
                       tq=2048, sub=512, unroll=None, **mla_args)
        cdiff = _attention(dif, "diff_attn", qd, (kd, kdc), (vd, vdc), dextra,
                           tq=1024, sub=256, unroll=None, **dif_args)
        h = _ffn(h, mods, None, *f2, k0=6, tm=tm, mix=(a, b, cdiff, wo),
                 final_g=final_norm if last else None)
        if not last:
            bc = _attention(_mla_attn_kernel, "mla_attn_ctx", qmc, (kmc,), (vmc,), (),
                            tq=n_ctx, sub=n_ctx, unroll=1, **mla_args)
            cc_ = _attention(dif, "diff_attn_ctx", qdc, (kdc,), (vdc,), dextra,
                             tq=n_ctx, sub=n_ctx, unroll=1, **dif_args)
            hc = _ffn(hc, mods, ctx_row, *f2, k0=6, tm=n_ctx, mix=(ac, bc, cc_, wo))
    return h
```

```python
import functools
import math

import jax
import jax.numpy as jnp
from jax import lax
from jax.experimental import pallas as pl
from jax.experimental.pallas import tpu as pltpu

F32 = jnp.float32
BF16 = jnp.bfloat16

GRID_W = 64
N_MOD = 9
POOL_WINDOWS = (2, 4, 8, 16)
POOL_GROUP = 64
POOL_WIDTH = POOL_GROUP * len(POOL_WINDOWS)
POOL_HALF = max(POOL_WINDOWS) // 2
MLA_HEADS = 4
MLA_NOPE = 64
MLA_ROPE = 32
MLA_V = 64
MLA_Q_RANK = 384
MLA_KV_RANK = 256
DIFF_HEADS = 4
DIFF_DIM = 64
DIFF_V = 2 * DIFF_DIM
ROPE_BASE = 10000.0
EPS = 1e-6

LANES = 128
HEAD_PAD = LANES
SUM_ROWS = 16
MLA_VT_ROWS = MLA_V + SUM_ROWS
MOD_ROWS = 24
VMEM_LIMIT = 56 * 1024 * 1024

C_POOL = 0
C_CQ = C_POOL + POOL_WIDTH
C_CKV = C_CQ + MLA_Q_RANK
C_KR = C_CKV + MLA_KV_RANK
C_DQ = C_KR + HEAD_PAD
C_DK = C_DQ + DIFF_HEADS * 2 * DIFF_DIM
C_DV = C_DK + DIFF_HEADS * 2 * DIFF_DIM

LOG2E = math.log2(math.e)
MLA_QSCALE = (MLA_NOPE + MLA_ROPE) ** -0.5 * LOG2E
DIFF_QSCALE = DIFF_DIM ** -0.5 * LOG2E


def _dot(a, b):
    return jnp.dot(a, b, preferred_element_type=F32)


def _dot_nt(a, b):
    return lax.dot_general(a, b, (((1,), (1,)), ((), ())), preferred_element_type=F32)


def _rms(x, g):
    return x * lax.rsqrt(jnp.mean(x * x, axis=-1, keepdims=True) + EPS) * g


def _silu(a):
    return a * (1.0 / (1.0 + jnp.exp(-a)))


def _const_spec(shape):
    return pl.BlockSpec(shape, lambda *_: (0,) * len(shape), pipeline_mode=pl.Buffered(1))


def _params(n_axes):
    return pltpu.CompilerParams(dimension_semantics=("parallel",) * n_axes, vmem_limit_bytes=VMEM_LIMIT)


def _mod_kernel(cc_ref, w_ref, b_ref, o_ref):
    sc = _silu(cc_ref[...])
    o_ref[0] = jnp.dot(sc, w_ref[0], preferred_element_type=F32, precision=lax.Precision.HIGHEST) + b_ref[0]


def _modulation(cc, w_mod, b_mod):
    depth, d, width = w_mod.shape
    bn = 9 * LANES
    return pl.pallas_call(
        _mod_kernel,
        grid=(depth, width // bn),
        in_specs=[
            pl.BlockSpec((MOD_ROWS, d), lambda l, j: (0, 0)),
            pl.BlockSpec((1, d, bn), lambda l, j: (l, 0, j)),
            pl.BlockSpec((1, 1, bn), lambda l, j: (l, 0, j)),
        ],
        out_specs=pl.BlockSpec((1, MOD_ROWS, bn), lambda l, j: (l, 0, j)),
        out_shape=jax.ShapeDtypeStruct((depth, MOD_ROWS, width), F32),
        compiler_params=_params(2),
        name="modulation",
    )(cc, w_mod, b_mod.reshape(depth, 1, width))


MXU_COLS = 256
FFN_CHUNK = MXU_COLS


def _ffn_kernel(*refs, pre, final, k0):
    it = iter(refs)
    x_ref, m_ref = next(it), next(it)
    if pre:
        a_ref, b_ref, c_ref, wo_ref = next(it), next(it), next(it), next(it)
    g_ref, w1_ref, w3_ref, w2_ref = next(it), next(it), next(it), next(it)
    if final:
        fg_ref = next(it)
    o_ref, hmid_ref = next(it), next(it)

    x = x_ref[0]
    if pre:
        mix = jnp.concatenate([a_ref[0], b_ref[0], c_ref[0]], axis=-1)
        x = x + m_ref[0, 5:6, :] * _dot(mix, wo_ref[...])
    xm = _rms(x, g_ref[...]) * (1.0 + m_ref[0, k0 + 1:k0 + 2, :]) + m_ref[0, k0:k0 + 1, :]
    xb = xm.astype(BF16)
    dff = w1_ref.shape[1]
    for lo in range(0, dff, FFN_CHUNK):
        cols = slice(lo, min(lo + FFN_CHUNK, dff))
        a = _dot(xb, w1_ref[:, cols])
        b = _dot(xb, w3_ref[:, cols])
        hmid_ref[:, cols] = (_silu(a) * b).astype(BF16)
    y = _dot(hmid_ref[...], w2_ref[...])
    out = x + (0.5 * m_ref[0, k0 + 2:k0 + 3, :]) * y
    if final:
        out = _rms(out, fg_ref[...])
    o_ref[0] = out


def _ffn(x, mods, mod_row, g, w1, w3, w2, *, k0, tm, mix=None, final_g=None):
    bsz, n, d = x.shape
    dff = w1.shape[1]
    pre, final = mix is not None, final_g is not None
    tok = lambda w: pl.BlockSpec((1, tm, w), lambda b, i: (b, i, 0))
    mod_spec = pl.BlockSpec((1, N_MOD, d), (lambda b, i: (b, 0, 0)) if mod_row is None else (lambda b, i: (mod_row, 0, 0)))
    args, specs = [x, mods], [tok(d), mod_spec]
    if pre:
        a, bb, c, wo = mix
        args += [a, bb, c, wo]
        specs += [tok(a.shape[-1]), tok(bb.shape[-1]), tok(c.shape[-1]), _const_spec(wo.shape)]
    args += [g.reshape(1, d), w1, w3, w2]
    specs += [_const_spec((1, d)), _const_spec(w1.shape), _const_spec(w3.shape), _const_spec(w2.shape)]
    if final:
        args.append(final_g.reshape(1, d))
        specs.append(_const_spec((1, d)))
    return pl.pallas_call(
        functools.partial(_ffn_kernel, pre=pre, final=final, k0=k0),
        grid=(bsz, n // tm),
        in_specs=specs,
        out_specs=tok(d),
        out_shape=jax.ShapeDtypeStruct(x.shape, F32),
        scratch_shapes=[pltpu.VMEM((tm, dff), BF16)],
        compiler_params=_params(2),
        name="ffn_mix" if pre else "ffn",
    )(*args)


def _rope(x, c, sa, sb, shift):
    return x * c + pltpu.roll(x, LANES - shift, 1) * sa + pltpu.roll(x, shift, 1) * sb


def _proj_kernel(x_ref, m_ref, g_ref, wall_ref, wdvt_ref, qn_ref, kvn_ref, wuq_ref, wk_ref, wvt_ref,
                 cm_ref, sam_ref, sbm_ref, cd_ref, sad_ref, sbd_ref,
                 u_ref, qm_ref, km_ref, vmt_ref, qd_ref, kd_ref, vdt_ref):
    x = x_ref[0]
    n = _rms(x, g_ref[...]) * (1.0 + m_ref[0, 4:5, :]) + m_ref[0, 3:4, :]
    nb = n.astype(BF16)
    proj = lambda lo, hi: _dot(nb, wall_ref[:, lo:hi])

    pm = proj(C_CQ, C_DQ)
    cqn = _rms(pm[:, :C_CKV - C_CQ], qn_ref[...]).astype(BF16)
    ckvn = _rms(pm[:, C_CKV - C_CQ:C_KR - C_CQ], kvn_ref[...]).astype(BF16)
    q = _dot(cqn, wuq_ref[...])
    k = _dot(ckvn, wk_ref[...])
    vt = _dot_nt(wvt_ref[...], ckvn)
    ones = jnp.ones((SUM_ROWS, vt.shape[1]), BF16)
    for h in range(MLA_HEADS):
        vmt_ref[0, 0, h * MLA_VT_ROWS:h * MLA_VT_ROWS + MLA_V, :] = vt[h * MLA_V:(h + 1) * MLA_V].astype(BF16)
        vmt_ref[0, 0, h * MLA_VT_ROWS + MLA_V:(h + 1) * MLA_VT_ROWS, :] = ones
    cm, sam, sbm = cm_ref[...], sam_ref[...], sbm_ref[...]
    half_m = MLA_ROPE // 2
    kr = _rope(pm[:, C_KR - C_CQ:], cm, sam, sbm, half_m)
    for h in range(MLA_HEADS):
        sl = slice(h * HEAD_PAD, (h + 1) * HEAD_PAD)
        qm_ref[0, :, sl] = (_rope(q[:, sl], cm, sam, sbm, half_m) * MLA_QSCALE).astype(BF16)
        km_ref[0, :, sl] = (k[:, sl] + kr).astype(BF16)

    cd, sad, sbd = cd_ref[...], sad_ref[...], sbd_ref[...]
    half_d = DIFF_DIM // 2
    for col0, out_ref, scale in ((C_DQ, qd_ref, DIFF_QSCALE), (C_DK, kd_ref, None)):
        for h0 in range(0, DIFF_HEADS, 2):
            pp = proj(col0 + h0 * HEAD_PAD, col0 + (h0 + 2) * HEAD_PAD)
            for h in (h0, h0 + 1):
                r = _rope(pp[:, (h - h0) * HEAD_PAD:(h - h0 + 1) * HEAD_PAD], cd, sad, sbd, half_d)
                out_ref[0, :, h * HEAD_PAD:(h + 1) * HEAD_PAD] = (r if scale is None else r * scale).astype(BF16)
    u_ref[0] = proj(C_POOL, C_CQ)
    vdt_ref[0, 0] = _dot_nt(wdvt_ref[...], nb).astype(BF16)


def _proj(x, mods, mod_row, g, wall, wdvt, qn, kvn, wuq, wk, wvt, tables, *, tm):
    bsz, n, d = x.shape
    tok = lambda w: pl.BlockSpec((1, tm, w), lambda b, i: (b, i, 0))
    vt = lambda r: pl.BlockSpec((1, 1, r, tm), lambda b, i: (b, i, 0, 0))
    mod_spec = pl.BlockSpec((1, N_MOD, d), (lambda b, i: (b, 0, 0)) if mod_row is None else (lambda b, i: (mod_row, 0, 0)))
    tab = pl.BlockSpec((tm, LANES), lambda b, i: (i, 0))
    wm, wd = MLA_HEADS * HEAD_PAD, DIFF_HEADS * HEAD_PAD
    rm, rd = MLA_HEADS * MLA_VT_ROWS, DIFF_HEADS * DIFF_V
    tok_sds = lambda w, dt: jax.ShapeDtypeStruct((bsz, n, w), dt)
    vt_sds = lambda r: jax.ShapeDtypeStruct((bsz, n // tm, r, tm), BF16)
    return pl.pallas_call(
        _proj_kernel,
        grid=(bsz, n // tm),
        in_specs=[tok(d), mod_spec, _const_spec((1, d)), _const_spec(wall.shape), _const_spec(wdvt.shape),
                  _const_spec((1, MLA_Q_RANK)), _const_spec((1, MLA_KV_RANK)),
                  _const_spec(wuq.shape), _const_spec(wk.shape), _const_spec(wvt.shape)] + [tab] * 6,
        out_specs=[tok(POOL_WIDTH), tok(wm), tok(wm), vt(rm), tok(wd), tok(wd), vt(rd)],
        out_shape=[tok_sds(POOL_WIDTH, F32), tok_sds(wm, BF16), tok_sds(wm, BF16), vt_sds(rm),
                   tok_sds(wd, BF16), tok_sds(wd, BF16), vt_sds(rd)],
        compiler_params=_params(2),
        name="mix_in",
    )(x, mods, g.reshape(1, d), wall, wdvt, qn.reshape(1, -1), kvn.reshape(1, -1), wuq, wk, wvt, *tables)


def _nested_window_sums(load, lane):
    a2 = load(-1) + load(0)
    a4 = a2 + load(-2) + load(1)
    a8 = a4 + load(-4) + load(-3) + load(2) + load(3)
    a16 = a8 + load(-8) + load(-7) + load(-6) + load(-5) + load(4) + load(5) + load(6) + load(7)
    return jnp.where(lane < POOL_GROUP, a2, jnp.where(lane < 2 * POOL_GROUP, a4, jnp.where(lane < 3 * POOL_GROUP, a8, a16)))


def _window_sums_1d(slab, n, lane):
    size = slab.shape[0]
    ahead = lambda x, k: pltpu.roll(x, size - k, 0)
    behind = lambda x, k: pltpu.roll(x, k, 0)[POOL_HALF:POOL_HALF + n]
    p2 = slab + ahead(slab, 1)
    p4 = p2 + ahead(p2, 2)
    p8 = p4 + ahead(p4, 4)
    p16 = p8 + ahead(p8, 8)
    return jnp.where(lane < POOL_GROUP, behind(p2, 1),
                     jnp.where(lane < 2 * POOL_GROUP, behind(p4, 2),
                               jnp.where(lane < 3 * POOL_GROUP, behind(p8, 4), p16[:n])))


def _pool_kernel(ul_ref, uc_ref, icg_ref, icx_ref, bd_ref, ps_ref, al_ref, ac_ref,
                 z_ref, y_ref, dl_ref, zc_ref, *, rows):
    gw, pad = GRID_W, POOL_HALF
    stride = gw + 2 * pad
    n_ctx = uc_ref.shape[1]
    lane = lax.broadcasted_iota(jnp.int32, (gw, POOL_WIDTH), 1)
    zero_rows = jnp.zeros((pad * gw, POOL_WIDTH), F32)
    zero_pad = jnp.zeros((pad, POOL_WIDTH), F32)

    z_ref[0:pad * gw, :] = zero_rows
    z_ref[pad * gw:(rows + pad) * gw, :] = ul_ref[0]
    z_ref[(rows + pad) * gw:(rows + 2 * pad) * gw, :] = zero_rows

    def row_pass(r, carry):
        base = pl.multiple_of((r + pad) * gw, 8)
        s = _nested_window_sums(lambda d: z_ref[pl.ds(base + d * gw, gw), :], lane)
        yb = pl.multiple_of(r * stride, 8)
        y_ref[pl.ds(yb, pad), :] = zero_pad
        y_ref[pl.ds(yb + pad, gw), :] = s * icg_ref[pl.ds(r, 1), :]
        y_ref[pl.ds(yb + pad + gw, pad), :] = zero_pad
        return carry

    lax.fori_loop(0, rows, row_pass, 0)

    def col_pass(r, carry):
        slab = y_ref[pl.ds(pl.multiple_of(r * stride, 8), stride), :]
        s = _window_sums_1d(slab, gw, lane)
        tok = pl.ds(pl.multiple_of(r * gw, 8), gw)
        dl_ref[tok, :] = (s * icg_ref[...] - ul_ref[0, tok, :]).astype(BF16)
        return carry

    lax.fori_loop(0, rows, col_pass, 0)
    al_ref[0] = (_dot(dl_ref[...], bd_ref[...]) * ps_ref[...]).astype(BF16)

    uc = uc_ref[0]
    zc_ref[0:pad, :] = zero_pad
    zc_ref[pad:pad + n_ctx, :] = uc
    zc_ref[pad + n_ctx:pad + n_ctx + pad, :] = zero_pad
    lane_c = lax.broadcasted_iota(jnp.int32, (n_ctx, POOL_WIDTH), 1)
    sc = _window_sums_1d(zc_ref[...], n_ctx, lane_c)
    dc = (sc * icx_ref[...] - uc).astype(BF16)
    ac_ref[0] = (_dot(dc, bd_ref[...]) * ps_ref[...]).astype(BF16)


def _pool(u_lat, u_ctx, icg, icx, bd, ps):
    bsz, s, w = u_lat.shape
    n_ctx = u_ctx.shape[1]
    rows = s // GRID_W
    stride = GRID_W + 2 * POOL_HALF
    return pl.pallas_call(
        functools.partial(_pool_kernel, rows=rows),
        grid=(bsz,),
        in_specs=[pl.BlockSpec((1, s, w), lambda b: (b, 0, 0)), pl.BlockSpec((1, n_ctx, w), lambda b: (b, 0, 0)),
                  _const_spec(icg.shape), _const_spec(icx.shape), _const_spec(bd.shape), _const_spec((1, w))],
        out_specs=[pl.BlockSpec((1, s, w), lambda b: (b, 0, 0)), pl.BlockSpec((1, n_ctx, w), lambda b: (b, 0, 0))],
        out_shape=[jax.ShapeDtypeStruct((bsz, s, w), BF16), jax.ShapeDtypeStruct((bsz, n_ctx, w), BF16)],
        scratch_shapes=[pltpu.VMEM(((rows + 2 * POOL_HALF) * GRID_W, w), F32), pltpu.VMEM((rows * stride, w), F32),
                        pltpu.VMEM((s, w), BF16), pltpu.VMEM((n_ctx + 2 * POOL_HALF, w), F32)],
        compiler_params=_params(1),
        name="pool",
    )(u_lat, u_ctx, icg, icx, bd, ps.reshape(1, w))


def _attn_units(once_ref, heads, vrows, prepare, qt_of, k_refs, vt_refs, s_refs, mx_refs, l_refs, acc_refs, finish):
    n_units = len(heads)
    n = s_refs[0].shape[1]
    offs, off = [], 0
    for k_ref in k_refs:
        offs.append(off)
        off += k_ref.shape[1]

    def loop_body(fns):
        def whole(t, carry):
            for k_ref, vt_ref, o in zip(k_refs, vt_refs, offs):
                n_chunks, kc = vt_ref.shape[1], vt_ref.shape[3]
                for c in range(n_chunks):
                    for fn in fns["chunk"]:
                        fn(k_ref, vt_ref, c, slice(c * kc, (c + 1) * kc), slice(o + c * kc, o + (c + 1) * kc))
            for fn in fns["once"]:
                fn()
            return carry

        lax.fori_loop(0, once_ref[0], whole, 0)

    def scores(u):
        slot = u % 2
        ksl = slice(heads[u] * HEAD_PAD, (heads[u] + 1) * HEAD_PAD)

        def fn(k_ref, vt_ref, c, krows, srows):
            st = _dot(k_ref[0, krows, ksl], qt_of(u))
            s_refs[slot][srows, :] = st
            mx_refs[slot][...] = jnp.maximum(mx_refs[slot][...], jnp.max(st.reshape(-1, 8, n), axis=0))

        return fn

    def probs(u, m):
        slot = u % 2
        vsl = slice(heads[u] * vrows, (heads[u] + 1) * vrows)

        def fn(k_ref, vt_ref, c, krows, srows):
            p = jnp.exp2(s_refs[slot][srows, :] - m)
            if l_refs is not None:
                l_refs[slot][...] += jnp.sum(p.reshape(-1, 8, n), axis=0)
            acc_refs[slot][...] += _dot(vt_ref[0, c, vsl, :], p.astype(BF16))

        return fn

    def epilogue(u):
        slot = u % 2
        return lambda: finish(u, acc_refs[slot][...],
                              None if l_refs is None else jnp.sum(l_refs[slot][...], axis=0, keepdims=True))

    neg_inf = jnp.full(mx_refs[0].shape, -jnp.inf, F32)
    mx_refs[0][...] = neg_inf
    prepare(0)
    loop_body({"chunk": [scores(0)], "once": [lambda: prepare(1)] if n_units > 1 else []})
    for u in range(n_units):
        slot = u % 2
        m = jnp.max(mx_refs[slot][...], axis=0, keepdims=True)
        if l_refs is not None:
            l_refs[slot][...] = jnp.zeros(l_refs[slot].shape, F32)
        acc_refs[slot][...] = jnp.zeros(acc_refs[slot].shape, F32)
        fns = {"chunk": [probs(u, m)], "once": [epilogue(u - 1)] if u > 0 else []}
        if u + 2 < n_units:
            fns["once"].append(functools.partial(prepare, u + 2))
        if u + 1 < n_units:
            mx_refs[1 - slot][...] = neg_inf
            fns["chunk"].insert(0, scores(u + 1))
        loop_body(fns)
    epilogue(n_units - 1)()


def _mla_attn_kernel(once_ref, *refs, nseg, sub):
    q_ref, k_refs, vt_refs = refs[0], refs[1:1 + nseg], refs[1 + nseg:1 + 2 * nseg]
    o_ref, s0_ref, s1_ref, mx0_ref, mx1_ref, acc0_ref, acc1_ref, ot_ref, qt_ref = refs[1 + 2 * nseg:]
    units = [(sb, h) for sb in range(q_ref.shape[1] // sub) for h in range(MLA_HEADS)]

    def prepare(u):
        sb, h = units[u]
        qh = q_ref[0, sb * sub:(sb + 1) * sub, h * HEAD_PAD:(h + 1) * HEAD_PAD]
        qt_ref[u] = qh.astype(F32).T.astype(BF16)

    def finish(u, acc, l):
        sb, h = units[u]
        ot_ref[h * MLA_V:(h + 1) * MLA_V, sb * sub:(sb + 1) * sub] = acc[:MLA_V] * (1.0 / acc[MLA_V:MLA_V + 1])

    _attn_units(once_ref, [h for _, h in units], MLA_VT_ROWS, prepare, lambda u: qt_ref[u], k_refs, vt_refs,
                (s0_ref, s1_ref), (mx0_ref, mx1_ref), None, (acc0_ref, acc1_ref), finish)
    o_ref[0] = ot_ref[...].T.astype(BF16)


def _diff_attn_kernel(once_ref, *refs, nseg, sub, lam_init):
    dl_ref, sub_ref, q_ref = refs[0], refs[1], refs[2]
    k_refs, vt_refs = refs[3:3 + nseg], refs[3 + nseg:3 + 2 * nseg]
    o_ref, s0_ref, s1_ref, mx0_ref, mx1_ref, acc0_ref, acc1_ref, l0_ref, l1_ref, qu_ref = refs[3 + 2 * nseg:]
    units = [(sb, h) for sb in range(q_ref.shape[1] // sub) for h in range(DIFF_HEADS)]
    dl = dl_ref[...]
    lam = (jnp.exp(jnp.sum(dl[0:1] * dl[1:2], axis=-1, keepdims=True))
           - jnp.exp(jnp.sum(dl[2:3] * dl[3:4], axis=-1, keepdims=True)) + lam_init)
    dim = lax.broadcasted_iota(jnp.int32, (HEAD_PAD, sub), 0)
    def prepare(u):
        sb, h = units[u]
        qt = q_ref[0, sb * sub:(sb + 1) * sub, h * HEAD_PAD:(h + 1) * HEAD_PAD].astype(F32).T
        qu_ref[u, :, :sub] = jnp.where(dim < DIFF_DIM, qt, 0.0).astype(BF16)
        qu_ref[u, :, sub:] = jnp.where(dim >= DIFF_DIM, qt, 0.0).astype(BF16)

    def finish(u, acc, l):
        sb, h = units[u]
        o = acc[:, :sub] * (1.0 / l[:, :sub]) - acc[:, sub:] * (lam / l[:, sub:])
        on = o * lax.rsqrt(jnp.mean(o * o, axis=0, keepdims=True) + EPS)
        o_ref[0, sb * sub:(sb + 1) * sub, h * DIFF_V:(h + 1) * DIFF_V] = (
            on.T * sub_ref[...] * (1.0 - lam_init)).astype(BF16)

    _attn_units(once_ref, [h for _, h in units], DIFF_V, prepare, lambda u: qu_ref[u], k_refs, vt_refs,
                (s0_ref, s1_ref), (mx0_ref, mx1_ref), (l0_ref, l1_ref), (acc0_ref, acc1_ref), finish)


def _attention(kernel, name, q, ks, vts, extra, *, tq, sub, w_out, maps_per_head):
    bsz, nq, wq = q.shape
    nseg = len(ks)
    n_heads = wq // HEAD_PAD
    whole = lambda a: pl.BlockSpec((1,) + a.shape[1:], lambda b, i: (b,) + (0,) * (a.ndim - 1))
    vrows = vts[0].shape[2] // n_heads
    n_keys = sum(k.shape[1] for k in ks)
    n = sub * maps_per_head
    scratch = [pltpu.VMEM((n_keys, n), F32)] * 2 + [pltpu.VMEM((8, n), F32)] * 2 + [pltpu.VMEM((vrows, n), F32)] * 2
    qt_scratch = pltpu.VMEM((n_heads * tq // sub, HEAD_PAD, n), BF16)
    if maps_per_head > 1:
        scratch += [pltpu.VMEM((8, n), F32)] * 2 + [qt_scratch]
    else:
        scratch += [pltpu.VMEM((w_out, tq), F32), qt_scratch]
    once = jnp.ones((1,), jnp.int32)
    return pl.pallas_call(
        functools.partial(kernel, nseg=nseg, sub=sub),
        grid=(bsz, nq // tq),
        in_specs=[pl.BlockSpec(memory_space=pltpu.SMEM)] + [_const_spec(e.shape) for e in extra]
        + [pl.BlockSpec((1, tq, wq), lambda b, i: (b, i, 0))] + [whole(k) for k in ks] + [whole(v) for v in vts],
        out_specs=pl.BlockSpec((1, tq, w_out), lambda b, i: (b, i, 0)),
        out_shape=jax.ShapeDtypeStruct((bsz, nq, w_out), BF16),
        scratch_shapes=scratch,
        compiler_params=_params(2),
        name=name,
    )(once, *extra, q, *ks, *vts)


def _axial_angles(n_tok, rot_dim):
    t = jnp.arange(n_tok)
    row = (t // GRID_W).astype(F32)
    col = (t % GRID_W).astype(F32)
    n_axis = rot_dim // 4
    inv = ROPE_BASE ** (-jnp.arange(n_axis, dtype=F32) / n_axis)
    return jnp.concatenate([row[:, None] * inv, col[:, None] * inv], axis=-1)


def _rope_tables(n_tok, n_ctx):
    am, ad = _axial_angles(n_tok, MLA_ROPE), _axial_angles(n_tok, DIFF_DIM)
    one = lambda w: jnp.ones((n_tok, w), F32)
    zero = lambda w: jnp.zeros((n_tok, w), F32)
    cm = jnp.concatenate([one(MLA_NOPE), jnp.cos(am), jnp.cos(am), one(HEAD_PAD - MLA_NOPE - MLA_ROPE)], axis=-1)
    sam = jnp.concatenate([zero(MLA_NOPE), -jnp.sin(am), zero(HEAD_PAD - MLA_NOPE - MLA_ROPE // 2)], axis=-1)
    sbm = jnp.concatenate([zero(MLA_NOPE + MLA_ROPE // 2), jnp.sin(am), zero(HEAD_PAD - MLA_NOPE - MLA_ROPE)], axis=-1)
    cd = jnp.tile(jnp.cos(ad), (1, 4))
    sad = jnp.tile(jnp.concatenate([-jnp.sin(ad), zero(DIFF_DIM // 2)], axis=-1), (1, 2))
    sbd = jnp.tile(jnp.concatenate([zero(DIFF_DIM // 2), jnp.sin(ad)], axis=-1), (1, 2))
    lat = (cm, sam, sbm, cd, sad, sbd)
    ident = (jnp.ones((n_ctx, LANES), F32),) + (jnp.zeros((n_ctx, LANES), F32),) * 2
    return lat, ident + ident


def _inv_counts(n):
    idx = jnp.arange(n)
    cols = []
    for k in POOL_WINDOWS:
        lo, hi = k // 2, k - 1 - k // 2
        cnt = (jnp.clip(idx + hi + 1, 0, n) - jnp.clip(idx - lo, 0, n)).astype(F32)
        cols.append(jnp.broadcast_to((1.0 / cnt)[:, None], (n, POOL_GROUP)))
    return jnp.concatenate(cols, axis=-1)


def _layer_weights(w_in, w_out, pool_w, mla_w_uq, mla_w_ukv):
    d = w_in.shape[0]
    s1, s2 = POOL_WIDTH, POOL_WIDTH + MLA_Q_RANK + MLA_KV_RANK + MLA_ROPE
    kr_pad = jnp.zeros((d, HEAD_PAD), F32).at[:, MLA_NOPE:MLA_NOPE + MLA_ROPE].set(w_in[:, s2 - MLA_ROPE:s2])
    s3 = s2 + 2 * DIFF_HEADS * 2 * DIFF_DIM
    wall = jnp.concatenate([w_in[:, :s2 - MLA_ROPE], kr_pad, w_in[:, s2:s3]], axis=-1).astype(BF16)
    wdvt = w_in[:, s3:].T.astype(BF16)
    qd = MLA_NOPE + MLA_ROPE
    wuq = jnp.pad(mla_w_uq.reshape(MLA_Q_RANK, MLA_HEADS, qd), ((0, 0), (0, 0), (0, HEAD_PAD - qd)))
    wuq = wuq.reshape(MLA_Q_RANK, MLA_HEADS * HEAD_PAD).astype(BF16)
    ukv = mla_w_ukv.reshape(MLA_KV_RANK, MLA_HEADS, MLA_NOPE + MLA_V)
    wk = jnp.pad(ukv[:, :, :MLA_NOPE], ((0, 0), (0, 0), (0, HEAD_PAD - MLA_NOPE)))
    wk = wk.reshape(MLA_KV_RANK, MLA_HEADS * HEAD_PAD).astype(BF16)
    wvt = ukv[:, :, MLA_NOPE:].reshape(MLA_KV_RANK, MLA_HEADS * MLA_V).T.astype(BF16)
    bd = jax.scipy.linalg.block_diag(*[pool_w[g] for g in range(len(POOL_WINDOWS))]).astype(BF16)
    return wall, wdvt, wuq, wk, wvt, bd, w_out.astype(BF16)


def kernel(x, c, ctx, c_ctx, w_mod, b_mod, ffn1_norm, ffn1_w1, ffn1_w3, ffn1_w2, mix_norm, w_in, w_out, pool_w, pool_scale, mla_q_norm, mla_w_uq, mla_kv_norm, mla_w_ukv, diff_lambda, diff_subln, ffn2_norm, ffn2_w1, ffn2_w3, ffn2_w2, final_norm):
    bsz, seq, d = x.shape
    n_ctx = ctx.shape[1]
    depth = w_mod.shape[0]
    assert bsz + 1 <= MOD_ROWS and seq % GRID_W == 0
    tm = 512
    ctx_row = bsz

    cc = jnp.zeros((MOD_ROWS, d), F32).at[:bsz].set(c).at[ctx_row].set(c_ctx)
    mods_all = _modulation(cc, w_mod, b_mod).reshape(depth, MOD_ROWS, N_MOD, d)

    tab_lat, tab_ctx = _rope_tables(seq, n_ctx)
    icg, icx = _inv_counts(GRID_W), _inv_counts(n_ctx)
    bf = lambda w: w.astype(BF16)
    mla_args = dict(w_out=MLA_HEADS * MLA_V, maps_per_head=1)
    dif_args = dict(w_out=DIFF_HEADS * DIFF_V, maps_per_head=2)

    h, hc = x, ctx
    for i in range(depth):
        last = i == depth - 1
        mods = mods_all[i]
        lam_init = 0.8 - 0.6 * math.exp(-0.3 * i)
        wall, wdvt, wuq, wk, wvt, bd, wo = _layer_weights(w_in[i], w_out[i], pool_w[i], mla_w_uq[i], mla_w_ukv[i])
        f1 = (ffn1_norm[i], bf(ffn1_w1[i]), bf(ffn1_w3[i]), bf(ffn1_w2[i]))
        f2 = (ffn2_norm[i], bf(ffn2_w1[i]), bf(ffn2_w3[i]), bf(ffn2_w2[i]))

        h = _ffn(h, mods, None, *f1, k0=0, tm=tm)
        hc = _ffn(hc, mods, ctx_row, *f1, k0=0, tm=n_ctx)

        pw = (mix_norm[i], wall, wdvt, mla_q_norm[i], mla_kv_norm[i], wuq, wk, wvt)
        u, qm, km, vm, qd, kd, vd = _proj(h, mods, None, *pw, tab_lat, tm=tm)
        uc, qmc, kmc, vmc, qdc, kdc, vdc = _proj(hc, mods, ctx_row, *pw, tab_ctx, tm=n_ctx)

        a, ac = _pool(u, uc, icg, icx, bd, pool_scale[i])
        dif = functools.partial(_diff_attn_kernel, lam_init=lam_init)
        dextra = (diff_lambda[i], diff_subln[i].reshape(1, DIFF_V))
        b = _attention(_mla_attn_kernel, "mla_attn", qm, (km, kmc), (vm, vmc), (),
                       tq=2048, sub=512, **mla_args)
        cdiff = _attention(dif, "diff_attn", qd, (kd, kdc), (vd, vdc), dextra,
                           tq=1024, sub=256, **dif_args)
        h = _ffn(h, mods, None, *f2, k0=6, tm=tm, mix=(a, b, cdiff, wo),
                 final_g=final_norm if last else None)
        if not last:
            bc = _attention(_mla_attn_kernel, "mla_attn_ctx", qmc, (kmc,), (vmc,), (),
                            tq=n_ctx, sub=n_ctx, **mla_args)
            cc_ = _attention(dif, "diff_attn_ctx", qdc, (kdc,), (vdc,), dextra,
                             tq=n_ctx, sub=n_ctx, **dif_args)
            hc = _ffn(hc, mods, ctx_row, *f2, k0=6, tm=n_ctx, mix=(ac, bc, cc_, wo))
    return h
```

```python
import functools
import math

import jax
import jax.numpy as jnp
from jax import lax
from jax.experimental import pallas as pl
from jax.experimental.pallas import tpu as pltpu

F32 = jnp.float32
BF16 = jnp.bfloat16

GRID_W = 64
N_MOD = 9
POOL_WINDOWS = (2, 4, 8, 16)
POOL_GROUP = 64
POOL_WIDTH = POOL_GROUP * len(POOL_WINDOWS)
POOL_HALF = max(POOL_WINDOWS) // 2
MLA_HEADS = 4
MLA_NOPE = 64
MLA_ROPE = 32
MLA_V = 64
MLA_Q_RANK = 384
MLA_KV_RANK = 256
DIFF_HEADS = 4
DIFF_DIM = 64
DIFF_V = 2 * DIFF_DIM
ROPE_BASE = 10000.0
EPS = 1e-6

LANES = 128
HEAD_PAD = LANES
SUM_ROWS = 16
MLA_VT_ROWS = MLA_V + SUM_ROWS
MOD_ROWS = 24
VMEM_LIMIT = 56 * 1024 * 1024

C_POOL = 0
C_CQ = C_POOL + POOL_WIDTH
C_CKV = C_CQ + MLA_Q_RANK
C_KR = C_CKV + MLA_KV_RANK
C_DQ = C_KR + HEAD_PAD
C_DK = C_DQ + DIFF_HEADS * 2 * DIFF_DIM
C_DV = C_DK + DIFF_HEADS * 2 * DIFF_DIM

LOG2E = math.log2(math.e)
MLA_QSCALE = (MLA_NOPE + MLA_ROPE) ** -0.5 * LOG2E
DIFF_QSCALE = DIFF_DIM ** -0.5 * LOG2E


def _dot(a, b):
    return jnp.dot(a, b, preferred_element_type=F32)


def _dot_nt(a, b):
    return lax.dot_general(a, b, (((1,), (1,)), ((), ())), preferred_element_type=F32)


def _rms(x, g):
    return x * lax.rsqrt(jnp.mean(x * x, axis=-1, keepdims=True) + EPS) * g


def _silu(a):
    return a * (1.0 / (1.0 + jnp.exp(-a)))


def _const_spec(shape):
    return pl.BlockSpec(shape, lambda *_: (0,) * len(shape), pipeline_mode=pl.Buffered(1))


def _params(n_axes):
    return pltpu.CompilerParams(dimension_semantics=("parallel",) * n_axes, vmem_limit_bytes=VMEM_LIMIT)


def _mod_kernel(cc_ref, w_ref, b_ref, o_ref):
    sc = _silu(cc_ref[...])
    o_ref[0] = jnp.dot(sc, w_ref[0], preferred_element_type=F32, precision=lax.Precision.HIGHEST) + b_ref[0]


def _modulation(cc, w_mod, b_mod):
    depth, d, width = w_mod.shape
    bn = 9 * LANES
    return pl.pallas_call(
        _mod_kernel,
        grid=(depth, width // bn),
        in_specs=[
            pl.BlockSpec((MOD_ROWS, d), lambda l, j: (0, 0)),
            pl.BlockSpec((1, d, bn), lambda l, j: (l, 0, j)),
            pl.BlockSpec((1, 1, bn), lambda l, j: (l, 0, j)),
        ],
        out_specs=pl.BlockSpec((1, MOD_ROWS, bn), lambda l, j: (l, 0, j)),
        out_shape=jax.ShapeDtypeStruct((depth, MOD_ROWS, width), F32),
        compiler_params=_params(2),
        name="modulation",
    )(cc, w_mod, b_mod.reshape(depth, 1, width))


MXU_COLS = 256
FFN_CHUNK = MXU_COLS


def _ffn_kernel(*refs, pre, final, k0):
    it = iter(refs)
    x_ref, m_ref = next(it), next(it)
    if pre:
        a_ref, b_ref, c_ref, wo_ref = next(it), next(it), next(it), next(it)
    g_ref, w1_ref, w3_ref, w2_ref = next(it), next(it), next(it), next(it)
    if final:
        fg_ref = next(it)
    o_ref, hmid_ref = next(it), next(it)

    x = x_ref[0]
    if pre:
        mix = jnp.concatenate([a_ref[0], b_ref[0], c_ref[0]], axis=-1)
        x = x + m_ref[0, 5:6, :] * _dot(mix, wo_ref[...])
    xm = _rms(x, g_ref[...]) * (1.0 + m_ref[0, k0 + 1:k0 + 2, :]) + m_ref[0, k0:k0 + 1, :]
    xb = xm.astype(BF16)
    dff = w1_ref.shape[1]
    for lo in range(0, dff, FFN_CHUNK):
        cols = slice(lo, min(lo + FFN_CHUNK, dff))
        a = _dot(xb, w1_ref[:, cols])
        b = _dot(xb, w3_ref[:, cols])
        hmid_ref[:, cols] = (_silu(a) * b).astype(BF16)
    y = _dot(hmid_ref[...], w2_ref[...])
    out = x + (0.5 * m_ref[0, k0 + 2:k0 + 3, :]) * y
    if final:
        out = _rms(out, fg_ref[...])
    o_ref[0] = out


def _ffn(x, mods, mod_row, g, w1, w3, w2, *, k0, tm, mix=None, final_g=None):
    bsz, n, d = x.shape
    dff = w1.shape[1]
    pre, final = mix is not None, final_g is not None
    tok = lambda w: pl.BlockSpec((1, tm, w), lambda b, i: (b, i, 0))
    mod_spec = pl.BlockSpec((1, N_MOD, d), (lambda b, i: (b, 0, 0)) if mod_row is None else (lambda b, i: (mod_row, 0, 0)))
    args, specs = [x, mods], [tok(d), mod_spec]
    if pre:
        a, bb, c, wo = mix
        args += [a, bb, c, wo]
        specs += [tok(a.shape[-1]), tok(bb.shape[-1]), tok(c.shape[-1]), _const_spec(wo.shape)]
    args += [g.reshape(1, d), w1, w3, w2]
    specs += [_const_spec((1, d)), _const_spec(w1.shape), _const_spec(w3.shape), _const_spec(w2.shape)]
    if final:
        args.append(final_g.reshape(1, d))
        specs.append(_const_spec((1, d)))
    return pl.pallas_call(
        functools.partial(_ffn_kernel, pre=pre, final=final, k0=k0),
        grid=(bsz, n // tm),
        in_specs=specs,
        out_specs=tok(d),
        out_shape=jax.ShapeDtypeStruct(x.shape, F32),
        scratch_shapes=[pltpu.VMEM((tm, dff), BF16)],
        compiler_params=_params(2),
        name="ffn_mix" if pre else "ffn",
    )(*args)


def _rope(x, c, sa, sb, shift):
    return x * c + pltpu.roll(x, LANES - shift, 1) * sa + pltpu.roll(x, shift, 1) * sb


def _proj_kernel(x_ref, m_ref, g_ref, wall_ref, wdvt_ref, qn_ref, kvn_ref, wuq_ref, wk_ref, wvt_ref,
                 cm_ref, sam_ref, sbm_ref, cd_ref, sad_ref, sbd_ref,
                 u_ref, qm_ref, km_ref, vmt_ref, qd_ref, kd_ref, vdt_ref):
    x = x_ref[0]
    n = _rms(x, g_ref[...]) * (1.0 + m_ref[0, 4:5, :]) + m_ref[0, 3:4, :]
    nb = n.astype(BF16)
    proj = lambda lo, hi: _dot(nb, wall_ref[:, lo:hi])

    pm = proj(C_CQ, C_DQ)
    cqn = _rms(pm[:, :C_CKV - C_CQ], qn_ref[...]).astype(BF16)
    ckvn = _rms(pm[:, C_CKV - C_CQ:C_KR - C_CQ], kvn_ref[...]).astype(BF16)
    q = _dot(cqn, wuq_ref[...])
    k = _dot(ckvn, wk_ref[...])
    vt = _dot_nt(wvt_ref[...], ckvn)
    ones = jnp.ones((SUM_ROWS, vt.shape[1]), BF16)
    for h in range(MLA_HEADS):
        vmt_ref[0, 0, h * MLA_VT_ROWS:h * MLA_VT_ROWS + MLA_V, :] = vt[h * MLA_V:(h + 1) * MLA_V].astype(BF16)
        vmt_ref[0, 0, h * MLA_VT_ROWS + MLA_V:(h + 1) * MLA_VT_ROWS, :] = ones
    cm, sam, sbm = cm_ref[...], sam_ref[...], sbm_ref[...]
    half_m = MLA_ROPE // 2
    kr = _rope(pm[:, C_KR - C_CQ:], cm, sam, sbm, half_m)
    for h in range(MLA_HEADS):
        sl = slice(h * HEAD_PAD, (h + 1) * HEAD_PAD)
        qm_ref[0, :, sl] = (_rope(q[:, sl], cm, sam, sbm, half_m) * MLA_QSCALE).astype(BF16)
        km_ref[0, :, sl] = (k[:, sl] + kr).astype(BF16)

    cd, sad, sbd = cd_ref[...], sad_ref[...], sbd_ref[...]
    half_d = DIFF_DIM // 2
    for col0, out_ref, scale in ((C_DQ, qd_ref, DIFF_QSCALE), (C_DK, kd_ref, None)):
        for h0 in range(0, DIFF_HEADS, 2):
            pp = proj(col0 + h0 * HEAD_PAD, col0 + (h0 + 2) * HEAD_PAD)
            for h in (h0, h0 + 1):
                r = _rope(pp[:, (h - h0) * HEAD_PAD:(h - h0 + 1) * HEAD_PAD], cd, sad, sbd, half_d)
                out_ref[0, :, h * HEAD_PAD:(h + 1) * HEAD_PAD] = (r if scale is None else r * scale).astype(BF16)
    u_ref[0] = proj(C_POOL, C_CQ)
    vdt_ref[0, 0] = _dot_nt(wdvt_ref[...], nb).astype(BF16)


def _proj(x, mods, mod_row, g, wall, wdvt, qn, kvn, wuq, wk, wvt, tables, *, tm):
    bsz, n, d = x.shape
    tok = lambda w: pl.BlockSpec((1, tm, w), lambda b, i: (b, i, 0))
    vt = lambda r: pl.BlockSpec((1, 1, r, tm), lambda b, i: (b, i, 0, 0))
    mod_spec = pl.BlockSpec((1, N_MOD, d), (lambda b, i: (b, 0, 0)) if mod_row is None else (lambda b, i: (mod_row, 0, 0)))
    tab = pl.BlockSpec((tm, LANES), lambda b, i: (i, 0))
    wm, wd = MLA_HEADS * HEAD_PAD, DIFF_HEADS * HEAD_PAD
    rm, rd = MLA_HEADS * MLA_VT_ROWS, DIFF_HEADS * DIFF_V
    tok_sds = lambda w, dt: jax.ShapeDtypeStruct((bsz, n, w), dt)
    vt_sds = lambda r: jax.ShapeDtypeStruct((bsz, n // tm, r, tm), BF16)
    return pl.pallas_call(
        _proj_kernel,
        grid=(bsz, n // tm),
        in_specs=[tok(d), mod_spec, _const_spec((1, d)), _const_spec(wall.shape), _const_spec(wdvt.shape),
                  _const_spec((1, MLA_Q_RANK)), _const_spec((1, MLA_KV_RANK)),
                  _const_spec(wuq.shape), _const_spec(wk.shape), _const_spec(wvt.shape)] + [tab] * 6,
        out_specs=[tok(POOL_WIDTH), tok(wm), tok(wm), vt(rm), tok(wd), tok(wd), vt(rd)],
        out_shape=[tok_sds(POOL_WIDTH, F32), tok_sds(wm, BF16), tok_sds(wm, BF16), vt_sds(rm),
                   tok_sds(wd, BF16), tok_sds(wd, BF16), vt_sds(rd)],
        compiler_params=_params(2),
        name="mix_in",
    )(x, mods, g.reshape(1, d), wall, wdvt, qn.reshape(1, -1), kvn.reshape(1, -1), wuq, wk, wvt, *tables)


def _nested_window_sums(load, lane):
    a2 = load(-1) + load(0)
    a4 = a2 + load(-2) + load(1)
    a8 = a4 + load(-4) + load(-3) + load(2) + load(3)
    a16 = a8 + load(-8) + load(-7) + load(-6) + load(-5) + load(4) + load(5) + load(6) + load(7)
    return jnp.where(lane < POOL_GROUP, a2, jnp.where(lane < 2 * POOL_GROUP, a4, jnp.where(lane < 3 * POOL_GROUP, a8, a16)))


def _window_sums_1d(slab, n, lane):
    size = slab.shape[0]
    ahead = lambda x, k: pltpu.roll(x, size - k, 0)
    behind = lambda x, k: pltpu.roll(x, k, 0)[POOL_HALF:POOL_HALF + n]
    p2 = slab + ahead(slab, 1)
    p4 = p2 + ahead(p2, 2)
    p8 = p4 + ahead(p4, 4)
    p16 = p8 + ahead(p8, 8)
    return jnp.where(lane < POOL_GROUP, behind(p2, 1),
                     jnp.where(lane < 2 * POOL_GROUP, behind(p4, 2),
                               jnp.where(lane < 3 * POOL_GROUP, behind(p8, 4), p16[:n])))


def _pool_kernel(ul_ref, uc_ref, icg_ref, icx_ref, bd_ref, ps_ref, al_ref, ac_ref,
                 z_ref, y_ref, dl_ref, zc_ref, *, rows):
    gw, pad = GRID_W, POOL_HALF
    stride = gw + 2 * pad
    n_ctx = uc_ref.shape[1]
    lane = lax.broadcasted_iota(jnp.int32, (gw, POOL_WIDTH), 1)
    zero_rows = jnp.zeros((pad * gw, POOL_WIDTH), F32)
    zero_pad = jnp.zeros((pad, POOL_WIDTH), F32)

    z_ref[0:pad * gw, :] = zero_rows
    z_ref[pad * gw:(rows + pad) * gw, :] = ul_ref[0]
    z_ref[(rows + pad) * gw:(rows + 2 * pad) * gw, :] = zero_rows

    def row_pass(r, carry):
        base = pl.multiple_of((r + pad) * gw, 8)
        s = _nested_window_sums(lambda d: z_ref[pl.ds(base + d * gw, gw), :], lane)
        yb = pl.multiple_of(r * stride, 8)
        y_ref[pl.ds(yb, pad), :] = zero_pad
        y_ref[pl.ds(yb + pad, gw), :] = s * icg_ref[pl.ds(r, 1), :]
        y_ref[pl.ds(yb + pad + gw, pad), :] = zero_pad
        return carry

    lax.fori_loop(0, rows, row_pass, 0)

    def col_pass(r, carry):
        slab = y_ref[pl.ds(pl.multiple_of(r * stride, 8), stride), :]
        s = _window_sums_1d(slab, gw, lane)
        tok = pl.ds(pl.multiple_of(r * gw, 8), gw)
        dl_ref[tok, :] = (s * icg_ref[...] - ul_ref[0, tok, :]).astype(BF16)
        return carry

    lax.fori_loop(0, rows, col_pass, 0)
    al_ref[0] = (_dot(dl_ref[...], bd_ref[...]) * ps_ref[...]).astype(BF16)

    uc = uc_ref[0]
    zc_ref[0:pad, :] = zero_pad
    zc_ref[pad:pad + n_ctx, :] = uc
    zc_ref[pad + n_ctx:pad + n_ctx + pad, :] = zero_pad
    lane_c = lax.broadcasted_iota(jnp.int32, (n_ctx, POOL_WIDTH), 1)
    sc = _window_sums_1d(zc_ref[...], n_ctx, lane_c)
    dc = (sc * icx_ref[...] - uc).astype(BF16)
    ac_ref[0] = (_dot(dc, bd_ref[...]) * ps_ref[...]).astype(BF16)


def _pool(u_lat, u_ctx, icg, icx, bd, ps):
    bsz, s, w = u_lat.shape
    n_ctx = u_ctx.shape[1]
    rows = s // GRID_W
    stride = GRID_W + 2 * POOL_HALF
    return pl.pallas_call(
        functools.partial(_pool_kernel, rows=rows),
        grid=(bsz,),
        in_specs=[pl.BlockSpec((1, s, w), lambda b: (b, 0, 0)), pl.BlockSpec((1, n_ctx, w), lambda b: (b, 0, 0)),
                  _const_spec(icg.shape), _const_spec(icx.shape), _const_spec(bd.shape), _const_spec((1, w))],
        out_specs=[pl.BlockSpec((1, s, w), lambda b: (b, 0, 0)), pl.BlockSpec((1, n_ctx, w), lambda b: (b, 0, 0))],
        out_shape=[jax.ShapeDtypeStruct((bsz, s, w), BF16), jax.ShapeDtypeStruct((bsz, n_ctx, w), BF16)],
        scratch_shapes=[pltpu.VMEM(((rows + 2 * POOL_HALF) * GRID_W, w), F32), pltpu.VMEM((rows * stride, w), F32),
                        pltpu.VMEM((s, w), BF16), pltpu.VMEM((n_ctx + 2 * POOL_HALF, w), F32)],
        compiler_params=_params(1),
        name="pool",
    )(u_lat, u_ctx, icg, icx, bd, ps.reshape(1, w))


def _attn_units(once_ref, heads, vrows, prepare, qt_of, k_refs, vt_refs, s_refs, mx_refs, l_refs, acc_refs, finish):
    n_units = len(heads)
    n = s_refs[0].shape[1]
    offs, off = [], 0
    for k_ref in k_refs:
        offs.append(off)
        off += k_ref.shape[1]

    def loop_body(fns):
        def whole(t, carry):
            for k_ref, vt_ref, o in zip(k_refs, vt_refs, offs):
                n_chunks, kc = vt_ref.shape[1], vt_ref.shape[3]
                for c in range(n_chunks):
                    for fn in fns["chunk"]:
                        fn(k_ref, vt_ref, c, slice(c * kc, (c + 1) * kc), slice(o + c * kc, o + (c + 1) * kc))
            for fn in fns["once"]:
                fn()
            return carry

        lax.fori_loop(0, once_ref[0], whole, 0)

    def scores(u):
        slot = u % 2
        ksl = slice(heads[u] * HEAD_PAD, (heads[u] + 1) * HEAD_PAD)

        def fn(k_ref, vt_ref, c, krows, srows):
            st = _dot(k_ref[0, krows, ksl], qt_of(u))
            s_refs[slot][srows, :] = st
            mx_refs[slot][...] = jnp.maximum(mx_refs[slot][...], jnp.max(st.reshape(-1, 8, n), axis=0))

        return fn

    def probs(u, m):
        slot = u % 2
        vsl = slice(heads[u] * vrows, (heads[u] + 1) * vrows)

        def fn(k_ref, vt_ref, c, krows, srows):
            p = jnp.exp2(s_refs[slot][srows, :] - m)
            if l_refs is not None:
                l_refs[slot][...] += jnp.sum(p.reshape(-1, 8, n), axis=0)
            acc_refs[slot][...] += _dot(vt_ref[0, c, vsl, :], p.astype(BF16))

        return fn

    def epilogue(u):
        slot = u % 2
        return lambda: finish(u, acc_refs[slot][...],
                              None if l_refs is None else jnp.sum(l_refs[slot][...], axis=0, keepdims=True))

    neg_inf = jnp.full(mx_refs[0].shape, -jnp.inf, F32)
    mx_refs[0][...] = neg_inf
    prepare(0)
    loop_body({"chunk": [scores(0)], "once": [lambda: prepare(1)] if n_units > 1 else []})
    for u in range(n_units):
        slot = u % 2
        m = jnp.max(mx_refs[slot][...], axis=0, keepdims=True)
        if l_refs is not None:
            l_refs[slot][...] = jnp.zeros(l_refs[slot].shape, F32)
        acc_refs[slot][...] = jnp.zeros(acc_refs[slot].shape, F32)
        fns = {"chunk": [probs(u, m)], "once": [epilogue(u - 1)] if u > 0 else []}
        if u + 2 < n_units:
            fns["once"].append(functools.partial(prepare, u + 2))
        if u + 1 < n_units:
            mx_refs[1 - slot][...] = neg_inf
            fns["chunk"].insert(0, scores(u + 1))
        loop_body(fns)
    epilogue(n_units - 1)()


def _mla_attn_kernel(once_ref, *refs, nseg, sub):
    q_ref, k_refs, vt_refs = refs[0], refs[1:1 + nseg], refs[1 + nseg:1 + 2 * nseg]
    o_ref, s0_ref, s1_ref, mx0_ref, mx1_ref, acc0_ref, acc1_ref, ot_ref, qt_ref = refs[1 + 2 * nseg:]
    units = [(sb, h) for sb in range(q_ref.shape[1] // sub) for h in range(MLA_HEADS)]

    def prepare(u):
        sb, h = units[u]
        qh = q_ref[0, sb * sub:(sb + 1) * sub, h * HEAD_PAD:(h + 1) * HEAD_PAD]
        qt_ref[u] = qh.astype(F32).T.astype(BF16)

    def finish(u, acc, l):
        sb, h = units[u]
        ot_ref[(h % 2) * MLA_V:(h % 2 + 1) * MLA_V, :] = acc[:MLA_V] * (1.0 / acc[MLA_V:MLA_V + 1])
        if h % 2 == 1:
            o_ref[0, sb * sub:(sb + 1) * sub, (h - 1) * MLA_V:(h + 1) * MLA_V] = ot_ref[...].T.astype(BF16)

    _attn_units(once_ref, [h for _, h in units], MLA_VT_ROWS, prepare, lambda u: qt_ref[u], k_refs, vt_refs,
                (s0_ref, s1_ref), (mx0_ref, mx1_ref), None, (acc0_ref, acc1_ref), finish)


def _diff_attn_kernel(once_ref, *refs, nseg, sub, lam_init):
    dl_ref, sub_ref, q_ref = refs[0], refs[1], refs[2]
    k_refs, vt_refs = refs[3:3 + nseg], refs[3 + nseg:3 + 2 * nseg]
    o_ref, s0_ref, s1_ref, mx0_ref, mx1_ref, acc0_ref, acc1_ref, l0_ref, l1_ref, qu_ref = refs[3 + 2 * nseg:]
    units = [(sb, h) for sb in range(q_ref.shape[1] // sub) for h in range(DIFF_HEADS)]
    dl = dl_ref[...]
    lam = (jnp.exp(jnp.sum(dl[0:1] * dl[1:2], axis=-1, keepdims=True))
           - jnp.exp(jnp.sum(dl[2:3] * dl[3:4], axis=-1, keepdims=True)) + lam_init)
    dim = lax.broadcasted_iota(jnp.int32, (HEAD_PAD, sub), 0)
    def prepare(u):
        sb, h = units[u]
        qt = q_ref[0, sb * sub:(sb + 1) * sub, h * HEAD_PAD:(h + 1) * HEAD_PAD].astype(F32).T
        qu_ref[u, :, :sub] = jnp.where(dim < DIFF_DIM, qt, 0.0).astype(BF16)
        qu_ref[u, :, sub:] = jnp.where(dim >= DIFF_DIM, qt, 0.0).astype(BF16)

    def finish(u, acc, l):
        sb, h = units[u]
        o = acc[:, :sub] * (1.0 / l[:, :sub]) - acc[:, sub:] * (lam / l[:, sub:])
        on = o * lax.rsqrt(jnp.mean(o * o, axis=0, keepdims=True) + EPS)
        o_ref[0, sb * sub:(sb + 1) * sub, h * DIFF_V:(h + 1) * DIFF_V] = (
            on.T * sub_ref[...] * (1.0 - lam_init)).astype(BF16)

    _attn_units(once_ref, [h for _, h in units], DIFF_V, prepare, lambda u: qu_ref[u], k_refs, vt_refs,
                (s0_ref, s1_ref), (mx0_ref, mx1_ref), (l0_ref, l1_ref), (acc0_ref, acc1_ref), finish)


def _attention(kernel, name, q, ks, vts, extra, *, tq, sub, w_out, maps_per_head):
    bsz, nq, wq = q.shape
    nseg = len(ks)
    n_heads = wq // HEAD_PAD
    whole = lambda a: pl.BlockSpec((1,) + a.shape[1:], lambda b, i: (b,) + (0,) * (a.ndim - 1))
    vrows = vts[0].shape[2] // n_heads
    n_keys = sum(k.shape[1] for k in ks)
    n = sub * maps_per_head
    scratch = [pltpu.VMEM((n_keys, n), F32)] * 2 + [pltpu.VMEM((8, n), F32)] * 2 + [pltpu.VMEM((vrows, n), F32)] * 2
    qt_scratch = pltpu.VMEM((n_heads * tq // sub, HEAD_PAD, n), BF16)
    if maps_per_head > 1:
        scratch += [pltpu.VMEM((8, n), F32)] * 2 + [qt_scratch]
    else:
        scratch += [pltpu.VMEM((LANES, n), F32), qt_scratch]
    once = jnp.ones((1,), jnp.int32)
    return pl.pallas_call(
        functools.partial(kernel, nseg=nseg, sub=sub),
        grid=(bsz, nq // tq),
        in_specs=[pl.BlockSpec(memory_space=pltpu.SMEM)] + [_const_spec(e.shape) for e in extra]
        + [pl.BlockSpec((1, tq, wq), lambda b, i: (b, i, 0))] + [whole(k) for k in ks] + [whole(v) for v in vts],
        out_specs=pl.BlockSpec((1, tq, w_out), lambda b, i: (b, i, 0)),
        out_shape=jax.ShapeDtypeStruct((bsz, nq, w_out), BF16),
        scratch_shapes=scratch,
        compiler_params=_params(2),
        name=name,
    )(once, *extra, q, *ks, *vts)


def _axial_angles(n_tok, rot_dim):
    t = jnp.arange(n_tok)
    row = (t // GRID_W).astype(F32)
    col = (t % GRID_W).astype(F32)
    n_axis = rot_dim // 4
    inv = ROPE_BASE ** (-jnp.arange(n_axis, dtype=F32) / n_axis)
    return jnp.concatenate([row[:, None] * inv, col[:, None] * inv], axis=-1)


def _rope_tables(n_tok, n_ctx):
    am, ad = _axial_angles(n_tok, MLA_ROPE), _axial_angles(n_tok, DIFF_DIM)
    one = lambda w: jnp.ones((n_tok, w), F32)
    zero = lambda w: jnp.zeros((n_tok, w), F32)
    cm = jnp.concatenate([one(MLA_NOPE), jnp.cos(am), jnp.cos(am), one(HEAD_PAD - MLA_NOPE - MLA_ROPE)], axis=-1)
    sam = jnp.concatenate([zero(MLA_NOPE), -jnp.sin(am), zero(HEAD_PAD - MLA_NOPE - MLA_ROPE // 2)], axis=-1)
    sbm = jnp.concatenate([zero(MLA_NOPE + MLA_ROPE // 2), jnp.sin(am), zero(HEAD_PAD - MLA_NOPE - MLA_ROPE)], axis=-1)
    cd = jnp.tile(jnp.cos(ad), (1, 4))
    sad = jnp.tile(jnp.concatenate([-jnp.sin(ad), zero(DIFF_DIM // 2)], axis=-1), (1, 2))
    sbd = jnp.tile(jnp.concatenate([zero(DIFF_DIM // 2), jnp.sin(ad)], axis=-1), (1, 2))
    lat = (cm, sam, sbm, cd, sad, sbd)
    ident = (jnp.ones((n_ctx, LANES), F32),) + (jnp.zeros((n_ctx, LANES), F32),) * 2
    return lat, ident + ident


def _inv_counts(n):
    idx = jnp.arange(n)
    cols = []
    for k in POOL_WINDOWS:
        lo, hi = k // 2, k - 1 - k // 2
        cnt = (jnp.clip(idx + hi + 1, 0, n) - jnp.clip(idx - lo, 0, n)).astype(F32)
        cols.append(jnp.broadcast_to((1.0 / cnt)[:, None], (n, POOL_GROUP)))
    return jnp.concatenate(cols, axis=-1)


def _layer_weights(w_in, w_out, pool_w, mla_w_uq, mla_w_ukv):
    d = w_in.shape[0]
    s1, s2 = POOL_WIDTH, POOL_WIDTH + MLA_Q_RANK + MLA_KV_RANK + MLA_ROPE
    kr_pad = jnp.zeros((d, HEAD_PAD), F32).at[:, MLA_NOPE:MLA_NOPE + MLA_ROPE].set(w_in[:, s2 - MLA_ROPE:s2])
    s3 = s2 + 2 * DIFF_HEADS * 2 * DIFF_DIM
    wall = jnp.concatenate([w_in[:, :s2 - MLA_ROPE], kr_pad, w_in[:, s2:s3]], axis=-1).astype(BF16)
    wdvt = w_in[:, s3:].T.astype(BF16)
    qd = MLA_NOPE + MLA_ROPE
    wuq = jnp.pad(mla_w_uq.reshape(MLA_Q_RANK, MLA_HEADS, qd), ((0, 0), (0, 0), (0, HEAD_PAD - qd)))
    wuq = wuq.reshape(MLA_Q_RANK, MLA_HEADS * HEAD_PAD).astype(BF16)
    ukv = mla_w_ukv.reshape(MLA_KV_RANK, MLA_HEADS, MLA_NOPE + MLA_V)
    wk = jnp.pad(ukv[:, :, :MLA_NOPE], ((0, 0), (0, 0), (0, HEAD_PAD - MLA_NOPE)))
    wk = wk.reshape(MLA_KV_RANK, MLA_HEADS * HEAD_PAD).astype(BF16)
    wvt = ukv[:, :, MLA_NOPE:].reshape(MLA_KV_RANK, MLA_HEADS * MLA_V).T.astype(BF16)
    bd = jax.scipy.linalg.block_diag(*[pool_w[g] for g in range(len(POOL_WINDOWS))]).astype(BF16)
    return wall, wdvt, wuq, wk, wvt, bd, w_out.astype(BF16)


def kernel(x, c, ctx, c_ctx, w_mod, b_mod, ffn1_norm, ffn1_w1, ffn1_w3, ffn1_w2, mix_norm, w_in, w_out, pool_w, pool_scale, mla_q_norm, mla_w_uq, mla_kv_norm, mla_w_ukv, diff_lambda, diff_subln, ffn2_norm, ffn2_w1, ffn2_w3, ffn2_w2, final_norm):
    bsz, seq, d = x.shape
    n_ctx = ctx.shape[1]
    depth = w_mod.shape[0]
    assert bsz + 1 <= MOD_ROWS and seq % GRID_W == 0
    tm = 512
    ctx_row = bsz

    cc = jnp.zeros((MOD_ROWS, d), F32).at[:bsz].set(c).at[ctx_row].set(c_ctx)
    mods_all = _modulation(cc, w_mod, b_mod).reshape(depth, MOD_ROWS, N_MOD, d)

    tab_lat, tab_ctx = _rope_tables(seq, n_ctx)
    icg, icx = _inv_counts(GRID_W), _inv_counts(n_ctx)
    bf = lambda w: w.astype(BF16)
    mla_args = dict(w_out=MLA_HEADS * MLA_V, maps_per_head=1)
    dif_args = dict(w_out=DIFF_HEADS * DIFF_V, maps_per_head=2)

    h, hc = x, ctx
    for i in range(depth):
        last = i == depth - 1
        mods = mods_all[i]
        lam_init = 0.8 - 0.6 * math.exp(-0.3 * i)
        wall, wdvt, wuq, wk, wvt, bd, wo = _layer_weights(w_in[i], w_out[i], pool_w[i], mla_w_uq[i], mla_w_ukv[i])
        f1 = (ffn1_norm[i], bf(ffn1_w1[i]), bf(ffn1_w3[i]), bf(ffn1_w2[i]))
        f2 = (ffn2_norm[i], bf(ffn2_w1[i]), bf(ffn2_w3[i]), bf(ffn2_w2[i]))

        h = _ffn(h, mods, None, *f1, k0=0, tm=tm)
        hc = _ffn(hc, mods, ctx_row, *f1, k0=0, tm=n_ctx)

        pw = (mix_norm[i], wall, wdvt, mla_q_norm[i], mla_kv_norm[i], wuq, wk, wvt)
        u, qm, km, vm, qd, kd, vd = _proj(h, mods, None, *pw, tab_lat, tm=tm)
        uc, qmc, kmc, vmc, qdc, kdc, vdc = _proj(hc, mods, ctx_row, *pw, tab_ctx, tm=n_ctx)

        a, ac = _pool(u, uc, icg, icx, bd, pool_scale[i])
        dif = functools.partial(_diff_attn_kernel, lam_init=lam_init)
        dextra = (diff_lambda[i], diff_subln[i].reshape(1, DIFF_V))
        b = _attention(_mla_attn_kernel, "mla_attn", qm, (km, kmc), (vm, vmc), (),
                       tq=2048, sub=512, **mla_args)
        cdiff = _attention(dif, "diff_attn", qd, (kd, kdc), (vd, vdc), dextra,
                           tq=1024, sub=256, **dif_args)
        h = _ffn(h, mods, None, *f2, k0=6, tm=tm, mix=(a, b, cdiff, wo),
                 final_g=final_norm if last else None)
        if not last:
            bc = _attention(_mla_attn_kernel, "mla_attn_ctx", qmc, (kmc,), (vmc,), (),
                            tq=n_ctx, sub=n_ctx, **mla_args)
            cc_ = _attention(dif, "diff_attn_ctx", qdc, (kdc,), (vdc,), dextra,
                             tq=n_ctx, sub=n_ctx, **dif_args)
            hc = _ffn(hc, mods, ctx_row, *f2, k0=6, tm=n_ctx, mix=(ac, bc, cc_, wo))
    return h
```

```python
import functools
import math

import jax
import jax.numpy as jnp
from jax import lax
from jax.experimental import pallas as pl
from jax.experimental.pallas import tpu as pltpu

F32 = jnp.float32
BF16 = jnp.bfloat16

GRID_W = 64
N_MOD = 9
POOL_WINDOWS = (2, 4, 8, 16)
POOL_GROUP = 64
POOL_WIDTH = POOL_GROUP * len(POOL_WINDOWS)
POOL_HALF = max(POOL_WINDOWS) // 2
MLA_HEADS = 4
MLA_NOPE = 64
MLA_ROPE = 32
MLA_V = 64
MLA_Q_RANK = 384
MLA_KV_RANK = 256
DIFF_HEADS = 4
DIFF_DIM = 64
DIFF_V = 2 * DIFF_DIM
ROPE_BASE = 10000.0
EPS = 1e-6

LANES = 128
HEAD_PAD = LANES
VT_CHUNK = 256
SUM_ROWS = 16
MLA_VT_ROWS = MLA_V + SUM_ROWS
MOD_ROWS = 24
VMEM_LIMIT = 56 * 1024 * 1024

C_POOL = 0
C_CQ = C_POOL + POOL_WIDTH
C_CKV = C_CQ + MLA_Q_RANK
C_KR = C_CKV + MLA_KV_RANK
C_DQ = C_KR + HEAD_PAD
C_DK = C_DQ + DIFF_HEADS * 2 * DIFF_DIM
C_DV = C_DK + DIFF_HEADS * 2 * DIFF_DIM

LOG2E = math.log2(math.e)
MLA_QSCALE = (MLA_NOPE + MLA_ROPE) ** -0.5 * LOG2E
DIFF_QSCALE = DIFF_DIM ** -0.5 * LOG2E


def _dot(a, b):
    return jnp.dot(a, b, preferred_element_type=F32)


def _dot_nt(a, b):
    return lax.dot_general(a, b, (((1,), (1,)), ((), ())), preferred_element_type=F32)


def _rms(x, g):
    return x * lax.rsqrt(jnp.mean(x * x, axis=-1, keepdims=True) + EPS) * g


def _silu(a):
    return a * (1.0 / (1.0 + jnp.exp(-a)))


def _const_spec(shape):
    return pl.BlockSpec(shape, lambda *_: (0,) * len(shape), pipeline_mode=pl.Buffered(1))


def _params(n_axes):
    return pltpu.CompilerParams(dimension_semantics=("parallel",) * n_axes, vmem_limit_bytes=VMEM_LIMIT)


def _mod_kernel(cc_ref, w_ref, b_ref, o_ref):
    sc = _silu(cc_ref[...])
    o_ref[0] = jnp.dot(sc, w_ref[0], preferred_element_type=F32, precision=lax.Precision.HIGHEST) + b_ref[0]


def _modulation(cc, w_mod, b_mod):
    depth, d, width = w_mod.shape
    bn = 9 * LANES
    return pl.pallas_call(
        _mod_kernel,
        grid=(depth, width // bn),
        in_specs=[
            pl.BlockSpec((MOD_ROWS, d), lambda l, j: (0, 0)),
            pl.BlockSpec((1, d, bn), lambda l, j: (l, 0, j)),
            pl.BlockSpec((1, 1, bn), lambda l, j: (l, 0, j)),
        ],
        out_specs=pl.BlockSpec((1, MOD_ROWS, bn), lambda l, j: (l, 0, j)),
        out_shape=jax.ShapeDtypeStruct((depth, MOD_ROWS, width), F32),
        compiler_params=_params(2),
        name="modulation",
    )(cc, w_mod, b_mod.reshape(depth, 1, width))


MXU_COLS = 256
FFN_CHUNK = MXU_COLS


def _ffn_kernel(*refs, pre, final, k0):
    it = iter(refs)
    x_ref, m_ref = next(it), next(it)
    if pre:
        a_ref, b_ref, c_ref, wo_ref = next(it), next(it), next(it), next(it)
    g_ref, w1_ref, w3_ref, w2_ref = next(it), next(it), next(it), next(it)
    if final:
        fg_ref = next(it)
    o_ref, hmid_ref = next(it), next(it)

    x = x_ref[0]
    if pre:
        mix = jnp.concatenate([a_ref[0], b_ref[0], c_ref[0]], axis=-1)
        x = x + m_ref[0, 5:6, :] * _dot(mix, wo_ref[...])
    xm = _rms(x, g_ref[...]) * (1.0 + m_ref[0, k0 + 1:k0 + 2, :]) + m_ref[0, k0:k0 + 1, :]
    xb = xm.astype(BF16)
    dff = w1_ref.shape[1]
    for lo in range(0, dff, FFN_CHUNK):
        cols = slice(lo, min(lo + FFN_CHUNK, dff))
        a = _dot(xb, w1_ref[:, cols])
        b = _dot(xb, w3_ref[:, cols])
        hmid_ref[:, cols] = (_silu(a) * b).astype(BF16)
    y = _dot(hmid_ref[...], w2_ref[...])
    out = x + (0.5 * m_ref[0, k0 + 2:k0 + 3, :]) * y
    if final:
        out = _rms(out, fg_ref[...])
    o_ref[0] = out


def _ffn(x, mods, mod_row, g, w1, w3, w2, *, k0, tm, mix=None, final_g=None):
    bsz, n, d = x.shape
    dff = w1.shape[1]
    pre, final = mix is not None, final_g is not None
    tok = lambda w: pl.BlockSpec((1, tm, w), lambda b, i: (b, i, 0))
    mod_spec = pl.BlockSpec((1, N_MOD, d), (lambda b, i: (b, 0, 0)) if mod_row is None else (lambda b, i: (mod_row, 0, 0)))
    args, specs = [x, mods], [tok(d), mod_spec]
    if pre:
        a, bb, c, wo = mix
        args += [a, bb, c, wo]
        specs += [tok(a.shape[-1]), tok(bb.shape[-1]), tok(c.shape[-1]), _const_spec(wo.shape)]
    args += [g.reshape(1, d), w1, w3, w2]
    specs += [_const_spec((1, d)), _const_spec(w1.shape), _const_spec(w3.shape), _const_spec(w2.shape)]
    if final:
        args.append(final_g.reshape(1, d))
        specs.append(_const_spec((1, d)))
    return pl.pallas_call(
        functools.partial(_ffn_kernel, pre=pre, final=final, k0=k0),
        grid=(bsz, n // tm),
        in_specs=specs,
        out_specs=tok(d),
        out_shape=jax.ShapeDtypeStruct(x.shape, F32),
        scratch_shapes=[pltpu.VMEM((tm, dff), BF16)],
        compiler_params=_params(2),
        name="ffn_mix" if pre else "ffn",
    )(*args)


def _rope(x, c, sa, sb, shift):
    return x * c + pltpu.roll(x, LANES - shift, 1) * sa + pltpu.roll(x, shift, 1) * sb


def _proj_kernel(x_ref, m_ref, g_ref, wall_ref, wdvt_ref, qn_ref, kvn_ref, wuq_ref, wk_ref, wvt_ref,
                 cm_ref, sam_ref, sbm_ref, cd_ref, sad_ref, sbd_ref,
                 u_ref, qm_ref, km_ref, vmt_ref, qd_ref, kd_ref, vdt_ref):
    x = x_ref[0]
    n = _rms(x, g_ref[...]) * (1.0 + m_ref[0, 4:5, :]) + m_ref[0, 3:4, :]
    nb = n.astype(BF16)
    proj = lambda lo, hi: _dot(nb, wall_ref[:, lo:hi])

    pm = proj(C_CQ, C_DQ)
    cqn = _rms(pm[:, :C_CKV - C_CQ], qn_ref[...]).astype(BF16)
    ckvn = _rms(pm[:, C_CKV - C_CQ:C_KR - C_CQ], kvn_ref[...]).astype(BF16)
    q = _dot(cqn, wuq_ref[...])
    k = _dot(ckvn, wk_ref[...])
    vt = _dot_nt(wvt_ref[...], ckvn).astype(BF16)
    ones = jnp.ones((SUM_ROWS, VT_CHUNK), BF16)
    for j in range(vmt_ref.shape[1]):
        cols = slice(j * VT_CHUNK, (j + 1) * VT_CHUNK)
        for h in range(MLA_HEADS):
            vmt_ref[0, j, h * MLA_VT_ROWS:h * MLA_VT_ROWS + MLA_V, :] = vt[h * MLA_V:(h + 1) * MLA_V, cols]
            vmt_ref[0, j, h * MLA_VT_ROWS + MLA_V:(h + 1) * MLA_VT_ROWS, :] = ones
    cm, sam, sbm = cm_ref[...], sam_ref[...], sbm_ref[...]
    half_m = MLA_ROPE // 2
    kr = _rope(pm[:, C_KR - C_CQ:], cm, sam, sbm, half_m)
    for h in range(MLA_HEADS):
        sl = slice(h * HEAD_PAD, (h + 1) * HEAD_PAD)
        qm_ref[0, :, sl] = (_rope(q[:, sl], cm, sam, sbm, half_m) * MLA_QSCALE).astype(BF16)
        km_ref[0, :, sl] = (k[:, sl] + kr).astype(BF16)

    cd, sad, sbd = cd_ref[...], sad_ref[...], sbd_ref[...]
    half_d = DIFF_DIM // 2
    for col0, out_ref, scale in ((C_DQ, qd_ref, DIFF_QSCALE), (C_DK, kd_ref, None)):
        for h0 in range(0, DIFF_HEADS, 2):
            pp = proj(col0 + h0 * HEAD_PAD, col0 + (h0 + 2) * HEAD_PAD)
            for h in (h0, h0 + 1):
                r = _rope(pp[:, (h - h0) * HEAD_PAD:(h - h0 + 1) * HEAD_PAD], cd, sad, sbd, half_d)
                out_ref[0, :, h * HEAD_PAD:(h + 1) * HEAD_PAD] = (r if scale is None else r * scale).astype(BF16)
    u_ref[0] = proj(C_POOL, C_CQ)
    vdt = _dot_nt(wdvt_ref[...], nb).astype(BF16)
    for j in range(vdt_ref.shape[1]):
        vdt_ref[0, j] = vdt[:, j * VT_CHUNK:(j + 1) * VT_CHUNK]


def _proj(x, mods, mod_row, g, wall, wdvt, qn, kvn, wuq, wk, wvt, tables, *, tm):
    bsz, n, d = x.shape
    tok = lambda w: pl.BlockSpec((1, tm, w), lambda b, i: (b, i, 0))
    vt = lambda r: pl.BlockSpec((1, tm // VT_CHUNK, r, VT_CHUNK), lambda b, i: (b, i, 0, 0))
    mod_spec = pl.BlockSpec((1, N_MOD, d), (lambda b, i: (b, 0, 0)) if mod_row is None else (lambda b, i: (mod_row, 0, 0)))
    tab = pl.BlockSpec((tm, LANES), lambda b, i: (i, 0))
    wm, wd = MLA_HEADS * HEAD_PAD, DIFF_HEADS * HEAD_PAD
    rm, rd = MLA_HEADS * MLA_VT_ROWS, DIFF_HEADS * DIFF_V
    tok_sds = lambda w, dt: jax.ShapeDtypeStruct((bsz, n, w), dt)
    vt_sds = lambda r: jax.ShapeDtypeStruct((bsz, n // VT_CHUNK, r, VT_CHUNK), BF16)
    return pl.pallas_call(
        _proj_kernel,
        grid=(bsz, n // tm),
        in_specs=[tok(d), mod_spec, _const_spec((1, d)), _const_spec(wall.shape), _const_spec(wdvt.shape),
                  _const_spec((1, MLA_Q_RANK)), _const_spec((1, MLA_KV_RANK)),
                  _const_spec(wuq.shape), _const_spec(wk.shape), _const_spec(wvt.shape)] + [tab] * 6,
        out_specs=[tok(POOL_WIDTH), tok(wm), tok(wm), vt(rm), tok(wd), tok(wd), vt(rd)],
        out_shape=[tok_sds(POOL_WIDTH, F32), tok_sds(wm, BF16), tok_sds(wm, BF16), vt_sds(rm),
                   tok_sds(wd, BF16), tok_sds(wd, BF16), vt_sds(rd)],
        compiler_params=_params(2),
        name="mix_in",
    )(x, mods, g.reshape(1, d), wall, wdvt, qn.reshape(1, -1), kvn.reshape(1, -1), wuq, wk, wvt, *tables)


def _nested_window_sums(load, lane):
    a2 = load(-1) + load(0)
    a4 = a2 + load(-2) + load(1)
    a8 = a4 + load(-4) + load(-3) + load(2) + load(3)
    a16 = a8 + load(-8) + load(-7) + load(-6) + load(-5) + load(4) + load(5) + load(6) + load(7)
    return jnp.where(lane < POOL_GROUP, a2, jnp.where(lane < 2 * POOL_GROUP, a4, jnp.where(lane < 3 * POOL_GROUP, a8, a16)))


def _window_sums_1d(slab, n, lane):
    size = slab.shape[0]
    ahead = lambda x, k: pltpu.roll(x, size - k, 0)
    behind = lambda x, k: pltpu.roll(x, k, 0)[POOL_HALF:POOL_HALF + n]
    p2 = slab + ahead(slab, 1)
    p4 = p2 + ahead(p2, 2)
    p8 = p4 + ahead(p4, 4)
    p16 = p8 + ahead(p8, 8)
    return jnp.where(lane < POOL_GROUP, behind(p2, 1),
                     jnp.where(lane < 2 * POOL_GROUP, behind(p4, 2),
                               jnp.where(lane < 3 * POOL_GROUP, behind(p8, 4), p16[:n])))


def _pool_kernel(ul_ref, uc_ref, icg_ref, icx_ref, bd_ref, ps_ref, al_ref, ac_ref,
                 z_ref, y_ref, dl_ref, zc_ref, *, rows):
    gw, pad = GRID_W, POOL_HALF
    stride = gw + 2 * pad
    n_ctx = uc_ref.shape[1]
    lane = lax.broadcasted_iota(jnp.int32, (gw, POOL_WIDTH), 1)
    zero_rows = jnp.zeros((pad * gw, POOL_WIDTH), F32)
    zero_pad = jnp.zeros((pad, POOL_WIDTH), F32)

    z_ref[0:pad * gw, :] = zero_rows
    z_ref[pad * gw:(rows + pad) * gw, :] = ul_ref[0]
    z_ref[(rows + pad) * gw:(rows + 2 * pad) * gw, :] = zero_rows

    def row_pass(r, carry):
        base = pl.multiple_of((r + pad) * gw, 8)
        s = _nested_window_sums(lambda d: z_ref[pl.ds(base + d * gw, gw), :], lane)
        yb = pl.multiple_of(r * stride, 8)
        y_ref[pl.ds(yb, pad), :] = zero_pad
        y_ref[pl.ds(yb + pad, gw), :] = s * icg_ref[pl.ds(r, 1), :]
        y_ref[pl.ds(yb + pad + gw, pad), :] = zero_pad
        return carry

    lax.fori_loop(0, rows, row_pass, 0)

    def col_pass(r, carry):
        slab = y_ref[pl.ds(pl.multiple_of(r * stride, 8), stride), :]
        s = _window_sums_1d(slab, gw, lane)
        tok = pl.ds(pl.multiple_of(r * gw, 8), gw)
        dl_ref[tok, :] = (s * icg_ref[...] - ul_ref[0, tok, :]).astype(BF16)
        return carry

    lax.fori_loop(0, rows, col_pass, 0)
    al_ref[0] = (_dot(dl_ref[...], bd_ref[...]) * ps_ref[...]).astype(BF16)

    uc = uc_ref[0]
    zc_ref[0:pad, :] = zero_pad
    zc_ref[pad:pad + n_ctx, :] = uc
    zc_ref[pad + n_ctx:pad + n_ctx + pad, :] = zero_pad
    lane_c = lax.broadcasted_iota(jnp.int32, (n_ctx, POOL_WIDTH), 1)
    sc = _window_sums_1d(zc_ref[...], n_ctx, lane_c)
    dc = (sc * icx_ref[...] - uc).astype(BF16)
    ac_ref[0] = (_dot(dc, bd_ref[...]) * ps_ref[...]).astype(BF16)


def _pool(u_lat, u_ctx, icg, icx, bd, ps):
    bsz, s, w = u_lat.shape
    n_ctx = u_ctx.shape[1]
    rows = s // GRID_W
    stride = GRID_W + 2 * POOL_HALF
    return pl.pallas_call(
        functools.partial(_pool_kernel, rows=rows),
        grid=(bsz,),
        in_specs=[pl.BlockSpec((1, s, w), lambda b: (b, 0, 0)), pl.BlockSpec((1, n_ctx, w), lambda b: (b, 0, 0)),
                  _const_spec(icg.shape), _const_spec(icx.shape), _const_spec(bd.shape), _const_spec((1, w))],
        out_specs=[pl.BlockSpec((1, s, w), lambda b: (b, 0, 0)), pl.BlockSpec((1, n_ctx, w), lambda b: (b, 0, 0))],
        out_shape=[jax.ShapeDtypeStruct((bsz, s, w), BF16), jax.ShapeDtypeStruct((bsz, n_ctx, w), BF16)],
        scratch_shapes=[pltpu.VMEM(((rows + 2 * POOL_HALF) * GRID_W, w), F32), pltpu.VMEM((rows * stride, w), F32),
                        pltpu.VMEM((s, w), BF16), pltpu.VMEM((n_ctx + 2 * POOL_HALF, w), F32)],
        compiler_params=_params(1),
        name="pool",
    )(u_lat, u_ctx, icg, icx, bd, ps.reshape(1, w))


def _attn_units(once_ref, heads, vrows, prepare, qt_of, k_refs, vt_refs, s_refs, mx_refs, l_refs, acc_refs, finish):
    n_units = len(heads)
    n = s_refs[0].shape[1]
    offs, off = [], 0
    for k_ref in k_refs:
        offs.append(off)
        off += k_ref.shape[1]

    def loop_body(fns):
        def whole(t, carry):
            for k_ref, vt_ref, o in zip(k_refs, vt_refs, offs):
                n_chunks, kc = vt_ref.shape[1], vt_ref.shape[3]
                for c in range(n_chunks):
                    for fn in fns["chunk"]:
                        fn(k_ref, vt_ref, c, slice(c * kc, (c + 1) * kc), slice(o + c * kc, o + (c + 1) * kc))
            for fn in fns["once"]:
                fn()
            return carry

        lax.fori_loop(0, once_ref[0], whole, 0)

    def scores(u):
        slot = u % 2
        ksl = slice(heads[u] * HEAD_PAD, (heads[u] + 1) * HEAD_PAD)

        def fn(k_ref, vt_ref, c, krows, srows):
            st = _dot(k_ref[0, krows, ksl], qt_of(u))
            s_refs[slot][srows, :] = st
            mx_refs[slot][...] = jnp.maximum(mx_refs[slot][...], jnp.max(st.reshape(-1, 8, n), axis=0))

        return fn

    def probs(u, m):
        slot = u % 2
        vsl = slice(heads[u] * vrows, (heads[u] + 1) * vrows)

        def fn(k_ref, vt_ref, c, krows, srows):
            p = jnp.exp2(s_refs[slot][srows, :] - m)
            if l_refs is not None:
                l_refs[slot][...] += jnp.sum(p.reshape(-1, 8, n), axis=0)
            acc_refs[slot][...] += _dot(vt_ref[0, c, vsl, :], p.astype(BF16))

        return fn

    def epilogue(u):
        slot = u % 2
        return lambda: finish(u, acc_refs[slot][...],
                              None if l_refs is None else jnp.sum(l_refs[slot][...], axis=0, keepdims=True))

    neg_inf = jnp.full(mx_refs[0].shape, -jnp.inf, F32)
    mx_refs[0][...] = neg_inf
    prepare(0)
    loop_body({"chunk": [scores(0)], "once": [lambda: prepare(1)] if n_units > 1 else []})
    for u in range(n_units):
        slot = u % 2
        m = jnp.max(mx_refs[slot][...], axis=0, keepdims=True)
        if l_refs is not None:
            l_refs[slot][...] = jnp.zeros(l_refs[slot].shape, F32)
        acc_refs[slot][...] = jnp.zeros(acc_refs[slot].shape, F32)
        fns = {"chunk": [probs(u, m)], "once": [epilogue(u - 1)] if u > 0 else []}
        if u + 2 < n_units:
            fns["once"].append(functools.partial(prepare, u + 2))
        if u + 1 < n_units:
            mx_refs[1 - slot][...] = neg_inf
            fns["chunk"].insert(0, scores(u + 1))
        loop_body(fns)
    epilogue(n_units - 1)()


def _mla_attn_kernel(once_ref, *refs, nseg, sub):
    q_ref, k_refs, vt_refs = refs[0], refs[1:1 + nseg], refs[1 + nseg:1 + 2 * nseg]
    o_ref, s0_ref, s1_ref, mx0_ref, mx1_ref, acc0_ref, acc1_ref, ot_ref, qt_ref = refs[1 + 2 * nseg:]
    units = [(sb, h) for sb in range(q_ref.shape[1] // sub) for h in range(MLA_HEADS)]

    def prepare(u):
        sb, h = units[u]
        qh = q_ref[0, sb * sub:(sb + 1) * sub, h * HEAD_PAD:(h + 1) * HEAD_PAD]
        qt_ref[u] = qh.astype(F32).T.astype(BF16)

    def finish(u, acc, l):
        sb, h = units[u]
        ot_ref[(h % 2) * MLA_V:(h % 2 + 1) * MLA_V, :] = acc[:MLA_V] * (1.0 / acc[MLA_V:MLA_V + 1])
        if h % 2 == 1:
            o_ref[0, sb * sub:(sb + 1) * sub, (h - 1) * MLA_V:(h + 1) * MLA_V] = ot_ref[...].T.astype(BF16)

    _attn_units(once_ref, [h for _, h in units], MLA_VT_ROWS, prepare, lambda u: qt_ref[u], k_refs, vt_refs,
                (s0_ref, s1_ref), (mx0_ref, mx1_ref), None, (acc0_ref, acc1_ref), finish)


def _diff_attn_kernel(once_ref, *refs, nseg, sub, lam_init):
    dl_ref, sub_ref, q_ref = refs[0], refs[1], refs[2]
    k_refs, vt_refs = refs[3:3 + nseg], refs[3 + nseg:3 + 2 * nseg]
    o_ref, s0_ref, s1_ref, mx0_ref, mx1_ref, acc0_ref, acc1_ref, l0_ref, l1_ref, qu_ref = refs[3 + 2 * nseg:]
    units = [(sb, h) for sb in range(q_ref.shape[1] // sub) for h in range(DIFF_HEADS)]
    dl = dl_ref[...]
    lam = (jnp.exp(jnp.sum(dl[0:1] * dl[1:2], axis=-1, keepdims=True))
           - jnp.exp(jnp.sum(dl[2:3] * dl[3:4], axis=-1, keepdims=True)) + lam_init)
    dim = lax.broadcasted_iota(jnp.int32, (HEAD_PAD, sub), 0)
    def prepare(u):
        sb, h = units[u]
        qt = q_ref[0, sb * sub:(sb + 1) * sub, h * HEAD_PAD:(h + 1) * HEAD_PAD].astype(F32).T
        qu_ref[u, :, :sub] = jnp.where(dim < DIFF_DIM, qt, 0.0).astype(BF16)
        qu_ref[u, :, sub:] = jnp.where(dim >= DIFF_DIM, qt, 0.0).astype(BF16)

    def finish(u, acc, l):
        sb, h = units[u]
        o = acc[:, :sub] * (1.0 / l[:, :sub]) - acc[:, sub:] * (lam / l[:, sub:])
        on = o * lax.rsqrt(jnp.mean(o * o, axis=0, keepdims=True) + EPS)
        o_ref[0, sb * sub:(sb + 1) * sub, h * DIFF_V:(h + 1) * DIFF_V] = (
            on.T * sub_ref[...] * (1.0 - lam_init)).astype(BF16)

    _attn_units(once_ref, [h for _, h in units], DIFF_V, prepare, lambda u: qu_ref[u], k_refs, vt_refs,
                (s0_ref, s1_ref), (mx0_ref, mx1_ref), (l0_ref, l1_ref), (acc0_ref, acc1_ref), finish)


def _attention(kernel, name, q, ks, vts, extra, *, tq, sub, w_out, maps_per_head):
    bsz, nq, wq = q.shape
    nseg = len(ks)
    n_heads = wq // HEAD_PAD
    whole = lambda a: pl.BlockSpec((1,) + a.shape[1:], lambda b, i: (b,) + (0,) * (a.ndim - 1))
    vrows = vts[0].shape[2] // n_heads
    n_keys = sum(k.shape[1] for k in ks)
    n = sub * maps_per_head
    scratch = [pltpu.VMEM((n_keys, n), F32)] * 2 + [pltpu.VMEM((8, n), F32)] * 2 + [pltpu.VMEM((vrows, n), F32)] * 2
    qt_scratch = pltpu.VMEM((n_heads * tq // sub, HEAD_PAD, n), BF16)
    if maps_per_head > 1:
        scratch += [pltpu.VMEM((8, n), F32)] * 2 + [qt_scratch]
    else:
        scratch += [pltpu.VMEM((LANES, n), F32), qt_scratch]
    once = jnp.ones((1,), jnp.int32)
    return pl.pallas_call(
        functools.partial(kernel, nseg=nseg, sub=sub),
        grid=(bsz, nq // tq),
        in_specs=[pl.BlockSpec(memory_space=pltpu.SMEM)] + [_const_spec(e.shape) for e in extra]
        + [pl.BlockSpec((1, tq, wq), lambda b, i: (b, i, 0))] + [whole(k) for k in ks] + [whole(v) for v in vts],
        out_specs=pl.BlockSpec((1, tq, w_out), lambda b, i: (b, i, 0)),
        out_shape=jax.ShapeDtypeStruct((bsz, nq, w_out), BF16),
        scratch_shapes=scratch,
        compiler_params=_params(2),
        name=name,
    )(once, *extra, q, *ks, *vts)


def _axial_angles(n_tok, rot_dim):
    t = jnp.arange(n_tok)
    row = (t // GRID_W).astype(F32)
    col = (t % GRID_W).astype(F32)
    n_axis = rot_dim // 4
    inv = ROPE_BASE ** (-jnp.arange(n_axis, dtype=F32) / n_axis)
    return jnp.concatenate([row[:, None] * inv, col[:, None] * inv], axis=-1)


def _rope_tables(n_tok, n_ctx):
    am, ad = _axial_angles(n_tok, MLA_ROPE), _axial_angles(n_tok, DIFF_DIM)
    one = lambda w: jnp.ones((n_tok, w), F32)
    zero = lambda w: jnp.zeros((n_tok, w), F32)
    cm = jnp.concatenate([one(MLA_NOPE), jnp.cos(am), jnp.cos(am), one(HEAD_PAD - MLA_NOPE - MLA_ROPE)], axis=-1)
    sam = jnp.concatenate([zero(MLA_NOPE), -jnp.sin(am), zero(HEAD_PAD - MLA_NOPE - MLA_ROPE // 2)], axis=-1)
    sbm = jnp.concatenate([zero(MLA_NOPE + MLA_ROPE // 2), jnp.sin(am), zero(HEAD_PAD - MLA_NOPE - MLA_ROPE)], axis=-1)
    cd = jnp.tile(jnp.cos(ad), (1, 4))
    sad = jnp.tile(jnp.concatenate([-jnp.sin(ad), zero(DIFF_DIM // 2)], axis=-1), (1, 2))
    sbd = jnp.tile(jnp.concatenate([zero(DIFF_DIM // 2), jnp.sin(ad)], axis=-1), (1, 2))
    lat = (cm, sam, sbm, cd, sad, sbd)
    ident = (jnp.ones((n_ctx, LANES), F32),) + (jnp.zeros((n_ctx, LANES), F32),) * 2
    return lat, ident + ident


def _inv_counts(n):
    idx = jnp.arange(n)
    cols = []
    for k in POOL_WINDOWS:
        lo, hi = k // 2, k - 1 - k // 2
        cnt = (jnp.clip(idx + hi + 1, 0, n) - jnp.clip(idx - lo, 0, n)).astype(F32)
        cols.append(jnp.broadcast_to((1.0 / cnt)[:, None], (n, POOL_GROUP)))
    return jnp.concatenate(cols, axis=-1)


def _layer_weights(w_in, w_out, pool_w, mla_w_uq, mla_w_ukv):
    d = w_in.shape[0]
    s1, s2 = POOL_WIDTH, POOL_WIDTH + MLA_Q_RANK + MLA_KV_RANK + MLA_ROPE
    kr_pad = jnp.zeros((d, HEAD_PAD), F32).at[:, MLA_NOPE:MLA_NOPE + MLA_ROPE].set(w_in[:, s2 - MLA_ROPE:s2])
    s3 = s2 + 2 * DIFF_HEADS * 2 * DIFF_DIM
    wall = jnp.concatenate([w_in[:, :s2 - MLA_ROPE], kr_pad, w_in[:, s2:s3]], axis=-1).astype(BF16)
    wdvt = w_in[:, s3:].T.astype(BF16)
    qd = MLA_NOPE + MLA_ROPE
    wuq = jnp.pad(mla_w_uq.reshape(MLA_Q_RANK, MLA_HEADS, qd), ((0, 0), (0, 0), (0, HEAD_PAD - qd)))
    wuq = wuq.reshape(MLA_Q_RANK, MLA_HEADS * HEAD_PAD).astype(BF16)
    ukv = mla_w_ukv.reshape(MLA_KV_RANK, MLA_HEADS, MLA_NOPE + MLA_V)
    wk = jnp.pad(ukv[:, :, :MLA_NOPE], ((0, 0), (0, 0), (0, HEAD_PAD - MLA_NOPE)))
    wk = wk.reshape(MLA_KV_RANK, MLA_HEADS * HEAD_PAD).astype(BF16)
    wvt = ukv[:, :, MLA_NOPE:].reshape(MLA_KV_RANK, MLA_HEADS * MLA_V).T.astype(BF16)
    bd = jax.scipy.linalg.block_diag(*[pool_w[g] for g in range(len(POOL_WINDOWS))]).astype(BF16)
    return wall, wdvt, wuq, wk, wvt, bd, w_out.astype(BF16)


def kernel(x, c, ctx, c_ctx, w_mod, b_mod, ffn1_norm, ffn1_w1, ffn1_w3, ffn1_w2, mix_norm, w_in, w_out, pool_w, pool_scale, mla_q_norm, mla_w_uq, mla_kv_norm, mla_w_ukv, diff_lambda, diff_subln, ffn2_norm, ffn2_w1, ffn2_w3, ffn2_w2, final_norm):
    bsz, seq, d = x.shape
    n_ctx = ctx.shape[1]
    depth = w_mod.shape[0]
    assert bsz + 1 <= MOD_ROWS and seq % GRID_W == 0
    tm = 512
    ctx_row = bsz

    cc = jnp.zeros((MOD_ROWS, d), F32).at[:bsz].set(c).at[ctx_row].set(c_ctx)
    mods_all = _modulation(cc, w_mod, b_mod).reshape(depth, MOD_ROWS, N_MOD, d)

    tab_lat, tab_ctx = _rope_tables(seq, n_ctx)
    icg, icx = _inv_counts(GRID_W), _inv_counts(n_ctx)
    bf = lambda w: w.astype(BF16)
    mla_args = dict(w_out=MLA_HEADS * MLA_V, maps_per_head=1)
    dif_args = dict(w_out=DIFF_HEADS * DIFF_V, maps_per_head=2)

    h, hc = x, ctx
    for i in range(depth):
        last = i == depth - 1
        mods = mods_all[i]
        lam_init = 0.8 - 0.6 * math.exp(-0.3 * i)
        wall, wdvt, wuq, wk, wvt, bd, wo = _layer_weights(w_in[i], w_out[i], pool_w[i], mla_w_uq[i], mla_w_ukv[i])
        f1 = (ffn1_norm[i], bf(ffn1_w1[i]), bf(ffn1_w3[i]), bf(ffn1_w2[i]))
        f2 = (ffn2_norm[i], bf(ffn2_w1[i]), bf(ffn2_w3[i]), bf(ffn2_w2[i]))

        h = _ffn(h, mods, None, *f1, k0=0, tm=tm)
        hc = _ffn(hc, mods, ctx_row, *f1, k0=0, tm=n_ctx)

        pw = (mix_norm[i], wall, wdvt, mla_q_norm[i], mla_kv_norm[i], wuq, wk, wvt)
        u, qm, km, vm, qd, kd, vd = _proj(h, mods, None, *pw, tab_lat, tm=tm)
        uc, qmc, kmc, vmc, qdc, kdc, vdc = _proj(hc, mods, ctx_row, *pw, tab_ctx, tm=n_ctx)

        a, ac = _pool(u, uc, icg, icx, bd, pool_scale[i])
        dif = functools.partial(_diff_attn_kernel, lam_init=lam_init)
        dextra = (diff_lambda[i], diff_subln[i].reshape(1, DIFF_V))
        b = _attention(_mla_attn_kernel, "mla_attn", qm, (km, kmc), (vm, vmc), (),
                       tq=2048, sub=512, **mla_args)
        cdiff = _attention(dif, "diff_attn", qd, (kd, kdc), (vd, vdc), dextra,
                           tq=1024, sub=256, **dif_args)
        h = _ffn(h, mods, None, *f2, k0=6, tm=tm, mix=(a, b, cdiff, wo),
                 final_g=final_norm if last else None)
        if not last:
            bc = _attention(_mla_attn_kernel, "mla_attn_ctx", qmc, (kmc,), (vmc,), (),
                            tq=n_ctx, sub=n_ctx, **mla_args)
            cc_ = _attention(dif, "diff_attn_ctx", qdc, (kdc,), (vdc,), dextra,
                             tq=n_ctx, sub=n_ctx, **dif_args)
            hc = _ffn(hc, mods, ctx_row, *f2, k0=6, tm=n_ctx, mix=(ac, bc, cc_, wo))
    return h
```

```python
import functools
import math

import jax
import jax.numpy as jnp
from jax import lax
from jax.experimental import pallas as pl
from jax.experimental.pallas import tpu as pltpu

F32 = jnp.float32
BF16 = jnp.bfloat16

GRID_W = 64
N_MOD = 9
POOL_WINDOWS = (2, 4, 8, 16)
POOL_GROUP = 64
POOL_WIDTH = POOL_GROUP * len(POOL_WINDOWS)
POOL_HALF = max(POOL_WINDOWS) // 2
MLA_HEADS = 4
MLA_NOPE = 64
MLA_ROPE = 32
MLA_V = 64
MLA_Q_RANK = 384
MLA_KV_RANK = 256
DIFF_HEADS = 4
DIFF_DIM = 64
DIFF_V = 2 * DIFF_DIM
ROPE_BASE = 10000.0
EPS = 1e-6

LANES = 128
HEAD_PAD = LANES
VT_CHUNK = 256
SUM_ROWS = 16
MLA_VT_ROWS = MLA_V + SUM_ROWS
MOD_ROWS = 24
VMEM_LIMIT = 56 * 1024 * 1024

C_POOL = 0
C_CQ = C_POOL + POOL_WIDTH
C_CKV = C_CQ + MLA_Q_RANK
C_KR = C_CKV + MLA_KV_RANK
C_DQ = C_KR + HEAD_PAD
C_DK = C_DQ + DIFF_HEADS * 2 * DIFF_DIM
C_DV = C_DK + DIFF_HEADS * 2 * DIFF_DIM

LOG2E = math.log2(math.e)
MLA_QSCALE = (MLA_NOPE + MLA_ROPE) ** -0.5 * LOG2E
DIFF_QSCALE = DIFF_DIM ** -0.5 * LOG2E


def _dot(a, b):
    return jnp.dot(a, b, preferred_element_type=F32)


def _dot_nt(a, b):
    return lax.dot_general(a, b, (((1,), (1,)), ((), ())), preferred_element_type=F32)


def _rms(x, g):
    return x * lax.rsqrt(jnp.mean(x * x, axis=-1, keepdims=True) + EPS) * g


def _silu(a):
    return a * (1.0 / (1.0 + jnp.exp(-a)))


def _const_spec(shape):
    return pl.BlockSpec(shape, lambda *_: (0,) * len(shape), pipeline_mode=pl.Buffered(1))


def _params(n_axes):
    return pltpu.CompilerParams(dimension_semantics=("parallel",) * n_axes, vmem_limit_bytes=VMEM_LIMIT)


def _mod_kernel(cc_ref, w_ref, b_ref, o_ref):
    sc = _silu(cc_ref[...])
    o_ref[0] = jnp.dot(sc, w_ref[0], preferred_element_type=F32, precision=lax.Precision.HIGHEST) + b_ref[0]


def _modulation(cc, w_mod, b_mod):
    depth, d, width = w_mod.shape
    bn = 9 * LANES
    return pl.pallas_call(
        _mod_kernel,
        grid=(depth, width // bn),
        in_specs=[
            pl.BlockSpec((MOD_ROWS, d), lambda l, j: (0, 0)),
            pl.BlockSpec((1, d, bn), lambda l, j: (l, 0, j)),
            pl.BlockSpec((1, 1, bn), lambda l, j: (l, 0, j)),
        ],
        out_specs=pl.BlockSpec((1, MOD_ROWS, bn), lambda l, j: (l, 0, j)),
        out_shape=jax.ShapeDtypeStruct((depth, MOD_ROWS, width), F32),
        compiler_params=_params(2),
        name="modulation",
    )(cc, w_mod, b_mod.reshape(depth, 1, width))


MXU_COLS = 256
FFN_CHUNK = MXU_COLS


def _ffn_kernel(*refs, pre, final, k0):
    it = iter(refs)
    x_ref, m_ref = next(it), next(it)
    if pre:
        a_ref, b_ref, c_ref, wo_ref = next(it), next(it), next(it), next(it)
    g_ref, w1_ref, w3_ref, w2_ref = next(it), next(it), next(it), next(it)
    if final:
        fg_ref = next(it)
    o_ref, hmid_ref = next(it), next(it)

    x = x_ref[0]
    if pre:
        mix = jnp.concatenate([a_ref[0], b_ref[0], c_ref[0]], axis=-1)
        x = x + m_ref[0, 5:6, :] * _dot(mix, wo_ref[...])
    xm = _rms(x, g_ref[...]) * (1.0 + m_ref[0, k0 + 1:k0 + 2, :]) + m_ref[0, k0:k0 + 1, :]
    xb = xm.astype(BF16)
    dff = w1_ref.shape[1]
    for lo in range(0, dff, FFN_CHUNK):
        cols = slice(lo, min(lo + FFN_CHUNK, dff))
        a = _dot(xb, w1_ref[:, cols])
        b = _dot(xb, w3_ref[:, cols])
        hmid_ref[:, cols] = (_silu(a) * b).astype(BF16)
    y = _dot(hmid_ref[...], w2_ref[...])
    out = x + (0.5 * m_ref[0, k0 + 2:k0 + 3, :]) * y
    if final:
        out = _rms(out, fg_ref[...])
    o_ref[0] = out


def _ffn(x, mods, mod_row, g, w1, w3, w2, *, k0, tm, mix=None, final_g=None):
    bsz, n, d = x.shape
    dff = w1.shape[1]
    pre, final = mix is not None, final_g is not None
    tok = lambda w: pl.BlockSpec((1, tm, w), lambda b, i: (b, i, 0))
    mod_spec = pl.BlockSpec((1, N_MOD, d), (lambda b, i: (b, 0, 0)) if mod_row is None else (lambda b, i: (mod_row, 0, 0)))
    args, specs = [x, mods], [tok(d), mod_spec]
    if pre:
        a, bb, c, wo = mix
        args += [a, bb, c, wo]
        specs += [tok(a.shape[-1]), tok(bb.shape[-1]), tok(c.shape[-1]), _const_spec(wo.shape)]
    args += [g.reshape(1, d), w1, w3, w2]
    specs += [_const_spec((1, d)), _const_spec(w1.shape), _const_spec(w3.shape), _const_spec(w2.shape)]
    if final:
        args.append(final_g.reshape(1, d))
        specs.append(_const_spec((1, d)))
    return pl.pallas_call(
        functools.partial(_ffn_kernel, pre=pre, final=final, k0=k0),
        grid=(bsz, n // tm),
        in_specs=specs,
        out_specs=tok(d),
        out_shape=jax.ShapeDtypeStruct(x.shape, F32),
        scratch_shapes=[pltpu.VMEM((tm, dff), BF16)],
        compiler_params=_params(2),
        name="ffn_mix" if pre else "ffn",
    )(*args)


def _rope(x, c, sa, sb, shift):
    return x * c + pltpu.roll(x, LANES - shift, 1) * sa + pltpu.roll(x, shift, 1) * sb


def _proj_kernel(x_ref, m_ref, g_ref, wall_ref, wdvt_ref, qn_ref, kvn_ref, wuq_ref, wk_ref, wvt_ref,
                 cm_ref, sam_ref, sbm_ref, cd_ref, sad_ref, sbd_ref,
                 u_ref, qm_ref, km_ref, vmt_ref, qd_ref, kd_ref, vdt_ref):
    x = x_ref[0]
    n = _rms(x, g_ref[...]) * (1.0 + m_ref[0, 4:5, :]) + m_ref[0, 3:4, :]
    nb = n.astype(BF16)
    proj = lambda lo, hi: _dot(nb, wall_ref[:, lo:hi])

    pm = proj(C_CQ, C_DQ)
    cqn = _rms(pm[:, :C_CKV - C_CQ], qn_ref[...]).astype(BF16)
    ckvn = _rms(pm[:, C_CKV - C_CQ:C_KR - C_CQ], kvn_ref[...]).astype(BF16)
    q = _dot(cqn, wuq_ref[...])
    k = _dot(ckvn, wk_ref[...])
    vt = _dot_nt(wvt_ref[...], ckvn).astype(BF16)
    ones = jnp.ones((SUM_ROWS, VT_CHUNK), BF16)
    for j in range(vmt_ref.shape[1]):
        cols = slice(j * VT_CHUNK, (j + 1) * VT_CHUNK)
        for h in range(MLA_HEADS):
            vmt_ref[0, j, h * MLA_VT_ROWS:h * MLA_VT_ROWS + MLA_V, :] = vt[h * MLA_V:(h + 1) * MLA_V, cols]
            vmt_ref[0, j, h * MLA_VT_ROWS + MLA_V:(h + 1) * MLA_VT_ROWS, :] = ones
    cm, sam, sbm = cm_ref[...], sam_ref[...], sbm_ref[...]
    half_m = MLA_ROPE // 2
    kr = _rope(pm[:, C_KR - C_CQ:], cm, sam, sbm, half_m)
    for h in range(MLA_HEADS):
        sl = slice(h * HEAD_PAD, (h + 1) * HEAD_PAD)
        qm_ref[0, :, sl] = (_rope(q[:, sl], cm, sam, sbm, half_m) * MLA_QSCALE).astype(BF16)
        km_ref[0, :, sl] = (k[:, sl] + kr).astype(BF16)

    cd, sad, sbd = cd_ref[...], sad_ref[...], sbd_ref[...]
    half_d = DIFF_DIM // 2
    for col0, out_ref, scale in ((C_DQ, qd_ref, DIFF_QSCALE), (C_DK, kd_ref, None)):
        for h0 in range(0, DIFF_HEADS, 2):
            pp = proj(col0 + h0 * HEAD_PAD, col0 + (h0 + 2) * HEAD_PAD)
            for h in (h0, h0 + 1):
                r = _rope(pp[:, (h - h0) * HEAD_PAD:(h - h0 + 1) * HEAD_PAD], cd, sad, sbd, half_d)
                out_ref[0, :, h * HEAD_PAD:(h + 1) * HEAD_PAD] = (r if scale is None else r * scale).astype(BF16)
    u_ref[0] = proj(C_POOL, C_CQ)
    vdt = _dot_nt(wdvt_ref[...], nb).astype(BF16)
    for j in range(vdt_ref.shape[1]):
        vdt_ref[0, j] = vdt[:, j * VT_CHUNK:(j + 1) * VT_CHUNK]


def _proj(x, mods, mod_row, g, wall, wdvt, qn, kvn, wuq, wk, wvt, tables, *, tm):
    bsz, n, d = x.shape
    tok = lambda w: pl.BlockSpec((1, tm, w), lambda b, i: (b, i, 0))
    vt = lambda r: pl.BlockSpec((1, tm // VT_CHUNK, r, VT_CHUNK), lambda b, i: (b, i, 0, 0))
    mod_spec = pl.BlockSpec((1, N_MOD, d), (lambda b, i: (b, 0, 0)) if mod_row is None else (lambda b, i: (mod_row, 0, 0)))
    tab = pl.BlockSpec((tm, LANES), lambda b, i: (i, 0))
    wm, wd = MLA_HEADS * HEAD_PAD, DIFF_HEADS * HEAD_PAD
    rm, rd = MLA_HEADS * MLA_VT_ROWS, DIFF_HEADS * DIFF_V
    tok_sds = lambda w, dt: jax.ShapeDtypeStruct((bsz, n, w), dt)
    vt_sds = lambda r: jax.ShapeDtypeStruct((bsz, n // VT_CHUNK, r, VT_CHUNK), BF16)
    return pl.pallas_call(
        _proj_kernel,
        grid=(bsz, n // tm),
        in_specs=[tok(d), mod_spec, _const_spec((1, d)), _const_spec(wall.shape), _const_spec(wdvt.shape),
                  _const_spec((1, MLA_Q_RANK)), _const_spec((1, MLA_KV_RANK)),
                  _const_spec(wuq.shape), _const_spec(wk.shape), _const_spec(wvt.shape)] + [tab] * 6,
        out_specs=[tok(POOL_WIDTH), tok(wm), tok(wm), vt(rm), tok(wd), tok(wd), vt(rd)],
        out_shape=[tok_sds(POOL_WIDTH, F32), tok_sds(wm, BF16), tok_sds(wm, BF16), vt_sds(rm),
                   tok_sds(wd, BF16), tok_sds(wd, BF16), vt_sds(rd)],
        compiler_params=_params(2),
        name="mix_in",
    )(x, mods, g.reshape(1, d), wall, wdvt, qn.reshape(1, -1), kvn.reshape(1, -1), wuq, wk, wvt, *tables)


def _nested_window_sums(load, lane):
    a2 = load(-1) + load(0)
    a4 = a2 + load(-2) + load(1)
    a8 = a4 + load(-4) + load(-3) + load(2) + load(3)
    a16 = a8 + load(-8) + load(-7) + load(-6) + load(-5) + load(4) + load(5) + load(6) + load(7)
    return jnp.where(lane < POOL_GROUP, a2, jnp.where(lane < 2 * POOL_GROUP, a4, jnp.where(lane < 3 * POOL_GROUP, a8, a16)))


def _window_sums_1d(slab, n, lane):
    size = slab.shape[0]
    ahead = lambda x, k: pltpu.roll(x, size - k, 0)
    behind = lambda x, k: pltpu.roll(x, k, 0)[POOL_HALF:POOL_HALF + n]
    p2 = slab + ahead(slab, 1)
    p4 = p2 + ahead(p2, 2)
    p8 = p4 + ahead(p4, 4)
    p16 = p8 + ahead(p8, 8)
    return jnp.where(lane < POOL_GROUP, behind(p2, 1),
                     jnp.where(lane < 2 * POOL_GROUP, behind(p4, 2),
                               jnp.where(lane < 3 * POOL_GROUP, behind(p8, 4), p16[:n])))


def _pool_kernel(ul_ref, uc_ref, icg_ref, icx_ref, bd_ref, ps_ref, al_ref, ac_ref,
                 z_ref, y_ref, dl_ref, zc_ref, *, rows):
    gw, pad = GRID_W, POOL_HALF
    stride = gw + 2 * pad
    n_ctx = uc_ref.shape[1]
    lane = lax.broadcasted_iota(jnp.int32, (gw, POOL_WIDTH), 1)
    zero_rows = jnp.zeros((pad * gw, POOL_WIDTH), F32)
    zero_pad = jnp.zeros((pad, POOL_WIDTH), F32)

    z_ref[0:pad * gw, :] = zero_rows
    z_ref[pad * gw:(rows + pad) * gw, :] = ul_ref[0]
    z_ref[(rows + pad) * gw:(rows + 2 * pad) * gw, :] = zero_rows

    def row_pass(r, carry):
        base = pl.multiple_of((r + pad) * gw, 8)
        s = _nested_window_sums(lambda d: z_ref[pl.ds(base + d * gw, gw), :], lane)
        yb = pl.multiple_of(r * stride, 8)
        y_ref[pl.ds(yb, pad), :] = zero_pad
        y_ref[pl.ds(yb + pad, gw), :] = s * icg_ref[pl.ds(r, 1), :]
        y_ref[pl.ds(yb + pad + gw, pad), :] = zero_pad
        return carry

    lax.fori_loop(0, rows, row_pass, 0)

    def col_pass(r, carry):
        slab = y_ref[pl.ds(pl.multiple_of(r * stride, 8), stride), :]
        s = _window_sums_1d(slab, gw, lane)
        tok = pl.ds(pl.multiple_of(r * gw, 8), gw)
        dl_ref[tok, :] = (s * icg_ref[...] - ul_ref[0, tok, :]).astype(BF16)
        return carry

    lax.fori_loop(0, rows, col_pass, 0)
    al_ref[0] = (_dot(dl_ref[...], bd_ref[...]) * ps_ref[...]).astype(BF16)

    uc = uc_ref[0]
    zc_ref[0:pad, :] = zero_pad
    zc_ref[pad:pad + n_ctx, :] = uc
    zc_ref[pad + n_ctx:pad + n_ctx + pad, :] = zero_pad
    lane_c = lax.broadcasted_iota(jnp.int32, (n_ctx, POOL_WIDTH), 1)
    sc = _window_sums_1d(zc_ref[...], n_ctx, lane_c)
    dc = (sc * icx_ref[...] - uc).astype(BF16)
    ac_ref[0] = (_dot(dc, bd_ref[...]) * ps_ref[...]).astype(BF16)


def _pool(u_lat, u_ctx, icg, icx, bd, ps):
    bsz, s, w = u_lat.shape
    n_ctx = u_ctx.shape[1]
    rows = s // GRID_W
    stride = GRID_W + 2 * POOL_HALF
    return pl.pallas_call(
        functools.partial(_pool_kernel, rows=rows),
        grid=(bsz,),
        in_specs=[pl.BlockSpec((1, s, w), lambda b: (b, 0, 0)), pl.BlockSpec((1, n_ctx, w), lambda b: (b, 0, 0)),
                  _const_spec(icg.shape), _const_spec(icx.shape), _const_spec(bd.shape), _const_spec((1, w))],
        out_specs=[pl.BlockSpec((1, s, w), lambda b: (b, 0, 0)), pl.BlockSpec((1, n_ctx, w), lambda b: (b, 0, 0))],
        out_shape=[jax.ShapeDtypeStruct((bsz, s, w), BF16), jax.ShapeDtypeStruct((bsz, n_ctx, w), BF16)],
        scratch_shapes=[pltpu.VMEM(((rows + 2 * POOL_HALF) * GRID_W, w), F32), pltpu.VMEM((rows * stride, w), F32),
                        pltpu.VMEM((s, w), BF16), pltpu.VMEM((n_ctx + 2 * POOL_HALF, w), F32)],
        compiler_params=_params(1),
        name="pool",
    )(u_lat, u_ctx, icg, icx, bd, ps.reshape(1, w))


def _attn_units(once_ref, heads, vrows, prepare, qt_of, k_refs, vt_refs, s_refs, mx_refs, l_refs, acc_refs, finish,
                first=None):
    n_units = len(heads)
    assert first is None or n_units % 2 == 0
    heads = list(heads) + list(heads[:2])
    n = s_refs[0].shape[1]
    offs, off = [], 0
    for k_ref in k_refs:
        offs.append(off)
        off += k_ref.shape[1]

    def loop_body(fns):
        def whole(t, carry):
            for k_ref, vt_ref, o in zip(k_refs, vt_refs, offs):
                n_chunks, kc = vt_ref.shape[1], vt_ref.shape[3]
                for c in range(n_chunks):
                    for fn in fns["chunk"]:
                        fn(k_ref, vt_ref, c, slice(c * kc, (c + 1) * kc), slice(o + c * kc, o + (c + 1) * kc))
            for fn in fns["once"]:
                fn()
            return carry

        lax.fori_loop(0, once_ref[0], whole, 0)

    def scores(u):
        slot = u % 2
        ksl = slice(heads[u] * HEAD_PAD, (heads[u] + 1) * HEAD_PAD)

        def fn(k_ref, vt_ref, c, krows, srows):
            st = _dot(k_ref[0, krows, ksl], qt_of(u))
            s_refs[slot][srows, :] = st
            mx_refs[slot][...] = jnp.maximum(mx_refs[slot][...], jnp.max(st.reshape(-1, 8, n), axis=0))

        return fn

    def probs(u, m):
        slot = u % 2
        vsl = slice(heads[u] * vrows, (heads[u] + 1) * vrows)

        def fn(k_ref, vt_ref, c, krows, srows):
            p = jnp.exp2(s_refs[slot][srows, :] - m)
            if l_refs is not None:
                l_refs[slot][...] += jnp.sum(p.reshape(-1, 8, n), axis=0)
            acc_refs[slot][...] += _dot(vt_ref[0, c, vsl, :], p.astype(BF16))

        return fn

    def epilogue(u):
        slot = u % 2
        return lambda: finish(u, acc_refs[slot][...],
                              None if l_refs is None else jnp.sum(l_refs[slot][...], axis=0, keepdims=True))

    neg_inf = jnp.full(mx_refs[0].shape, -jnp.inf, F32)
    last_unit = n_units if first is None else n_units + 2

    def fill():
        mx_refs[0][...] = neg_inf
        prepare(0)
        loop_body({"chunk": [scores(0)], "once": [lambda: prepare(1)] if n_units > 1 else []})

    if first is None:
        fill()
    else:
        pl.when(first)(fill)
    for u in range(n_units):
        slot = u % 2
        m = jnp.max(mx_refs[slot][...], axis=0, keepdims=True)
        if l_refs is not None:
            l_refs[slot][...] = jnp.zeros(l_refs[slot].shape, F32)
        acc_refs[slot][...] = jnp.zeros(acc_refs[slot].shape, F32)
        fns = {"chunk": [probs(u, m)], "once": [epilogue(u - 1)] if u > 0 else []}
        if u + 2 < last_unit:
            fns["once"].append(functools.partial(prepare, u + 2))
        if u + 1 < min(last_unit, n_units + 1):
            mx_refs[1 - slot][...] = neg_inf
            fns["chunk"].insert(0, scores(u + 1))
        loop_body(fns)
    epilogue(n_units - 1)()


def _first_step(lookahead):
    return (pl.program_id(1) == 0) if lookahead else None


def _mla_attn_kernel(once_ref, *refs, nseg, sub, lookahead):
    q_ref, qn_ref, k_refs, vt_refs = refs[0], refs[1], refs[2:2 + nseg], refs[2 + nseg:2 + 2 * nseg]
    o_ref, s0_ref, s1_ref, mx0_ref, mx1_ref, acc0_ref, acc1_ref, ot_ref, qt_ref = refs[2 + 2 * nseg:]
    units = [(sb, h) for sb in range(q_ref.shape[1] // sub) for h in range(MLA_HEADS)]
    n_units = len(units)

    def prepare(u):
        sb, h = units[u % n_units]
        qh = (q_ref if u < n_units else qn_ref)[0, sb * sub:(sb + 1) * sub, h * HEAD_PAD:(h + 1) * HEAD_PAD]
        qt_ref[u % n_units] = qh.astype(F32).T.astype(BF16)

    def finish(u, acc, l):
        sb, h = units[u]
        ot_ref[(h % 2) * MLA_V:(h % 2 + 1) * MLA_V, :] = acc[:MLA_V] * (1.0 / acc[MLA_V:MLA_V + 1])
        if h % 2 == 1:
            o_ref[0, sb * sub:(sb + 1) * sub, (h - 1) * MLA_V:(h + 1) * MLA_V] = ot_ref[...].T.astype(BF16)

    _attn_units(once_ref, [h for _, h in units], MLA_VT_ROWS, prepare, lambda u: qt_ref[u % n_units],
                k_refs, vt_refs, (s0_ref, s1_ref), (mx0_ref, mx1_ref), None, (acc0_ref, acc1_ref), finish,
                _first_step(lookahead))


def _diff_attn_kernel(once_ref, *refs, nseg, sub, lookahead, lam_init):
    dl_ref, sub_ref, q_ref, qn_ref = refs[0], refs[1], refs[2], refs[3]
    k_refs, vt_refs = refs[4:4 + nseg], refs[4 + nseg:4 + 2 * nseg]
    o_ref, s0_ref, s1_ref, mx0_ref, mx1_ref, acc0_ref, acc1_ref, l0_ref, l1_ref, qu_ref = refs[4 + 2 * nseg:]
    units = [(sb, h) for sb in range(q_ref.shape[1] // sub) for h in range(DIFF_HEADS)]
    n_units = len(units)
    dl = dl_ref[...]
    lam = (jnp.exp(jnp.sum(dl[0:1] * dl[1:2], axis=-1, keepdims=True))
           - jnp.exp(jnp.sum(dl[2:3] * dl[3:4], axis=-1, keepdims=True)) + lam_init)
    dim = lax.broadcasted_iota(jnp.int32, (HEAD_PAD, sub), 0)

    def prepare(u):
        sb, h = units[u % n_units]
        qh = (q_ref if u < n_units else qn_ref)[0, sb * sub:(sb + 1) * sub, h * HEAD_PAD:(h + 1) * HEAD_PAD]
        qt = qh.astype(F32).T
        qu_ref[u % n_units, :, :sub] = jnp.where(dim < DIFF_DIM, qt, 0.0).astype(BF16)
        qu_ref[u % n_units, :, sub:] = jnp.where(dim >= DIFF_DIM, qt, 0.0).astype(BF16)

    def finish(u, acc, l):
        sb, h = units[u]
        o = acc[:, :sub] * (1.0 / l[:, :sub]) - acc[:, sub:] * (lam / l[:, sub:])
        on = o * lax.rsqrt(jnp.mean(o * o, axis=0, keepdims=True) + EPS)
        o_ref[0, sb * sub:(sb + 1) * sub, h * DIFF_V:(h + 1) * DIFF_V] = (
            on.T * sub_ref[...] * (1.0 - lam_init)).astype(BF16)

    _attn_units(once_ref, [h for _, h in units], DIFF_V, prepare, lambda u: qu_ref[u % n_units],
                k_refs, vt_refs, (s0_ref, s1_ref), (mx0_ref, mx1_ref), (l0_ref, l1_ref), (acc0_ref, acc1_ref),
                finish, _first_step(lookahead))


def _attention(kernel, name, q, ks, vts, extra, *, tq, sub, w_out, maps_per_head):
    bsz, nq, wq = q.shape
    nseg = len(ks)
    n_steps = nq // tq
    lookahead = n_steps > 1
    n_heads = wq // HEAD_PAD
    whole = lambda a: pl.BlockSpec((1,) + a.shape[1:], lambda b, i: (b,) + (0,) * (a.ndim - 1))
    vrows = vts[0].shape[2] // n_heads
    n_keys = sum(k.shape[1] for k in ks)
    n = sub * maps_per_head
    scratch = [pltpu.VMEM((n_keys, n), F32)] * 2 + [pltpu.VMEM((8, n), F32)] * 2 + [pltpu.VMEM((vrows, n), F32)] * 2
    qt_scratch = pltpu.VMEM((n_heads * tq // sub, HEAD_PAD, n), BF16)
    if maps_per_head > 1:
        scratch += [pltpu.VMEM((8, n), F32)] * 2 + [qt_scratch]
    else:
        scratch += [pltpu.VMEM((LANES, n), F32), qt_scratch]
    once = jnp.ones((1,), jnp.int32)
    q_spec = pl.BlockSpec((1, tq, wq), lambda b, i: (b, i, 0))
    q_next_spec = pl.BlockSpec((1, tq, wq), lambda b, i: (b, jnp.minimum(i + 1, n_steps - 1), 0))
    semantics = ("parallel", "arbitrary" if lookahead else "parallel")
    return pl.pallas_call(
        functools.partial(kernel, nseg=nseg, sub=sub, lookahead=lookahead),
        grid=(bsz, n_steps),
        in_specs=[pl.BlockSpec(memory_space=pltpu.SMEM)] + [_const_spec(e.shape) for e in extra]
        + [q_spec, q_next_spec] + [whole(k) for k in ks] + [whole(v) for v in vts],
        out_specs=pl.BlockSpec((1, tq, w_out), lambda b, i: (b, i, 0)),
        out_shape=jax.ShapeDtypeStruct((bsz, nq, w_out), BF16),
        scratch_shapes=scratch,
        compiler_params=pltpu.CompilerParams(dimension_semantics=semantics, vmem_limit_bytes=VMEM_LIMIT),
        name=name,
    )(once, *extra, q, q, *ks, *vts)


def _axial_angles(n_tok, rot_dim):
    t = jnp.arange(n_tok)
    row = (t // GRID_W).astype(F32)
    col = (t % GRID_W).astype(F32)
    n_axis = rot_dim // 4
    inv = ROPE_BASE ** (-jnp.arange(n_axis, dtype=F32) / n_axis)
    return jnp.concatenate([row[:, None] * inv, col[:, None] * inv], axis=-1)


def _rope_tables(n_tok, n_ctx):
    am, ad = _axial_angles(n_tok, MLA_ROPE), _axial_angles(n_tok, DIFF_DIM)
    one = lambda w: jnp.ones((n_tok, w), F32)
    zero = lambda w: jnp.zeros((n_tok, w), F32)
    cm = jnp.concatenate([one(MLA_NOPE), jnp.cos(am), jnp.cos(am), one(HEAD_PAD - MLA_NOPE - MLA_ROPE)], axis=-1)
    sam = jnp.concatenate([zero(MLA_NOPE), -jnp.sin(am), zero(HEAD_PAD - MLA_NOPE - MLA_ROPE // 2)], axis=-1)
    sbm = jnp.concatenate([zero(MLA_NOPE + MLA_ROPE // 2), jnp.sin(am), zero(HEAD_PAD - MLA_NOPE - MLA_ROPE)], axis=-1)
    cd = jnp.tile(jnp.cos(ad), (1, 4))
    sad = jnp.tile(jnp.concatenate([-jnp.sin(ad), zero(DIFF_DIM // 2)], axis=-1), (1, 2))
    sbd = jnp.tile(jnp.concatenate([zero(DIFF_DIM // 2), jnp.sin(ad)], axis=-1), (1, 2))
    lat = (cm, sam, sbm, cd, sad, sbd)
    ident = (jnp.ones((n_ctx, LANES), F32),) + (jnp.zeros((n_ctx, LANES), F32),) * 2
    return lat, ident + ident


def _inv_counts(n):
    idx = jnp.arange(n)
    cols = []
    for k in POOL_WINDOWS:
        lo, hi = k // 2, k - 1 - k // 2
        cnt = (jnp.clip(idx + hi + 1, 0, n) - jnp.clip(idx - lo, 0, n)).astype(F32)
        cols.append(jnp.broadcast_to((1.0 / cnt)[:, None], (n, POOL_GROUP)))
    return jnp.concatenate(cols, axis=-1)


def _layer_weights(w_in, w_out, pool_w, mla_w_uq, mla_w_ukv):
    d = w_in.shape[0]
    s1, s2 = POOL_WIDTH, POOL_WIDTH + MLA_Q_RANK + MLA_KV_RANK + MLA_ROPE
    kr_pad = jnp.zeros((d, HEAD_PAD), F32).at[:, MLA_NOPE:MLA_NOPE + MLA_ROPE].set(w_in[:, s2 - MLA_ROPE:s2])
    s3 = s2 + 2 * DIFF_HEADS * 2 * DIFF_DIM
    wall = jnp.concatenate([w_in[:, :s2 - MLA_ROPE], kr_pad, w_in[:, s2:s3]], axis=-1).astype(BF16)
    wdvt = w_in[:, s3:].T.astype(BF16)
    qd = MLA_NOPE + MLA_ROPE
    wuq = jnp.pad(mla_w_uq.reshape(MLA_Q_RANK, MLA_HEADS, qd), ((0, 0), (0, 0), (0, HEAD_PAD - qd)))
    wuq = wuq.reshape(MLA_Q_RANK, MLA_HEADS * HEAD_PAD).astype(BF16)
    ukv = mla_w_ukv.reshape(MLA_KV_RANK, MLA_HEADS, MLA_NOPE + MLA_V)
    wk = jnp.pad(ukv[:, :, :MLA_NOPE], ((0, 0), (0, 0), (0, HEAD_PAD - MLA_NOPE)))
    wk = wk.reshape(MLA_KV_RANK, MLA_HEADS * HEAD_PAD).astype(BF16)
    wvt = ukv[:, :, MLA_NOPE:].reshape(MLA_KV_RANK, MLA_HEADS * MLA_V).T.astype(BF16)
    bd = jax.scipy.linalg.block_diag(*[pool_w[g] for g in range(len(POOL_WINDOWS))]).astype(BF16)
    return wall, wdvt, wuq, wk, wvt, bd, w_out.astype(BF16)


def kernel(x, c, ctx, c_ctx, w_mod, b_mod, ffn1_norm, ffn1_w1, ffn1_w3, ffn1_w2, mix_norm, w_in, w_out, pool_w, pool_scale, mla_q_norm, mla_w_uq, mla_kv_norm, mla_w_ukv, diff_lambda, diff_subln, ffn2_norm, ffn2_w1, ffn2_w3, ffn2_w2, final_norm):
    bsz, seq, d = x.shape
    n_ctx = ctx.shape[1]
    depth = w_mod.shape[0]
    assert bsz + 1 <= MOD_ROWS and seq % GRID_W == 0
    tm = 512
    ctx_row = bsz

    cc = jnp.zeros((MOD_ROWS, d), F32).at[:bsz].set(c).at[ctx_row].set(c_ctx)
    mods_all = _modulation(cc, w_mod, b_mod).reshape(depth, MOD_ROWS, N_MOD, d)

    tab_lat, tab_ctx = _rope_tables(seq, n_ctx)
    icg, icx = _inv_counts(GRID_W), _inv_counts(n_ctx)
    bf = lambda w: w.astype(BF16)
    mla_args = dict(w_out=MLA_HEADS * MLA_V, maps_per_head=1)
    dif_args = dict(w_out=DIFF_HEADS * DIFF_V, maps_per_head=2)

    h, hc = x, ctx
    for i in range(depth):
        last = i == depth - 1
        mods = mods_all[i]
        lam_init = 0.8 - 0.6 * math.exp(-0.3 * i)
        wall, wdvt, wuq, wk, wvt, bd, wo = _layer_weights(w_in[i], w_out[i], pool_w[i], mla_w_uq[i], mla_w_ukv[i])
        f1 = (ffn1_norm[i], bf(ffn1_w1[i]), bf(ffn1_w3[i]), bf(ffn1_w2[i]))
        f2 = (ffn2_norm[i], bf(ffn2_w1[i]), bf(ffn2_w3[i]), bf(ffn2_w2[i]))

        h = _ffn(h, mods, None, *f1, k0=0, tm=tm)
        hc = _ffn(hc, mods, ctx_row, *f1, k0=0, tm=n_ctx)

        pw = (mix_norm[i], wall, wdvt, mla_q_norm[i], mla_kv_norm[i], wuq, wk, wvt)
        u, qm, km, vm, qd, kd, vd = _proj(h, mods, None, *pw, tab_lat, tm=tm)
        uc, qmc, kmc, vmc, qdc, kdc, vdc = _proj(hc, mods, ctx_row, *pw, tab_ctx, tm=n_ctx)

        a, ac = _pool(u, uc, icg, icx, bd, pool_scale[i])
        dif = functools.partial(_diff_attn_kernel, lam_init=lam_init)
        dextra = (diff_lambda[i], diff_subln[i].reshape(1, DIFF_V))
        b = _attention(_mla_attn_kernel, "mla_attn", qm, (km, kmc), (vm, vmc), (),
                       tq=2048, sub=512, **mla_args)
        cdiff = _attention(dif, "diff_attn", qd, (kd, kdc), (vd, vdc), dextra,
                           tq=1024, sub=256, **dif_args)
        h = _ffn(h, mods, None, *f2, k0=6, tm=tm, mix=(a, b, cdiff, wo),
                 final_g=final_norm if last else None)
        if not last:
            bc = _attention(_mla_attn_kernel, "mla_attn_ctx", qmc, (kmc,), (vmc,), (),
                            tq=n_ctx, sub=n_ctx, **mla_args)
            cc_ = _attention(dif, "diff_attn_ctx", qdc, (kdc,), (vdc,), dextra,
                             tq=n_ctx, sub=n_ctx, **dif_args)
            hc = _ffn(hc, mods, ctx_row, *f2, k0=6, tm=n_ctx, mix=(ac, bc, cc_, wo))
    return h
```

```python
import functools
import math

import jax
import jax.numpy as jnp
from jax import lax
from jax.experimental import pallas as pl
from jax.experimental.pallas import tpu as pltpu

F32 = jnp.float32
BF16 = jnp.bfloat16

GRID_W = 64
N_MOD = 9
POOL_WINDOWS = (2, 4, 8, 16)
POOL_GROUP = 64
POOL_WIDTH = POOL_GROUP * len(POOL_WINDOWS)
POOL_HALF = max(POOL_WINDOWS) // 2
MLA_HEADS = 4
MLA_NOPE = 64
MLA_ROPE = 32
MLA_V = 64
MLA_Q_RANK = 384
MLA_KV_RANK = 256
DIFF_HEADS = 4
DIFF_DIM = 64
DIFF_V = 2 * DIFF_DIM
ROPE_BASE = 10000.0
EPS = 1e-6

LANES = 128
HEAD_PAD = LANES
VT_CHUNK = 256
SUM_ROWS = 16
MLA_VT_ROWS = MLA_V + SUM_ROWS
MOD_ROWS = 24
VMEM_LIMIT = 56 * 1024 * 1024

C_POOL = 0
C_CQ = C_POOL + POOL_WIDTH
C_CKV = C_CQ + MLA_Q_RANK
C_KR = C_CKV + MLA_KV_RANK
C_DQ = C_KR + HEAD_PAD
C_DK = C_DQ + DIFF_HEADS * 2 * DIFF_DIM
C_DV = C_DK + DIFF_HEADS * 2 * DIFF_DIM

LOG2E = math.log2(math.e)
MLA_QSCALE = (MLA_NOPE + MLA_ROPE) ** -0.5 * LOG2E
DIFF_QSCALE = DIFF_DIM ** -0.5 * LOG2E


def _dot(a, b):
    return jnp.dot(a, b, preferred_element_type=F32)


def _dot_nt(a, b):
    return lax.dot_general(a, b, (((1,), (1,)), ((), ())), preferred_element_type=F32)


def _rms(x, g):
    return x * lax.rsqrt(jnp.mean(x * x, axis=-1, keepdims=True) + EPS) * g


def _silu(a):
    return a * (1.0 / (1.0 + jnp.exp(-a)))


def _const_spec(shape):
    return pl.BlockSpec(shape, lambda *_: (0,) * len(shape), pipeline_mode=pl.Buffered(1))


def _params(n_axes):
    return pltpu.CompilerParams(dimension_semantics=("parallel",) * n_axes, vmem_limit_bytes=VMEM_LIMIT)


def _mod_kernel(cc_ref, w_ref, b_ref, o_ref):
    sc = _silu(cc_ref[...])
    o_ref[0] = jnp.dot(sc, w_ref[0], preferred_element_type=F32, precision=lax.Precision.HIGHEST) + b_ref[0]


def _modulation(cc, w_mod, b_mod):
    depth, d, width = w_mod.shape
    bn = 9 * LANES
    return pl.pallas_call(
        _mod_kernel,
        grid=(depth, width // bn),
        in_specs=[
            pl.BlockSpec((MOD_ROWS, d), lambda l, j: (0, 0)),
            pl.BlockSpec((1, d, bn), lambda l, j: (l, 0, j)),
            pl.BlockSpec((1, 1, bn), lambda l, j: (l, 0, j)),
        ],
        out_specs=pl.BlockSpec((1, MOD_ROWS, bn), lambda l, j: (l, 0, j)),
        out_shape=jax.ShapeDtypeStruct((depth, MOD_ROWS, width), F32),
        compiler_params=_params(2),
        name="modulation",
    )(cc, w_mod, b_mod.reshape(depth, 1, width))


MXU_COLS = 256
FFN_CHUNK = MXU_COLS


def _ffn_kernel(*refs, pre, final, k0):
    it = iter(refs)
    x_ref, m_ref = next(it), next(it)
    if pre:
        a_ref, b_ref, c_ref, wo_ref = next(it), next(it), next(it), next(it)
    g_ref, w1_ref, w3_ref, w2_ref = next(it), next(it), next(it), next(it)
    if final:
        fg_ref = next(it)
    o_ref, hmid_ref = next(it), next(it)

    x = x_ref[0]
    if pre:
        mix = jnp.concatenate([a_ref[0], b_ref[0], c_ref[0]], axis=-1)
        x = x + m_ref[0, 5:6, :] * _dot(mix, wo_ref[...])
    xm = _rms(x, g_ref[...]) * (1.0 + m_ref[0, k0 + 1:k0 + 2, :]) + m_ref[0, k0:k0 + 1, :]
    xb = xm.astype(BF16)
    dff = w1_ref.shape[1]
    for lo in range(0, dff, FFN_CHUNK):
        cols = slice(lo, min(lo + FFN_CHUNK, dff))
        a = _dot(xb, w1_ref[:, cols])
        b = _dot(xb, w3_ref[:, cols])
        hmid_ref[:, cols] = (_silu(a) * b).astype(BF16)
    y = _dot(hmid_ref[...], w2_ref[...])
    out = x + (0.5 * m_ref[0, k0 + 2:k0 + 3, :]) * y
    if final:
        out = _rms(out, fg_ref[...])
    o_ref[0] = out


def _ffn(x, mods, mod_row, g, w1, w3, w2, *, k0, tm, mix=None, final_g=None):
    bsz, n, d = x.shape
    dff = w1.shape[1]
    pre, final = mix is not None, final_g is not None
    tok = lambda w: pl.BlockSpec((1, tm, w), lambda b, i: (b, i, 0))
    mod_spec = pl.BlockSpec((1, N_MOD, d), (lambda b, i: (b, 0, 0)) if mod_row is None else (lambda b, i: (mod_row, 0, 0)))
    args, specs = [x, mods], [tok(d), mod_spec]
    if pre:
        a, bb, c, wo = mix
        args += [a, bb, c, wo]
        specs += [tok(a.shape[-1]), tok(bb.shape[-1]), tok(c.shape[-1]), _const_spec(wo.shape)]
    args += [g.reshape(1, d), w1, w3, w2]
    specs += [_const_spec((1, d)), _const_spec(w1.shape), _const_spec(w3.shape), _const_spec(w2.shape)]
    if final:
        args.append(final_g.reshape(1, d))
        specs.append(_const_spec((1, d)))
    return pl.pallas_call(
        functools.partial(_ffn_kernel, pre=pre, final=final, k0=k0),
        grid=(bsz, n // tm),
        in_specs=specs,
        out_specs=tok(d),
        out_shape=jax.ShapeDtypeStruct(x.shape, F32),
        scratch_shapes=[pltpu.VMEM((tm, dff), BF16)],
        compiler_params=_params(2),
        name="ffn_mix" if pre else "ffn",
    )(*args)


def _rope(x, c, sa, sb, shift):
    return x * c + pltpu.roll(x, LANES - shift, 1) * sa + pltpu.roll(x, shift, 1) * sb


def _proj_kernel(x_ref, m_ref, g_ref, wall_ref, wdvt_ref, qn_ref, kvn_ref, wuq_ref, wk_ref, wvt_ref,
                 cm_ref, sam_ref, sbm_ref, cd_ref, sad_ref, sbd_ref,
                 u_ref, qm_ref, km_ref, vmt_ref, qd_ref, kd_ref, vdt_ref):
    x = x_ref[0]
    n = _rms(x, g_ref[...]) * (1.0 + m_ref[0, 4:5, :]) + m_ref[0, 3:4, :]
    nb = n.astype(BF16)
    proj = lambda lo, hi: _dot(nb, wall_ref[:, lo:hi])

    pm = proj(C_CQ, C_DQ)
    cqn = _rms(pm[:, :C_CKV - C_CQ], qn_ref[...]).astype(BF16)
    ckvn = _rms(pm[:, C_CKV - C_CQ:C_KR - C_CQ], kvn_ref[...]).astype(BF16)
    q = _dot(cqn, wuq_ref[...])
    k = _dot(ckvn, wk_ref[...])
    vt = _dot_nt(wvt_ref[...], ckvn).astype(BF16)
    ones = jnp.ones((SUM_ROWS, VT_CHUNK), BF16)
    for j in range(vmt_ref.shape[1]):
        cols = slice(j * VT_CHUNK, (j + 1) * VT_CHUNK)
        for h in range(MLA_HEADS):
            vmt_ref[0, j, h * MLA_VT_ROWS:h * MLA_VT_ROWS + MLA_V, :] = vt[h * MLA_V:(h + 1) * MLA_V, cols]
            vmt_ref[0, j, h * MLA_VT_ROWS + MLA_V:(h + 1) * MLA_VT_ROWS, :] = ones
    cm, sam, sbm = cm_ref[...], sam_ref[...], sbm_ref[...]
    half_m = MLA_ROPE // 2
    kr = _rope(pm[:, C_KR - C_CQ:], cm, sam, sbm, half_m)
    for h in range(MLA_HEADS):
        sl = slice(h * HEAD_PAD, (h + 1) * HEAD_PAD)
        qm_ref[0, :, sl] = (_rope(q[:, sl], cm, sam, sbm, half_m) * MLA_QSCALE).astype(BF16)
        km_ref[0, :, sl] = (k[:, sl] + kr).astype(BF16)

    cd, sad, sbd = cd_ref[...], sad_ref[...], sbd_ref[...]
    half_d = DIFF_DIM // 2
    for col0, out_ref, scale in ((C_DQ, qd_ref, DIFF_QSCALE), (C_DK, kd_ref, None)):
        for h0 in range(0, DIFF_HEADS, 2):
            pp = proj(col0 + h0 * HEAD_PAD, col0 + (h0 + 2) * HEAD_PAD)
            for h in (h0, h0 + 1):
                r = _rope(pp[:, (h - h0) * HEAD_PAD:(h - h0 + 1) * HEAD_PAD], cd, sad, sbd, half_d)
                out_ref[0, :, h * HEAD_PAD:(h + 1) * HEAD_PAD] = (r if scale is None else r * scale).astype(BF16)
    u_ref[0] = proj(C_POOL, C_CQ)
    vdt = _dot_nt(wdvt_ref[...], nb).astype(BF16)
    for j in range(vdt_ref.shape[1]):
        vdt_ref[0, j] = vdt[:, j * VT_CHUNK:(j + 1) * VT_CHUNK]


def _proj(x, mods, mod_row, g, wall, wdvt, qn, kvn, wuq, wk, wvt, tables, *, tm):
    bsz, n, d = x.shape
    tok = lambda w: pl.BlockSpec((1, tm, w), lambda b, i: (b, i, 0))
    vt = lambda r: pl.BlockSpec((1, tm // VT_CHUNK, r, VT_CHUNK), lambda b, i: (b, i, 0, 0))
    mod_spec = pl.BlockSpec((1, N_MOD, d), (lambda b, i: (b, 0, 0)) if mod_row is None else (lambda b, i: (mod_row, 0, 0)))
    tab = pl.BlockSpec((tm, LANES), lambda b, i: (i, 0))
    wm, wd = MLA_HEADS * HEAD_PAD, DIFF_HEADS * HEAD_PAD
    rm, rd = MLA_HEADS * MLA_VT_ROWS, DIFF_HEADS * DIFF_V
    tok_sds = lambda w, dt: jax.ShapeDtypeStruct((bsz, n, w), dt)
    vt_sds = lambda r: jax.ShapeDtypeStruct((bsz, n // VT_CHUNK, r, VT_CHUNK), BF16)
    return pl.pallas_call(
        _proj_kernel,
        grid=(bsz, n // tm),
        in_specs=[tok(d), mod_spec, _const_spec((1, d)), _const_spec(wall.shape), _const_spec(wdvt.shape),
                  _const_spec((1, MLA_Q_RANK)), _const_spec((1, MLA_KV_RANK)),
                  _const_spec(wuq.shape), _const_spec(wk.shape), _const_spec(wvt.shape)] + [tab] * 6,
        out_specs=[tok(POOL_WIDTH), tok(wm), tok(wm), vt(rm), tok(wd), tok(wd), vt(rd)],
        out_shape=[tok_sds(POOL_WIDTH, F32), tok_sds(wm, BF16), tok_sds(wm, BF16), vt_sds(rm),
                   tok_sds(wd, BF16), tok_sds(wd, BF16), vt_sds(rd)],
        compiler_params=_params(2),
        name="mix_in",
    )(x, mods, g.reshape(1, d), wall, wdvt, qn.reshape(1, -1), kvn.reshape(1, -1), wuq, wk, wvt, *tables)


def _nested_window_sums(load, lane):
    a2 = load(-1) + load(0)
    a4 = a2 + load(-2) + load(1)
    a8 = a4 + load(-4) + load(-3) + load(2) + load(3)
    a16 = a8 + load(-8) + load(-7) + load(-6) + load(-5) + load(4) + load(5) + load(6) + load(7)
    return jnp.where(lane < POOL_GROUP, a2, jnp.where(lane < 2 * POOL_GROUP, a4, jnp.where(lane < 3 * POOL_GROUP, a8, a16)))


def _window_sums_1d(slab, n, lane):
    size = slab.shape[0]
    ahead = lambda x, k: pltpu.roll(x, size - k, 0)
    behind = lambda x, k: pltpu.roll(x, k, 0)[POOL_HALF:POOL_HALF + n]
    p2 = slab + ahead(slab, 1)
    p4 = p2 + ahead(p2, 2)
    p8 = p4 + ahead(p4, 4)
    p16 = p8 + ahead(p8, 8)
    return jnp.where(lane < POOL_GROUP, behind(p2, 1),
                     jnp.where(lane < 2 * POOL_GROUP, behind(p4, 2),
                               jnp.where(lane < 3 * POOL_GROUP, behind(p8, 4), p16[:n])))


def _pool_kernel(ul_ref, uc_ref, icg_ref, icx_ref, bd_ref, ps_ref, al_ref, ac_ref,
                 z_ref, y_ref, dl_ref, zc_ref, *, rows):
    gw, pad = GRID_W, POOL_HALF
    stride = gw + 2 * pad
    n_ctx = uc_ref.shape[1]
    lane = lax.broadcasted_iota(jnp.int32, (gw, POOL_WIDTH), 1)
    zero_rows = jnp.zeros((pad * gw, POOL_WIDTH), F32)
    zero_pad = jnp.zeros((pad, POOL_WIDTH), F32)

    z_ref[0:pad * gw, :] = zero_rows
    z_ref[pad * gw:(rows + pad) * gw, :] = ul_ref[0]
    z_ref[(rows + pad) * gw:(rows + 2 * pad) * gw, :] = zero_rows

    def row_pass(r, carry):
        base = pl.multiple_of((r + pad) * gw, 8)
        s = _nested_window_sums(lambda d: z_ref[pl.ds(base + d * gw, gw), :], lane)
        yb = pl.multiple_of(r * stride, 8)
        y_ref[pl.ds(yb, pad), :] = zero_pad
        y_ref[pl.ds(yb + pad, gw), :] = s * icg_ref[pl.ds(r, 1), :]
        y_ref[pl.ds(yb + pad + gw, pad), :] = zero_pad
        return carry

    lax.fori_loop(0, rows, row_pass, 0)

    def col_pass(r, carry):
        slab = y_ref[pl.ds(pl.multiple_of(r * stride, 8), stride), :]
        s = _window_sums_1d(slab, gw, lane)
        tok = pl.ds(pl.multiple_of(r * gw, 8), gw)
        dl_ref[tok, :] = (s * icg_ref[...] - ul_ref[0, tok, :]).astype(BF16)
        return carry

    lax.fori_loop(0, rows, col_pass, 0)
    al_ref[0] = (_dot(dl_ref[...], bd_ref[...]) * ps_ref[...]).astype(BF16)

    uc = uc_ref[0]
    zc_ref[0:pad, :] = zero_pad
    zc_ref[pad:pad + n_ctx, :] = uc
    zc_ref[pad + n_ctx:pad + n_ctx + pad, :] = zero_pad
    lane_c = lax.broadcasted_iota(jnp.int32, (n_ctx, POOL_WIDTH), 1)
    sc = _window_sums_1d(zc_ref[...], n_ctx, lane_c)
    dc = (sc * icx_ref[...] - uc).astype(BF16)
    ac_ref[0] = (_dot(dc, bd_ref[...]) * ps_ref[...]).astype(BF16)


def _pool(u_lat, u_ctx, icg, icx, bd, ps):
    bsz, s, w = u_lat.shape
    n_ctx = u_ctx.shape[1]
    rows = s // GRID_W
    stride = GRID_W + 2 * POOL_HALF
    return pl.pallas_call(
        functools.partial(_pool_kernel, rows=rows),
        grid=(bsz,),
        in_specs=[pl.BlockSpec((1, s, w), lambda b: (b, 0, 0)), pl.BlockSpec((1, n_ctx, w), lambda b: (b, 0, 0)),
                  _const_spec(icg.shape), _const_spec(icx.shape), _const_spec(bd.shape), _const_spec((1, w))],
        out_specs=[pl.BlockSpec((1, s, w), lambda b: (b, 0, 0)), pl.BlockSpec((1, n_ctx, w), lambda b: (b, 0, 0))],
        out_shape=[jax.ShapeDtypeStruct((bsz, s, w), BF16), jax.ShapeDtypeStruct((bsz, n_ctx, w), BF16)],
        scratch_shapes=[pltpu.VMEM(((rows + 2 * POOL_HALF) * GRID_W, w), F32), pltpu.VMEM((rows * stride, w), F32),
                        pltpu.VMEM((s, w), BF16), pltpu.VMEM((n_ctx + 2 * POOL_HALF, w), F32)],
        compiler_params=_params(1),
        name="pool",
    )(u_lat, u_ctx, icg, icx, bd, ps.reshape(1, w))


def _attn_units(once_ref, heads, vrows, prepare, qt_of, k_refs, vt_refs, s_refs, mx_refs, l_refs, acc_refs, finish,
                first=None):
    n_units = len(heads)
    assert first is None or n_units % 2 == 0
    heads = list(heads) + list(heads[:2])
    n = s_refs[0].shape[1]
    offs, off = [], 0
    for k_ref in k_refs:
        offs.append(off)
        off += k_ref.shape[1]

    def loop_body(fns):
        def whole(t, carry):
            for k_ref, vt_ref, o in zip(k_refs, vt_refs, offs):
                n_chunks, kc = vt_ref.shape[1], vt_ref.shape[3]
                for c in range(n_chunks):
                    for fn in fns["chunk"]:
                        fn(k_ref, vt_ref, c, slice(c * kc, (c + 1) * kc), slice(o + c * kc, o + (c + 1) * kc))
            for fn in fns["once"]:
                fn()
            return carry

        lax.fori_loop(0, once_ref[0], whole, 0)

    def scores(u):
        slot = u % 2
        ksl = slice(heads[u] * HEAD_PAD, (heads[u] + 1) * HEAD_PAD)

        def fn(k_ref, vt_ref, c, krows, srows):
            st = _dot(k_ref[0, krows, ksl], qt_of(u))
            s_refs[slot][srows, :] = st
            mx_refs[slot][...] = jnp.maximum(mx_refs[slot][...], jnp.max(st.reshape(-1, 8, n), axis=0))

        return fn

    def probs(u, m):
        slot = u % 2
        vsl = slice(heads[u] * vrows, (heads[u] + 1) * vrows)

        def fn(k_ref, vt_ref, c, krows, srows):
            p = jnp.exp2(s_refs[slot][srows, :] - m)
            if l_refs is not None:
                l_refs[slot][...] += jnp.sum(p.reshape(-1, 8, n), axis=0)
            acc_refs[slot][...] += _dot(vt_ref[0, c, vsl, :], p.astype(BF16))

        return fn

    def epilogue(u):
        slot = u % 2
        return lambda: finish(u, acc_refs[slot][...],
                              None if l_refs is None else jnp.sum(l_refs[slot][...], axis=0, keepdims=True))

    neg_inf = jnp.full(mx_refs[0].shape, -jnp.inf, F32)
    last_unit = n_units if first is None else n_units + 2

    def fill():
        mx_refs[0][...] = neg_inf
        prepare(0)
        loop_body({"chunk": [scores(0)], "once": [lambda: prepare(1)] if n_units > 1 else []})

    if first is None:
        fill()
    else:
        pl.when(first)(fill)
    for u in range(n_units):
        slot = u % 2
        m = jnp.max(mx_refs[slot][...], axis=0, keepdims=True)
        if l_refs is not None:
            l_refs[slot][...] = jnp.zeros(l_refs[slot].shape, F32)
        acc_refs[slot][...] = jnp.zeros(acc_refs[slot].shape, F32)
        fns = {"chunk": [probs(u, m)], "once": [epilogue(u - 1)] if u > 0 else []}
        if u + 2 < last_unit:
            fns["once"].append(functools.partial(prepare, u + 2))
        if u + 1 < min(last_unit, n_units + 1):
            mx_refs[1 - slot][...] = neg_inf
            fns["chunk"].insert(0, scores(u + 1))
        loop_body(fns)
    epilogue(n_units - 1)()


def _first_step(lookahead):
    return (pl.program_id(1) == 0) if lookahead else None


def _mla_attn_kernel(once_ref, *refs, nseg, sub, lookahead):
    q_ref, qn_ref, k_refs, vt_refs = refs[0], refs[1], refs[2:2 + nseg], refs[2 + nseg:2 + 2 * nseg]
    o_ref, s0_ref, s1_ref, mx0_ref, mx1_ref, acc0_ref, acc1_ref, ot_ref, qt_ref = refs[2 + 2 * nseg:]
    units = [(sb, h) for sb in range(q_ref.shape[1] // sub) for h in range(MLA_HEADS)]
    n_units = len(units)

    def prepare(u):
        sb, h = units[u % n_units]
        qh = (q_ref if u < n_units else qn_ref)[0, sb * sub:(sb + 1) * sub, h * HEAD_PAD:(h + 1) * HEAD_PAD]
        qt_ref[u % n_units] = qh.astype(F32).T.astype(BF16)

    def finish(u, acc, l):
        sb, h = units[u]
        ot_ref[(h % 2) * MLA_V:(h % 2 + 1) * MLA_V, :] = acc[:MLA_V] * (1.0 / acc[MLA_V:MLA_V + 1])
        if h % 2 == 1:
            o_ref[0, sb * sub:(sb + 1) * sub, (h - 1) * MLA_V:(h + 1) * MLA_V] = ot_ref[...].T.astype(BF16)

    _attn_units(once_ref, [h for _, h in units], MLA_VT_ROWS, prepare, lambda u: qt_ref[u % n_units],
                k_refs, vt_refs, (s0_ref, s1_ref), (mx0_ref, mx1_ref), None, (acc0_ref, acc1_ref), finish,
                _first_step(lookahead))


def _diff_attn_kernel(once_ref, *refs, nseg, sub, lookahead, lam_init):
    dl_ref, sub_ref, q_ref, qn_ref = refs[0], refs[1], refs[2], refs[3]
    k_refs, vt_refs = refs[4:4 + nseg], refs[4 + nseg:4 + 2 * nseg]
    o_ref, s0_ref, s1_ref, mx0_ref, mx1_ref, acc0_ref, acc1_ref, l0_ref, l1_ref, qu_ref = refs[4 + 2 * nseg:]
    units = [(sb, h) for sb in range(q_ref.shape[1] // sub) for h in range(DIFF_HEADS)]
    n_units = len(units)
    dl = dl_ref[...]
    lam = (jnp.exp(jnp.sum(dl[0:1] * dl[1:2], axis=-1, keepdims=True))
           - jnp.exp(jnp.sum(dl[2:3] * dl[3:4], axis=-1, keepdims=True)) + lam_init)
    dim = lax.broadcasted_iota(jnp.int32, (HEAD_PAD, sub), 0)

    def prepare(u):
        sb, h = units[u % n_units]
        qh = (q_ref if u < n_units else qn_ref)[0, sb * sub:(sb + 1) * sub, h * HEAD_PAD:(h + 1) * HEAD_PAD]
        qt = qh.astype(F32).T
        qu_ref[u % n_units, :, :sub] = jnp.where(dim < DIFF_DIM, qt, 0.0).astype(BF16)
        qu_ref[u % n_units, :, sub:] = jnp.where(dim >= DIFF_DIM, qt, 0.0).astype(BF16)

    def finish(u, acc, l):
        sb, h = units[u]
        o = acc[:, :sub] * (1.0 / l[:, :sub]) - acc[:, sub:] * (lam / l[:, sub:])
        on = o * lax.rsqrt(jnp.mean(o * o, axis=0, keepdims=True) + EPS)
        o_ref[0, sb * sub:(sb + 1) * sub, h * DIFF_V:(h + 1) * DIFF_V] = (
            on.T * sub_ref[...] * (1.0 - lam_init)).astype(BF16)

    _attn_units(once_ref, [h for _, h in units], DIFF_V, prepare, lambda u: qu_ref[u % n_units],
                k_refs, vt_refs, (s0_ref, s1_ref), (mx0_ref, mx1_ref), (l0_ref, l1_ref), (acc0_ref, acc1_ref),
                finish, _first_step(lookahead))


def _attention(kernel, name, q, ks, vts, extra, *, tq, sub, w_out, maps_per_head):
    bsz, nq, wq = q.shape
    nseg = len(ks)
    n_steps = nq // tq
    lookahead = n_steps > 1
    n_heads = wq // HEAD_PAD
    whole = lambda a: pl.BlockSpec((1,) + a.shape[1:], lambda b, i: (b,) + (0,) * (a.ndim - 1))
    vrows = vts[0].shape[2] // n_heads
    n_keys = sum(k.shape[1] for k in ks)
    n = sub * maps_per_head
    scratch = [pltpu.VMEM((n_keys, n), F32)] * 2 + [pltpu.VMEM((8, n), F32)] * 2 + [pltpu.VMEM((vrows, n), F32)] * 2
    qt_scratch = pltpu.VMEM((n_heads * tq // sub, HEAD_PAD, n), BF16)
    if maps_per_head > 1:
        scratch += [pltpu.VMEM((8, n), F32)] * 2 + [qt_scratch]
    else:
        scratch += [pltpu.VMEM((LANES, n), F32), qt_scratch]
    once = jnp.ones((1,), jnp.int32)
    q_spec = pl.BlockSpec((1, tq, wq), lambda b, i: (b, i, 0))
    q_next_spec = pl.BlockSpec((1, tq, wq), lambda b, i: (b, jnp.minimum(i + 1, n_steps - 1), 0))
    semantics = ("parallel", "arbitrary" if lookahead else "parallel")
    return pl.pallas_call(
        functools.partial(kernel, nseg=nseg, sub=sub, lookahead=lookahead),
        grid=(bsz, n_steps),
        in_specs=[pl.BlockSpec(memory_space=pltpu.SMEM)] + [_const_spec(e.shape) for e in extra]
        + [q_spec, q_next_spec] + [whole(k) for k in ks] + [whole(v) for v in vts],
        out_specs=pl.BlockSpec((1, tq, w_out), lambda b, i: (b, i, 0)),
        out_shape=jax.ShapeDtypeStruct((bsz, nq, w_out), BF16),
        scratch_shapes=scratch,
        compiler_params=pltpu.CompilerParams(dimension_semantics=semantics, vmem_limit_bytes=VMEM_LIMIT),
        name=name,
    )(once, *extra, q, q, *ks, *vts)


def _axial_angles(n_tok, rot_dim):
    t = jnp.arange(n_tok)
    row = (t // GRID_W).astype(F32)
    col = (t % GRID_W).astype(F32)
    n_axis = rot_dim // 4
    inv = ROPE_BASE ** (-jnp.arange(n_axis, dtype=F32) / n_axis)
    return jnp.concatenate([row[:, None] * inv, col[:, None] * inv], axis=-1)


def _rope_tables(n_tok, n_ctx):
    am, ad = _axial_angles(n_tok, MLA_ROPE), _axial_angles(n_tok, DIFF_DIM)
    one = lambda w: jnp.ones((n_tok, w), F32)
    zero = lambda w: jnp.zeros((n_tok, w), F32)
    cm = jnp.concatenate([one(MLA_NOPE), jnp.cos(am), jnp.cos(am), one(HEAD_PAD - MLA_NOPE - MLA_ROPE)], axis=-1)
    sam = jnp.concatenate([zero(MLA_NOPE), -jnp.sin(am), zero(HEAD_PAD - MLA_NOPE - MLA_ROPE // 2)], axis=-1)
    sbm = jnp.concatenate([zero(MLA_NOPE + MLA_ROPE // 2), jnp.sin(am), zero(HEAD_PAD - MLA_NOPE - MLA_ROPE)], axis=-1)
    cd = jnp.tile(jnp.cos(ad), (1, 4))
    sad = jnp.tile(jnp.concatenate([-jnp.sin(ad), zero(DIFF_DIM // 2)], axis=-1), (1, 2))
    sbd = jnp.tile(jnp.concatenate([zero(DIFF_DIM // 2), jnp.sin(ad)], axis=-1), (1, 2))
    lat = (cm, sam, sbm, cd, sad, sbd)
    ident = (jnp.ones((n_ctx, LANES), F32),) + (jnp.zeros((n_ctx, LANES), F32),) * 2
    return lat, ident + ident


def _inv_counts(n):
    idx = jnp.arange(n)
    cols = []
    for k in POOL_WINDOWS:
        lo, hi = k // 2, k - 1 - k // 2
        cnt = (jnp.clip(idx + hi + 1, 0, n) - jnp.clip(idx - lo, 0, n)).astype(F32)
        cols.append(jnp.broadcast_to((1.0 / cnt)[:, None], (n, POOL_GROUP)))
    return jnp.concatenate(cols, axis=-1)


def _layer_weights(w_in, w_out, pool_w, mla_w_uq, mla_w_ukv):
    d = w_in.shape[0]
    s1, s2 = POOL_WIDTH, POOL_WIDTH + MLA_Q_RANK + MLA_KV_RANK + MLA_ROPE
    kr_pad = jnp.zeros((d, HEAD_PAD), F32).at[:, MLA_NOPE:MLA_NOPE + MLA_ROPE].set(w_in[:, s2 - MLA_ROPE:s2])
    s3 = s2 + 2 * DIFF_HEADS * 2 * DIFF_DIM
    wall = jnp.concatenate([w_in[:, :s2 - MLA_ROPE], kr_pad, w_in[:, s2:s3]], axis=-1).astype(BF16)
    wdvt = w_in[:, s3:].T.astype(BF16)
    qd = MLA_NOPE + MLA_ROPE
    wuq = jnp.pad(mla_w_uq.reshape(MLA_Q_RANK, MLA_HEADS, qd), ((0, 0), (0, 0), (0, HEAD_PAD - qd)))
    wuq = wuq.reshape(MLA_Q_RANK, MLA_HEADS * HEAD_PAD).astype(BF16)
    ukv = mla_w_ukv.reshape(MLA_KV_RANK, MLA_HEADS, MLA_NOPE + MLA_V)
    wk = jnp.pad(ukv[:, :, :MLA_NOPE], ((0, 0), (0, 0), (0, HEAD_PAD - MLA_NOPE)))
    wk = wk.reshape(MLA_KV_RANK, MLA_HEADS * HEAD_PAD).astype(BF16)
    wvt = ukv[:, :, MLA_NOPE:].reshape(MLA_KV_RANK, MLA_HEADS * MLA_V).T.astype(BF16)
    bd = jax.scipy.linalg.block_diag(*[pool_w[g] for g in range(len(POOL_WINDOWS))]).astype(BF16)
    return wall, wdvt, wuq, wk, wvt, bd, w_out.astype(BF16)


def kernel(x, c, ctx, c_ctx, w_mod, b_mod, ffn1_norm, ffn1_w1, ffn1_w3, ffn1_w2, mix_norm, w_in, w_out, pool_w, pool_scale, mla_q_norm, mla_w_uq, mla_kv_norm, mla_w_ukv, diff_lambda, diff_subln, ffn2_norm, ffn2_w1, ffn2_w3, ffn2_w2, final_norm):
    bsz, seq, d = x.shape
    n_ctx = ctx.shape[1]
    depth = w_mod.shape[0]
    assert bsz + 1 <= MOD_ROWS and seq % GRID_W == 0
    tm = 512
    ctx_row = bsz

    cc = jnp.zeros((MOD_ROWS, d), F32).at[:bsz].set(c).at[ctx_row].set(c_ctx)
    mods_all = _modulation(cc, w_mod, b_mod).reshape(depth, MOD_ROWS, N_MOD, d)

    tab_lat, tab_ctx = _rope_tables(seq, n_ctx)
    icg, icx = _inv_counts(GRID_W), _inv_counts(n_ctx)
    bf = lambda w: w.astype(BF16)
    mla_args = dict(w_out=MLA_HEADS * MLA_V, maps_per_head=1)
    dif_args = dict(w_out=DIFF_HEADS * DIFF_V, maps_per_head=2)

    h, hc = x, ctx
    for i in range(depth):
        last = i == depth - 1
        mods = mods_all[i]
        lam_init = 0.8 - 0.6 * math.exp(-0.3 * i)
        wall, wdvt, wuq, wk, wvt, bd, wo = _layer_weights(w_in[i], w_out[i], pool_w[i], mla_w_uq[i], mla_w_ukv[i])
        f1 = (ffn1_norm[i], bf(ffn1_w1[i]), bf(ffn1_w3[i]), bf(ffn1_w2[i]))
        f2 = (ffn2_norm[i], bf(ffn2_w1[i]), bf(ffn2_w3[i]), bf(ffn2_w2[i]))

        h = _ffn(h, mods, None, *f1, k0=0, tm=tm)
        hc = _ffn(hc, mods, ctx_row, *f1, k0=0, tm=n_ctx)

        pw = (mix_norm[i], wall, wdvt, mla_q_norm[i], mla_kv_norm[i], wuq, wk, wvt)
        u, qm, km, vm, qd, kd, vd = _proj(h, mods, None, *pw, tab_lat, tm=2 * tm)
        uc, qmc, kmc, vmc, qdc, kdc, vdc = _proj(hc, mods, ctx_row, *pw, tab_ctx, tm=n_ctx)

        a, ac = _pool(u, uc, icg, icx, bd, pool_scale[i])
        dif = functools.partial(_diff_attn_kernel, lam_init=lam_init)
        dextra = (diff_lambda[i], diff_subln[i].reshape(1, DIFF_V))
        b = _attention(_mla_attn_kernel, "mla_attn", qm, (km, kmc), (vm, vmc), (),
                       tq=2048, sub=512, **mla_args)
        cdiff = _attention(dif, "diff_attn", qd, (kd, kdc), (vd, vdc), dextra,
                           tq=1024, sub=256, **dif_args)
        h = _ffn(h, mods, None, *f2, k0=6, tm=tm, mix=(a, b, cdiff, wo),
                 final_g=final_norm if last else None)
        if not last:
            bc = _attention(_mla_attn_kernel, "mla_attn_ctx", qmc, (kmc,), (vmc,), (),
                            tq=n_ctx, sub=n_ctx, **mla_args)
            cc_ = _attention(dif, "diff_attn_ctx", qdc, (kdc,), (vdc,), dextra,
                             tq=n_ctx, sub=n_ctx, **dif_args)
            hc = _ffn(hc, mods, ctx_row, *f2, k0=6, tm=n_ctx, mix=(ac, bc, cc_, wo))
    return h
```

```python
import functools
import math

import jax
import jax.numpy as jnp
from jax import lax
from jax.experimental import pallas as pl
from jax.experimental.pallas import tpu as pltpu

F32 = jnp.float32
BF16 = jnp.bfloat16

GRID_W = 64
N_MOD = 9
POOL_WINDOWS = (2, 4, 8, 16)
POOL_GROUP = 64
POOL_WIDTH = POOL_GROUP * len(POOL_WINDOWS)
POOL_HALF = max(POOL_WINDOWS) // 2
MLA_HEADS = 4
MLA_NOPE = 64
MLA_ROPE = 32
MLA_V = 64
MLA_Q_RANK = 384
MLA_KV_RANK = 256
DIFF_HEADS = 4
DIFF_DIM = 64
DIFF_V = 2 * DIFF_DIM
ROPE_BASE = 10000.0
EPS = 1e-6

LANES = 128
HEAD_PAD = LANES
VT_CHUNK = 256
SUM_ROWS = 16
MLA_VT_ROWS = MLA_V + SUM_ROWS
MOD_ROWS = 24
VMEM_LIMIT = 56 * 1024 * 1024

C_POOL = 0
C_CQ = C_POOL + POOL_WIDTH
C_CKV = C_CQ + MLA_Q_RANK
C_KR = C_CKV + MLA_KV_RANK
C_DQ = C_KR + HEAD_PAD
C_DK = C_DQ + DIFF_HEADS * 2 * DIFF_DIM
C_DV = C_DK + DIFF_HEADS * 2 * DIFF_DIM

LOG2E = math.log2(math.e)
MLA_QSCALE = (MLA_NOPE + MLA_ROPE) ** -0.5 * LOG2E
DIFF_QSCALE = DIFF_DIM ** -0.5 * LOG2E


def _dot(a, b):
    return jnp.dot(a, b, preferred_element_type=F32)


def _dot_nt(a, b):
    return lax.dot_general(a, b, (((1,), (1,)), ((), ())), preferred_element_type=F32)


def _rms(x, g):
    return x * lax.rsqrt(jnp.mean(x * x, axis=-1, keepdims=True) + EPS) * g


def _silu(a):
    return a * (1.0 / (1.0 + jnp.exp(-a)))


def _const_spec(shape):
    return pl.BlockSpec(shape, lambda *_: (0,) * len(shape), pipeline_mode=pl.Buffered(1))


def _params(n_axes):
    return pltpu.CompilerParams(dimension_semantics=("parallel",) * n_axes, vmem_limit_bytes=VMEM_LIMIT)


def _mod_kernel(cc_ref, w_ref, b_ref, o_ref):
    sc = _silu(cc_ref[...])
    o_ref[0] = jnp.dot(sc, w_ref[0], preferred_element_type=F32, precision=lax.Precision.HIGHEST) + b_ref[0]


def _modulation(cc, w_mod, b_mod):
    depth, d, width = w_mod.shape
    bn = 9 * LANES
    return pl.pallas_call(
        _mod_kernel,
        grid=(depth, width // bn),
        in_specs=[
            pl.BlockSpec((MOD_ROWS, d), lambda l, j: (0, 0)),
            pl.BlockSpec((1, d, bn), lambda l, j: (l, 0, j)),
            pl.BlockSpec((1, 1, bn), lambda l, j: (l, 0, j)),
        ],
        out_specs=pl.BlockSpec((1, MOD_ROWS, bn), lambda l, j: (l, 0, j)),
        out_shape=jax.ShapeDtypeStruct((depth, MOD_ROWS, width), F32),
        compiler_params=_params(2),
        name="modulation",
    )(cc, w_mod, b_mod.reshape(depth, 1, width))


MXU_COLS = 256
FFN_CHUNK = MXU_COLS


def _ffn_kernel(*refs, pre, final, k0):
    it = iter(refs)
    x_ref, m_ref = next(it), next(it)
    if pre:
        a_ref, b_ref, c_ref, wo_ref = next(it), next(it), next(it), next(it)
    g_ref, w1_ref, w3_ref, w2_ref = next(it), next(it), next(it), next(it)
    if final:
        fg_ref = next(it)
    o_ref, hmid_ref = next(it), next(it)

    x = x_ref[0]
    if pre:
        mix = jnp.concatenate([a_ref[0], b_ref[0], c_ref[0]], axis=-1)
        x = x + m_ref[0, 5:6, :] * _dot(mix, wo_ref[...])
    xm = _rms(x, g_ref[...]) * (1.0 + m_ref[0, k0 + 1:k0 + 2, :]) + m_ref[0, k0:k0 + 1, :]
    xb = xm.astype(BF16)
    dff = w1_ref.shape[1]
    for lo in range(0, dff, FFN_CHUNK):
        cols = slice(lo, min(lo + FFN_CHUNK, dff))
        a = _dot(xb, w1_ref[:, cols])
        b = _dot(xb, w3_ref[:, cols])
        hmid_ref[:, cols] = (_silu(a) * b).astype(BF16)
    y = _dot(hmid_ref[...], w2_ref[...])
    out = x + (0.5 * m_ref[0, k0 + 2:k0 + 3, :]) * y
    if final:
        out = _rms(out, fg_ref[...])
    o_ref[0] = out


def _ffn(x, mods, mod_row, g, w1, w3, w2, *, k0, tm, mix=None, final_g=None):
    bsz, n, d = x.shape
    dff = w1.shape[1]
    pre, final = mix is not None, final_g is not None
    tok = lambda w: pl.BlockSpec((1, tm, w), lambda b, i: (b, i, 0))
    mod_spec = pl.BlockSpec((1, N_MOD, d), (lambda b, i: (b, 0, 0)) if mod_row is None else (lambda b, i: (mod_row, 0, 0)))
    args, specs = [x, mods], [tok(d), mod_spec]
    if pre:
        a, bb, c, wo = mix
        args += [a, bb, c, wo]
        specs += [tok(a.shape[-1]), tok(bb.shape[-1]), tok(c.shape[-1]), _const_spec(wo.shape)]
    args += [g.reshape(1, d), w1, w3, w2]
    specs += [_const_spec((1, d)), _const_spec(w1.shape), _const_spec(w3.shape), _const_spec(w2.shape)]
    if final:
        args.append(final_g.reshape(1, d))
        specs.append(_const_spec((1, d)))
    return pl.pallas_call(
        functools.partial(_ffn_kernel, pre=pre, final=final, k0=k0),
        grid=(bsz, n // tm),
        in_specs=specs,
        out_specs=tok(d),
        out_shape=jax.ShapeDtypeStruct(x.shape, F32),
        scratch_shapes=[pltpu.VMEM((tm, dff), BF16)],
        compiler_params=_params(2),
        name="ffn_mix" if pre else "ffn",
    )(*args)


def _rope(x, c, sa, sb, shift):
    return x * c + pltpu.roll(x, LANES - shift, 1) * sa + pltpu.roll(x, shift, 1) * sb


def _proj_kernel(x_ref, m_ref, g_ref, wall_ref, wdvt_ref, qn_ref, kvn_ref, wuq_ref, wk_ref, wvt_ref,
                 cm_ref, sam_ref, sbm_ref, cd_ref, sad_ref, sbd_ref,
                 u_ref, qm_ref, km_ref, vmt_ref, qd_ref, kd_ref, vdt_ref):
    x = x_ref[0]
    n = _rms(x, g_ref[...]) * (1.0 + m_ref[0, 4:5, :]) + m_ref[0, 3:4, :]
    nb = n.astype(BF16)
    proj = lambda lo, hi: _dot(nb, wall_ref[:, lo:hi])

    pm = proj(C_CQ, C_DQ)
    cqn = _rms(pm[:, :C_CKV - C_CQ], qn_ref[...]).astype(BF16)
    ckvn = _rms(pm[:, C_CKV - C_CQ:C_KR - C_CQ], kvn_ref[...]).astype(BF16)
    q = _dot(cqn, wuq_ref[...])
    k = _dot(ckvn, wk_ref[...])
    vt = _dot_nt(wvt_ref[...], ckvn).astype(BF16)
    ones = jnp.ones((SUM_ROWS, VT_CHUNK), BF16)
    for j in range(vmt_ref.shape[1]):
        cols = slice(j * VT_CHUNK, (j + 1) * VT_CHUNK)
        for h in range(MLA_HEADS):
            vmt_ref[0, j, h * MLA_VT_ROWS:h * MLA_VT_ROWS + MLA_V, :] = vt[h * MLA_V:(h + 1) * MLA_V, cols]
            vmt_ref[0, j, h * MLA_VT_ROWS + MLA_V:(h + 1) * MLA_VT_ROWS, :] = ones
    cm, sam, sbm = cm_ref[...], sam_ref[...], sbm_ref[...]
    half_m = MLA_ROPE // 2
    kr = _rope(pm[:, C_KR - C_CQ:], cm, sam, sbm, half_m)
    for h in range(MLA_HEADS):
        sl = slice(h * HEAD_PAD, (h + 1) * HEAD_PAD)
        qm_ref[0, :, sl] = (_rope(q[:, sl], cm, sam, sbm, half_m) * MLA_QSCALE).astype(BF16)
        km_ref[0, :, sl] = (k[:, sl] + kr).astype(BF16)

    cd, sad, sbd = cd_ref[...], sad_ref[...], sbd_ref[...]
    half_d = DIFF_DIM // 2
    for col0, out_ref, scale in ((C_DQ, qd_ref, DIFF_QSCALE), (C_DK, kd_ref, None)):
        for h0 in range(0, DIFF_HEADS, 2):
            pp = proj(col0 + h0 * HEAD_PAD, col0 + (h0 + 2) * HEAD_PAD)
            for h in (h0, h0 + 1):
                r = _rope(pp[:, (h - h0) * HEAD_PAD:(h - h0 + 1) * HEAD_PAD], cd, sad, sbd, half_d)
                out_ref[0, :, h * HEAD_PAD:(h + 1) * HEAD_PAD] = (r if scale is None else r * scale).astype(BF16)
    u_ref[0] = proj(C_POOL, C_CQ)
    vdt = _dot_nt(wdvt_ref[...], nb).astype(BF16)
    for j in range(vdt_ref.shape[1]):
        vdt_ref[0, j] = vdt[:, j * VT_CHUNK:(j + 1) * VT_CHUNK]


def _proj(x, mods, mod_row, g, wall, wdvt, qn, kvn, wuq, wk, wvt, tables, *, tm):
    bsz, n, d = x.shape
    tok = lambda w: pl.BlockSpec((1, tm, w), lambda b, i: (b, i, 0))
    vt = lambda r: pl.BlockSpec((1, tm // VT_CHUNK, r, VT_CHUNK), lambda b, i: (b, i, 0, 0))
    mod_spec = pl.BlockSpec((1, N_MOD, d), (lambda b, i: (b, 0, 0)) if mod_row is None else (lambda b, i: (mod_row, 0, 0)))
    tab = pl.BlockSpec((tm, LANES), lambda b, i: (i, 0))
    wm, wd = MLA_HEADS * HEAD_PAD, DIFF_HEADS * HEAD_PAD
    rm, rd = MLA_HEADS * MLA_VT_ROWS, DIFF_HEADS * DIFF_V
    tok_sds = lambda w, dt: jax.ShapeDtypeStruct((bsz, n, w), dt)
    vt_sds = lambda r: jax.ShapeDtypeStruct((bsz, n // VT_CHUNK, r, VT_CHUNK), BF16)
    return pl.pallas_call(
        _proj_kernel,
        grid=(bsz, n // tm),
        in_specs=[tok(d), mod_spec, _const_spec((1, d)), _const_spec(wall.shape), _const_spec(wdvt.shape),
                  _const_spec((1, MLA_Q_RANK)), _const_spec((1, MLA_KV_RANK)),
                  _const_spec(wuq.shape), _const_spec(wk.shape), _const_spec(wvt.shape)] + [tab] * 6,
        out_specs=[tok(POOL_WIDTH), tok(wm), tok(wm), vt(rm), tok(wd), tok(wd), vt(rd)],
        out_shape=[tok_sds(POOL_WIDTH, F32), tok_sds(wm, BF16), tok_sds(wm, BF16), vt_sds(rm),
                   tok_sds(wd, BF16), tok_sds(wd, BF16), vt_sds(rd)],
        compiler_params=_params(2),
        name="mix_in",
    )(x, mods, g.reshape(1, d), wall, wdvt, qn.reshape(1, -1), kvn.reshape(1, -1), wuq, wk, wvt, *tables)


def _nested_window_sums(load, lane):
    a2 = load(-1) + load(0)
    a4 = a2 + load(-2) + load(1)
    a8 = a4 + load(-4) + load(-3) + load(2) + load(3)
    a16 = a8 + load(-8) + load(-7) + load(-6) + load(-5) + load(4) + load(5) + load(6) + load(7)
    return jnp.where(lane < POOL_GROUP, a2, jnp.where(lane < 2 * POOL_GROUP, a4, jnp.where(lane < 3 * POOL_GROUP, a8, a16)))


def _window_sums_1d(slab, n, lane):
    size = slab.shape[0]
    ahead = lambda x, k: pltpu.roll(x, size - k, 0)
    behind = lambda x, k: pltpu.roll(x, k, 0)[POOL_HALF:POOL_HALF + n]
    p2 = slab + ahead(slab, 1)
    p4 = p2 + ahead(p2, 2)
    p8 = p4 + ahead(p4, 4)
    p16 = p8 + ahead(p8, 8)
    return jnp.where(lane < POOL_GROUP, behind(p2, 1),
                     jnp.where(lane < 2 * POOL_GROUP, behind(p4, 2),
                               jnp.where(lane < 3 * POOL_GROUP, behind(p8, 4), p16[:n])))


def _pool_kernel(ul_ref, uc_ref, icg_ref, icx_ref, bd_ref, ps_ref, al_ref, ac_ref,
                 z_ref, y_ref, dl_ref, zc_ref, *, rows):
    gw, pad = GRID_W, POOL_HALF
    stride = gw + 2 * pad
    n_ctx = uc_ref.shape[1]
    lane = lax.broadcasted_iota(jnp.int32, (gw, POOL_WIDTH), 1)
    zero_rows = jnp.zeros((pad * gw, POOL_WIDTH), F32)
    zero_pad = jnp.zeros((pad, POOL_WIDTH), F32)

    z_ref[0:pad * gw, :] = zero_rows
    z_ref[pad * gw:(rows + pad) * gw, :] = ul_ref[0]
    z_ref[(rows + pad) * gw:(rows + 2 * pad) * gw, :] = zero_rows

    def row_pass(r, carry):
        base = pl.multiple_of((r + pad) * gw, 8)
        s = _nested_window_sums(lambda d: z_ref[pl.ds(base + d * gw, gw), :], lane)
        yb = pl.multiple_of(r * stride, 8)
        y_ref[pl.ds(yb, pad), :] = zero_pad
        y_ref[pl.ds(yb + pad, gw), :] = s * icg_ref[pl.ds(r, 1), :]
        y_ref[pl.ds(yb + pad + gw, pad), :] = zero_pad
        return carry

    lax.fori_loop(0, rows, row_pass, 0)

    def col_pass(r, carry):
        slab = y_ref[pl.ds(pl.multiple_of(r * stride, 8), stride), :]
        s = _window_sums_1d(slab, gw, lane)
        tok = pl.ds(pl.multiple_of(r * gw, 8), gw)
        dl_ref[tok, :] = (s * icg_ref[...] - ul_ref[0, tok, :]).astype(BF16)
        return carry

    lax.fori_loop(0, rows, col_pass, 0)
    al_ref[0] = (_dot(dl_ref[...], bd_ref[...]) * ps_ref[...]).astype(BF16)

    uc = uc_ref[0]
    zc_ref[0:pad, :] = zero_pad
    zc_ref[pad:pad + n_ctx, :] = uc
    zc_ref[pad + n_ctx:pad + n_ctx + pad, :] = zero_pad
    lane_c = lax.broadcasted_iota(jnp.int32, (n_ctx, POOL_WIDTH), 1)
    sc = _window_sums_1d(zc_ref[...], n_ctx, lane_c)
    dc = (sc * icx_ref[...] - uc).astype(BF16)
    ac_ref[0] = (_dot(dc, bd_ref[...]) * ps_ref[...]).astype(BF16)


def _pool(u_lat, u_ctx, icg, icx, bd, ps):
    bsz, s, w = u_lat.shape
    n_ctx = u_ctx.shape[1]
    rows = s // GRID_W
    stride = GRID_W + 2 * POOL_HALF
    return pl.pallas_call(
        functools.partial(_pool_kernel, rows=rows),
        grid=(bsz,),
        in_specs=[pl.BlockSpec((1, s, w), lambda b: (b, 0, 0)), pl.BlockSpec((1, n_ctx, w), lambda b: (b, 0, 0)),
                  _const_spec(icg.shape), _const_spec(icx.shape), _const_spec(bd.shape), _const_spec((1, w))],
        out_specs=[pl.BlockSpec((1, s, w), lambda b: (b, 0, 0)), pl.BlockSpec((1, n_ctx, w), lambda b: (b, 0, 0))],
        out_shape=[jax.ShapeDtypeStruct((bsz, s, w), BF16), jax.ShapeDtypeStruct((bsz, n_ctx, w), BF16)],
        scratch_shapes=[pltpu.VMEM(((rows + 2 * POOL_HALF) * GRID_W, w), F32), pltpu.VMEM((rows * stride, w), F32),
                        pltpu.VMEM((s, w), BF16), pltpu.VMEM((n_ctx + 2 * POOL_HALF, w), F32)],
        compiler_params=_params(1),
        name="pool",
    )(u_lat, u_ctx, icg, icx, bd, ps.reshape(1, w))


def _attn_units(once_ref, heads, vrows, prepare, qt_of, k_refs, vt_refs, s_refs, mx_refs, l_refs, acc_refs, finish,
                first=None):
    n_units = len(heads)
    assert first is None or n_units % 2 == 0
    heads = list(heads) + list(heads[:2])
    n = s_refs[0].shape[1]
    offs, off = [], 0
    for k_ref in k_refs:
        offs.append(off)
        off += k_ref.shape[1]

    def loop_body(fns):
        def whole(t, carry):
            for k_ref, vt_ref, o in zip(k_refs, vt_refs, offs):
                n_chunks, kc = vt_ref.shape[1], vt_ref.shape[3]
                for c in range(n_chunks):
                    for fn in fns["chunk"]:
                        fn(k_ref, vt_ref, c, slice(c * kc, (c + 1) * kc), slice(o + c * kc, o + (c + 1) * kc))
            for fn in fns["once"]:
                fn()
            return carry

        lax.fori_loop(0, once_ref[0], whole, 0)

    def scores(u):
        slot = u % 2
        ksl = slice(heads[u] * HEAD_PAD, (heads[u] + 1) * HEAD_PAD)

        def fn(k_ref, vt_ref, c, krows, srows):
            st = _dot(k_ref[0, krows, ksl], qt_of(u))
            s_refs[slot][srows, :] = st
            mx_refs[slot][...] = jnp.maximum(mx_refs[slot][...], jnp.max(st.reshape(-1, 8, n), axis=0))

        return fn

    def probs(u, m):
        slot = u % 2
        vsl = slice(heads[u] * vrows, (heads[u] + 1) * vrows)

        def fn(k_ref, vt_ref, c, krows, srows):
            p = jnp.exp2(s_refs[slot][srows, :] - m)
            if l_refs is not None:
                l_refs[slot][...] += jnp.sum(p.reshape(-1, 8, n), axis=0)
            acc_refs[slot][...] += _dot(vt_ref[0, c, vsl, :], p.astype(BF16))

        return fn

    def epilogue(u):
        slot = u % 2
        return lambda: finish(u, acc_refs[slot][...],
                              None if l_refs is None else jnp.sum(l_refs[slot][...], axis=0, keepdims=True))

    neg_inf = jnp.full(mx_refs[0].shape, -jnp.inf, F32)
    last_unit = n_units if first is None else n_units + 2

    def fill():
        mx_refs[0][...] = neg_inf
        prepare(0)
        loop_body({"chunk": [scores(0)], "once": [lambda: prepare(1)] if n_units > 1 else []})

    if first is None:
        fill()
    else:
        pl.when(first)(fill)
    for u in range(n_units):
        slot = u % 2
        m = jnp.max(mx_refs[slot][...], axis=0, keepdims=True)
        if l_refs is not None:
            l_refs[slot][...] = jnp.zeros(l_refs[slot].shape, F32)
        acc_refs[slot][...] = jnp.zeros(acc_refs[slot].shape, F32)
        fns = {"chunk": [probs(u, m)], "once": [epilogue(u - 1)] if u > 0 else []}
        if u + 2 < last_unit:
            fns["once"].append(functools.partial(prepare, u + 2))
        if u + 1 < min(last_unit, n_units + 1):
            mx_refs[1 - slot][...] = neg_inf
            fns["chunk"].insert(0, scores(u + 1))
        loop_body(fns)
    epilogue(n_units - 1)()


def _first_step(lookahead):
    return (pl.program_id(1) == 0) if lookahead else None


def _mla_attn_kernel(once_ref, *refs, nseg, sub, lookahead):
    q_ref, qn_ref, k_refs, vt_refs = refs[0], refs[1], refs[2:2 + nseg], refs[2 + nseg:2 + 2 * nseg]
    o_ref, s0_ref, s1_ref, mx0_ref, mx1_ref, acc0_ref, acc1_ref, ot_ref, qt_ref = refs[2 + 2 * nseg:]
    units = [(sb, h) for sb in range(q_ref.shape[1] // sub) for h in range(MLA_HEADS)]
    n_units = len(units)

    def prepare(u):
        sb, h = units[u % n_units]
        qh = (q_ref if u < n_units else qn_ref)[0, sb * sub:(sb + 1) * sub, h * HEAD_PAD:(h + 1) * HEAD_PAD]
        qt_ref[u % n_units] = qh.astype(F32).T.astype(BF16)

    def finish(u, acc, l):
        sb, h = units[u]
        ot_ref[(h % 2) * MLA_V:(h % 2 + 1) * MLA_V, :] = acc[:MLA_V] * (1.0 / acc[MLA_V:MLA_V + 1])
        if h % 2 == 1:
            o_ref[0, sb * sub:(sb + 1) * sub, (h - 1) * MLA_V:(h + 1) * MLA_V] = ot_ref[...].T.astype(BF16)

    _attn_units(once_ref, [h for _, h in units], MLA_VT_ROWS, prepare, lambda u: qt_ref[u % n_units],
                k_refs, vt_refs, (s0_ref, s1_ref), (mx0_ref, mx1_ref), None, (acc0_ref, acc1_ref), finish,
                _first_step(lookahead))


def _diff_attn_kernel(once_ref, *refs, nseg, sub, lookahead, lam_init):
    dl_ref, sub_ref, q_ref, qn_ref = refs[0], refs[1], refs[2], refs[3]
    k_refs, vt_refs = refs[4:4 + nseg], refs[4 + nseg:4 + 2 * nseg]
    o_ref, s0_ref, s1_ref, mx0_ref, mx1_ref, acc0_ref, acc1_ref, l0_ref, l1_ref, qu_ref = refs[4 + 2 * nseg:]
    units = [(sb, h) for sb in range(q_ref.shape[1] // sub) for h in range(DIFF_HEADS)]
    n_units = len(units)
    dl = dl_ref[...]
    lam = (jnp.exp(jnp.sum(dl[0:1] * dl[1:2], axis=-1, keepdims=True))
           - jnp.exp(jnp.sum(dl[2:3] * dl[3:4], axis=-1, keepdims=True)) + lam_init)
    dim = lax.broadcasted_iota(jnp.int32, (HEAD_PAD, sub), 0)

    def prepare(u):
        sb, h = units[u % n_units]
        qh = (q_ref if u < n_units else qn_ref)[0, sb * sub:(sb + 1) * sub, h * HEAD_PAD:(h + 1) * HEAD_PAD]
        qt = qh.astype(F32).T
        qu_ref[u % n_units, :, :sub] = jnp.where(dim < DIFF_DIM, qt, 0.0).astype(BF16)
        qu_ref[u % n_units, :, sub:] = jnp.where(dim >= DIFF_DIM, qt, 0.0).astype(BF16)

    def finish(u, acc, l):
        sb, h = units[u]
        o = acc[:, :sub] * (1.0 / l[:, :sub]) - acc[:, sub:] * (lam / l[:, sub:])
        on = o * lax.rsqrt(jnp.mean(o * o, axis=0, keepdims=True) + EPS)
        o_ref[0, sb * sub:(sb + 1) * sub, h * DIFF_V:(h + 1) * DIFF_V] = (
            on.T * sub_ref[...] * (1.0 - lam_init)).astype(BF16)

    _attn_units(once_ref, [h for _, h in units], DIFF_V, prepare, lambda u: qu_ref[u % n_units],
                k_refs, vt_refs, (s0_ref, s1_ref), (mx0_ref, mx1_ref), (l0_ref, l1_ref), (acc0_ref, acc1_ref),
                finish, _first_step(lookahead))


def _attention(kernel, name, q, ks, vts, extra, *, tq, sub, w_out, maps_per_head):
    bsz, nq, wq = q.shape
    nseg = len(ks)
    n_steps = nq // tq
    lookahead = n_steps > 1
    n_heads = wq // HEAD_PAD
    whole = lambda a: pl.BlockSpec((1,) + a.shape[1:], lambda b, i: (b,) + (0,) * (a.ndim - 1))
    vrows = vts[0].shape[2] // n_heads
    n_keys = sum(k.shape[1] for k in ks)
    n = sub * maps_per_head
    scratch = [pltpu.VMEM((n_keys, n), F32)] * 2 + [pltpu.VMEM((8, n), F32)] * 2 + [pltpu.VMEM((vrows, n), F32)] * 2
    qt_scratch = pltpu.VMEM((n_heads * tq // sub, HEAD_PAD, n), BF16)
    if maps_per_head > 1:
        scratch += [pltpu.VMEM((8, n), F32)] * 2 + [qt_scratch]
    else:
        scratch += [pltpu.VMEM((LANES, n), F32), qt_scratch]
    once = jnp.ones((1,), jnp.int32)
    q_spec = pl.BlockSpec((1, tq, wq), lambda b, i: (b, i, 0))
    q_next_spec = pl.BlockSpec((1, tq, wq), lambda b, i: (b, jnp.minimum(i + 1, n_steps - 1), 0))
    semantics = ("parallel", "arbitrary" if lookahead else "parallel")
    return pl.pallas_call(
        functools.partial(kernel, nseg=nseg, sub=sub, lookahead=lookahead),
        grid=(bsz, n_steps),
        in_specs=[pl.BlockSpec(memory_space=pltpu.SMEM)] + [_const_spec(e.shape) for e in extra]
        + [q_spec, q_next_spec] + [whole(k) for k in ks] + [whole(v) for v in vts],
        out_specs=pl.BlockSpec((1, tq, w_out), lambda b, i: (b, i, 0)),
        out_shape=jax.ShapeDtypeStruct((bsz, nq, w_out), BF16),
        scratch_shapes=scratch,
        compiler_params=pltpu.CompilerParams(dimension_semantics=semantics, vmem_limit_bytes=VMEM_LIMIT),
        name=name,
    )(once, *extra, q, q, *ks, *vts)


def _axial_angles(n_tok, rot_dim):
    t = jnp.arange(n_tok)
    row = (t // GRID_W).astype(F32)
    col = (t % GRID_W).astype(F32)
    n_axis = rot_dim // 4
    inv = ROPE_BASE ** (-jnp.arange(n_axis, dtype=F32) / n_axis)
    return jnp.concatenate([row[:, None] * inv, col[:, None] * inv], axis=-1)


def _rope_tables(n_tok, n_ctx):
    am, ad = _axial_angles(n_tok, MLA_ROPE), _axial_angles(n_tok, DIFF_DIM)
    one = lambda w: jnp.ones((n_tok, w), F32)
    zero = lambda w: jnp.zeros((n_tok, w), F32)
    cm = jnp.concatenate([one(MLA_NOPE), jnp.cos(am), jnp.cos(am), one(HEAD_PAD - MLA_NOPE - MLA_ROPE)], axis=-1)
    sam = jnp.concatenate([zero(MLA_NOPE), -jnp.sin(am), zero(HEAD_PAD - MLA_NOPE - MLA_ROPE // 2)], axis=-1)
    sbm = jnp.concatenate([zero(MLA_NOPE + MLA_ROPE // 2), jnp.sin(am), zero(HEAD_PAD - MLA_NOPE - MLA_ROPE)], axis=-1)
    cd = jnp.tile(jnp.cos(ad), (1, 4))
    sad = jnp.tile(jnp.concatenate([-jnp.sin(ad), zero(DIFF_DIM // 2)], axis=-1), (1, 2))
    sbd = jnp.tile(jnp.concatenate([zero(DIFF_DIM // 2), jnp.sin(ad)], axis=-1), (1, 2))
    lat = (cm, sam, sbm, cd, sad, sbd)
    ident = (jnp.ones((n_ctx, LANES), F32),) + (jnp.zeros((n_ctx, LANES), F32),) * 2
    return lat, ident + ident


def _inv_counts(n):
    idx = jnp.arange(n)
    cols = []
    for k in POOL_WINDOWS:
        lo, hi = k // 2, k - 1 - k // 2
        cnt = (jnp.clip(idx + hi + 1, 0, n) - jnp.clip(idx - lo, 0, n)).astype(F32)
        cols.append(jnp.broadcast_to((1.0 / cnt)[:, None], (n, POOL_GROUP)))
    return jnp.concatenate(cols, axis=-1)


def _layer_weights(w_in, w_out, pool_w, mla_w_uq, mla_w_ukv):
    d = w_in.shape[0]
    s1, s2 = POOL_WIDTH, POOL_WIDTH + MLA_Q_RANK + MLA_KV_RANK + MLA_ROPE
    kr_pad = jnp.zeros((d, HEAD_PAD), F32).at[:, MLA_NOPE:MLA_NOPE + MLA_ROPE].set(w_in[:, s2 - MLA_ROPE:s2])
    s3 = s2 + 2 * DIFF_HEADS * 2 * DIFF_DIM
    wall = jnp.concatenate([w_in[:, :s2 - MLA_ROPE], kr_pad, w_in[:, s2:s3]], axis=-1).astype(BF16)
    wdvt = w_in[:, s3:].T.astype(BF16)
    qd = MLA_NOPE + MLA_ROPE
    wuq = jnp.pad(mla_w_uq.reshape(MLA_Q_RANK, MLA_HEADS, qd), ((0, 0), (0, 0), (0, HEAD_PAD - qd)))
    wuq = wuq.reshape(MLA_Q_RANK, MLA_HEADS * HEAD_PAD).astype(BF16)
    ukv = mla_w_ukv.reshape(MLA_KV_RANK, MLA_HEADS, MLA_NOPE + MLA_V)
    wk = jnp.pad(ukv[:, :, :MLA_NOPE], ((0, 0), (0, 0), (0, HEAD_PAD - MLA_NOPE)))
    wk = wk.reshape(MLA_KV_RANK, MLA_HEADS * HEAD_PAD).astype(BF16)
    wvt = ukv[:, :, MLA_NOPE:].reshape(MLA_KV_RANK, MLA_HEADS * MLA_V).T.astype(BF16)
    bd = jax.scipy.linalg.block_diag(*[pool_w[g] for g in range(len(POOL_WINDOWS))]).astype(BF16)
    return wall, wdvt, wuq, wk, wvt, bd, w_out.astype(BF16)


def kernel(x, c, ctx, c_ctx, w_mod, b_mod, ffn1_norm, ffn1_w1, ffn1_w3, ffn1_w2, mix_norm, w_in, w_out, pool_w, pool_scale, mla_q_norm, mla_w_uq, mla_kv_norm, mla_w_ukv, diff_lambda, diff_subln, ffn2_norm, ffn2_w1, ffn2_w3, ffn2_w2, final_norm):
    bsz, seq, d = x.shape
    n_ctx = ctx.shape[1]
    depth = w_mod.shape[0]
    assert bsz + 1 <= MOD_ROWS and seq % GRID_W == 0
    tm = 1024
    ctx_row = bsz

    cc = jnp.zeros((MOD_ROWS, d), F32).at[:bsz].set(c).at[ctx_row].set(c_ctx)
    mods_all = _modulation(cc, w_mod, b_mod).reshape(depth, MOD_ROWS, N_MOD, d)

    tab_lat, tab_ctx = _rope_tables(seq, n_ctx)
    icg, icx = _inv_counts(GRID_W), _inv_counts(n_ctx)
    bf = lambda w: w.astype(BF16)
    mla_args = dict(w_out=MLA_HEADS * MLA_V, maps_per_head=1)
    dif_args = dict(w_out=DIFF_HEADS * DIFF_V, maps_per_head=2)

    h, hc = x, ctx
    for i in range(depth):
        last = i == depth - 1
        mods = mods_all[i]
        lam_init = 0.8 - 0.6 * math.exp(-0.3 * i)
        wall, wdvt, wuq, wk, wvt, bd, wo = _layer_weights(w_in[i], w_out[i], pool_w[i], mla_w_uq[i], mla_w_ukv[i])
        f1 = (ffn1_norm[i], bf(ffn1_w1[i]), bf(ffn1_w3[i]), bf(ffn1_w2[i]))
        f2 = (ffn2_norm[i], bf(ffn2_w1[i]), bf(ffn2_w3[i]), bf(ffn2_w2[i]))

        h = _ffn(h, mods, None, *f1, k0=0, tm=tm)
        hc = _ffn(hc, mods, ctx_row, *f1, k0=0, tm=n_ctx)

        pw = (mix_norm[i], wall, wdvt, mla_q_norm[i], mla_kv_norm[i], wuq, wk, wvt)
        u, qm, km, vm, qd, kd, vd = _proj(h, mods, None, *pw, tab_lat, tm=tm)
        uc, qmc, kmc, vmc, qdc, kdc, vdc = _proj(hc, mods, ctx_row, *pw, tab_ctx, tm=n_ctx)

        a, ac = _pool(u, uc, icg, icx, bd, pool_scale[i])
        dif = functools.partial(_diff_attn_kernel, lam_init=lam_init)
        dextra = (diff_lambda[i], diff_subln[i].reshape(1, DIFF_V))
        b = _attention(_mla_attn_kernel, "mla_attn", qm, (km, kmc), (vm, vmc), (),
                       tq=2048, sub=512, **mla_args)
        cdiff = _attention(dif, "diff_attn", qd, (kd, kdc), (vd, vdc), dextra,
                           tq=1024, sub=256, **dif_args)
        h = _ffn(h, mods, None, *f2, k0=6, tm=tm, mix=(a, b, cdiff, wo),
                 final_g=final_norm if last else None)
        if not last:
            bc = _attention(_mla_attn_kernel, "mla_attn_ctx", qmc, (kmc,), (vmc,), (),
                            tq=n_ctx, sub=n_ctx, **mla_args)
            cc_ = _attention(dif, "diff_attn_ctx", qdc, (kdc,), (vdc,), dextra,
                             tq=n_ctx, sub=n_ctx, **dif_args)
            hc = _ffn(hc, mods, ctx_row, *f2, k0=6, tm=n_ctx, mix=(ac, bc, cc_, wo))
    return h
```

```python
import functools
import math

import jax
import jax.numpy as jnp
from jax import lax
from jax.experimental import pallas as pl
from jax.experimental.pallas import tpu as pltpu

F32 = jnp.float32
BF16 = jnp.bfloat16

GRID_W = 64
N_MOD = 9
POOL_WINDOWS = (2, 4, 8, 16)
POOL_GROUP = 64
POOL_WIDTH = POOL_GROUP * len(POOL_WINDOWS)
POOL_HALF = max(POOL_WINDOWS) // 2
MLA_HEADS = 4
MLA_NOPE = 64
MLA_ROPE = 32
MLA_V = 64
MLA_Q_RANK = 384
MLA_KV_RANK = 256
DIFF_HEADS = 4
DIFF_DIM = 64
DIFF_V = 2 * DIFF_DIM
ROPE_BASE = 10000.0
EPS = 1e-6

LANES = 128
HEAD_PAD = LANES
VT_CHUNK = 256
SUM_ROWS = 16
MLA_VT_ROWS = MLA_V + SUM_ROWS
MOD_ROWS = 24
VMEM_LIMIT = 56 * 1024 * 1024

C_POOL = 0
C_CQ = C_POOL + POOL_WIDTH
C_CKV = C_CQ + MLA_Q_RANK
C_KR = C_CKV + MLA_KV_RANK
C_DQ = C_KR + HEAD_PAD
C_DK = C_DQ + DIFF_HEADS * 2 * DIFF_DIM
C_DV = C_DK + DIFF_HEADS * 2 * DIFF_DIM

LOG2E = math.log2(math.e)
MLA_QSCALE = (MLA_NOPE + MLA_ROPE) ** -0.5 * LOG2E
DIFF_QSCALE = DIFF_DIM ** -0.5 * LOG2E


def _dot(a, b):
    return jnp.dot(a, b, preferred_element_type=F32)


def _dot_nt(a, b):
    return lax.dot_general(a, b, (((1,), (1,)), ((), ())), preferred_element_type=F32)


def _rms(x, g):
    return x * lax.rsqrt(jnp.mean(x * x, axis=-1, keepdims=True) + EPS) * g


def _silu(a):
    return a * (1.0 / (1.0 + jnp.exp(-a)))


def _const_spec(shape):
    return pl.BlockSpec(shape, lambda *_: (0,) * len(shape), pipeline_mode=pl.Buffered(1))


def _params(n_axes):
    return pltpu.CompilerParams(dimension_semantics=("parallel",) * n_axes, vmem_limit_bytes=VMEM_LIMIT)


def _mod_kernel(cc_ref, w_ref, b_ref, o_ref):
    sc = _silu(cc_ref[...])
    o_ref[0] = jnp.dot(sc, w_ref[0], preferred_element_type=F32, precision=lax.Precision.HIGHEST) + b_ref[0]


def _modulation(cc, w_mod, b_mod):
    depth, d, width = w_mod.shape
    bn = 9 * LANES
    return pl.pallas_call(
        _mod_kernel,
        grid=(depth, width // bn),
        in_specs=[
            pl.BlockSpec((MOD_ROWS, d), lambda l, j: (0, 0)),
            pl.BlockSpec((1, d, bn), lambda l, j: (l, 0, j)),
            pl.BlockSpec((1, 1, bn), lambda l, j: (l, 0, j)),
        ],
        out_specs=pl.BlockSpec((1, MOD_ROWS, bn), lambda l, j: (l, 0, j)),
        out_shape=jax.ShapeDtypeStruct((depth, MOD_ROWS, width), F32),
        compiler_params=_params(2),
        name="modulation",
    )(cc, w_mod, b_mod.reshape(depth, 1, width))


MXU_COLS = 256
FFN_CHUNK = MXU_COLS


def _ffn_kernel(*refs, pre, final, k0):
    it = iter(refs)
    x_ref, m_ref = next(it), next(it)
    if pre:
        a_ref, b_ref, c_ref, wo_ref = next(it), next(it), next(it), next(it)
    g_ref, w1_ref, w3_ref, w2_ref = next(it), next(it), next(it), next(it)
    if final:
        fg_ref = next(it)
    o_ref, hmid_ref = next(it), next(it)

    x = x_ref[0]
    if pre:
        mix = jnp.concatenate([a_ref[0], b_ref[0], c_ref[0]], axis=-1)
        x = x + m_ref[0, 5:6, :] * _dot(mix, wo_ref[...])
    xm = _rms(x, g_ref[...]) * (1.0 + m_ref[0, k0 + 1:k0 + 2, :]) + m_ref[0, k0:k0 + 1, :]
    xb = xm.astype(BF16)
    dff = w1_ref.shape[1]
    for lo in range(0, dff, FFN_CHUNK):
        cols = slice(lo, min(lo + FFN_CHUNK, dff))
        a = _dot(xb, w1_ref[:, cols])
        b = _dot(xb, w3_ref[:, cols])
        hmid_ref[:, cols] = (_silu(a) * b).astype(BF16)
    y = _dot(hmid_ref[...], w2_ref[...])
    out = x + (0.5 * m_ref[0, k0 + 2:k0 + 3, :]) * y
    if final:
        out = _rms(out, fg_ref[...])
    o_ref[0] = out


def _ffn(x, mods, mod_row, g, w1, w3, w2, *, k0, tm, mix=None, final_g=None):
    bsz, n, d = x.shape
    dff = w1.shape[1]
    pre, final = mix is not None, final_g is not None
    tok = lambda w: pl.BlockSpec((1, tm, w), lambda b, i: (b, i, 0))
    mod_spec = pl.BlockSpec((1, N_MOD, d), (lambda b, i: (b, 0, 0)) if mod_row is None else (lambda b, i: (mod_row, 0, 0)))
    args, specs = [x, mods], [tok(d), mod_spec]
    if pre:
        a, bb, c, wo = mix
        args += [a, bb, c, wo]
        specs += [tok(a.shape[-1]), tok(bb.shape[-1]), tok(c.shape[-1]), _const_spec(wo.shape)]
    args += [g.reshape(1, d), w1, w3, w2]
    specs += [_const_spec((1, d)), _const_spec(w1.shape), _const_spec(w3.shape), _const_spec(w2.shape)]
    if final:
        args.append(final_g.reshape(1, d))
        specs.append(_const_spec((1, d)))
    return pl.pallas_call(
        functools.partial(_ffn_kernel, pre=pre, final=final, k0=k0),
        grid=(bsz, n // tm),
        in_specs=specs,
        out_specs=tok(d),
        out_shape=jax.ShapeDtypeStruct(x.shape, F32),
        scratch_shapes=[pltpu.VMEM((tm, dff), BF16)],
        compiler_params=_params(2),
        name="ffn_mix" if pre else "ffn",
    )(*args)


def _rope(x, c, sa, sb, shift):
    return x * c + pltpu.roll(x, LANES - shift, 1) * sa + pltpu.roll(x, shift, 1) * sb


def _proj_kernel(x_ref, m_ref, g_ref, wall_ref, wdvt_ref, qn_ref, kvn_ref, wuq_ref, wk_ref, wvt_ref,
                 cm_ref, sam_ref, sbm_ref, cd_ref, sad_ref, sbd_ref,
                 u_ref, qm_ref, km_ref, vmt_ref, qd_ref, kd_ref, vdt_ref):
    x = x_ref[0]
    n = _rms(x, g_ref[...]) * (1.0 + m_ref[0, 4:5, :]) + m_ref[0, 3:4, :]
    nb = n.astype(BF16)
    proj = lambda lo, hi: _dot(nb, wall_ref[:, lo:hi])

    pm = proj(C_CQ, C_DQ)
    cqn = _rms(pm[:, :C_CKV - C_CQ], qn_ref[...]).astype(BF16)
    ckvn = _rms(pm[:, C_CKV - C_CQ:C_KR - C_CQ], kvn_ref[...]).astype(BF16)
    q = _dot(cqn, wuq_ref[...])
    k = _dot(ckvn, wk_ref[...])
    vt = _dot_nt(wvt_ref[...], ckvn).astype(BF16)
    ones = jnp.ones((SUM_ROWS, VT_CHUNK), BF16)
    for j in range(vmt_ref.shape[1]):
        cols = slice(j * VT_CHUNK, (j + 1) * VT_CHUNK)
        for h in range(MLA_HEADS):
            vmt_ref[0, j, h * MLA_VT_ROWS:h * MLA_VT_ROWS + MLA_V, :] = vt[h * MLA_V:(h + 1) * MLA_V, cols]
            vmt_ref[0, j, h * MLA_VT_ROWS + MLA_V:(h + 1) * MLA_VT_ROWS, :] = ones
    cm, sam, sbm = cm_ref[...], sam_ref[...], sbm_ref[...]
    half_m = MLA_ROPE // 2
    kr = _rope(pm[:, C_KR - C_CQ:], cm, sam, sbm, half_m)
    for h in range(MLA_HEADS):
        sl = slice(h * HEAD_PAD, (h + 1) * HEAD_PAD)
        qm_ref[0, :, sl] = (_rope(q[:, sl], cm, sam, sbm, half_m) * MLA_QSCALE).astype(BF16)
        km_ref[0, :, sl] = (k[:, sl] + kr).astype(BF16)

    cd, sad, sbd = cd_ref[...], sad_ref[...], sbd_ref[...]
    half_d = DIFF_DIM // 2
    for col0, out_ref, scale in ((C_DQ, qd_ref, DIFF_QSCALE), (C_DK, kd_ref, None)):
        for h0 in range(0, DIFF_HEADS, 2):
            pp = proj(col0 + h0 * HEAD_PAD, col0 + (h0 + 2) * HEAD_PAD)
            for h in (h0, h0 + 1):
                r = _rope(pp[:, (h - h0) * HEAD_PAD:(h - h0 + 1) * HEAD_PAD], cd, sad, sbd, half_d)
                out_ref[0, :, h * HEAD_PAD:(h + 1) * HEAD_PAD] = (r if scale is None else r * scale).astype(BF16)
    u_ref[0] = proj(C_POOL, C_CQ)
    vdt = _dot_nt(wdvt_ref[...], nb).astype(BF16)
    for j in range(vdt_ref.shape[1]):
        vdt_ref[0, j] = vdt[:, j * VT_CHUNK:(j + 1) * VT_CHUNK]


def _proj(x, mods, mod_row, g, wall, wdvt, qn, kvn, wuq, wk, wvt, tables, *, tm):
    bsz, n, d = x.shape
    tok = lambda w: pl.BlockSpec((1, tm, w), lambda b, i: (b, i, 0))
    vt = lambda r: pl.BlockSpec((1, tm // VT_CHUNK, r, VT_CHUNK), lambda b, i: (b, i, 0, 0))
    mod_spec = pl.BlockSpec((1, N_MOD, d), (lambda b, i: (b, 0, 0)) if mod_row is None else (lambda b, i: (mod_row, 0, 0)))
    tab = pl.BlockSpec((tm, LANES), lambda b, i: (i, 0))
    wm, wd = MLA_HEADS * HEAD_PAD, DIFF_HEADS * HEAD_PAD
    rm, rd = MLA_HEADS * MLA_VT_ROWS, DIFF_HEADS * DIFF_V
    tok_sds = lambda w, dt: jax.ShapeDtypeStruct((bsz, n, w), dt)
    vt_sds = lambda r: jax.ShapeDtypeStruct((bsz, n // VT_CHUNK, r, VT_CHUNK), BF16)
    return pl.pallas_call(
        _proj_kernel,
        grid=(bsz, n // tm),
        in_specs=[tok(d), mod_spec, _const_spec((1, d)), _const_spec(wall.shape), _const_spec(wdvt.shape),
                  _const_spec((1, MLA_Q_RANK)), _const_spec((1, MLA_KV_RANK)),
                  _const_spec(wuq.shape), _const_spec(wk.shape), _const_spec(wvt.shape)] + [tab] * 6,
        out_specs=[tok(POOL_WIDTH), tok(wm), tok(wm), vt(rm), tok(wd), tok(wd), vt(rd)],
        out_shape=[tok_sds(POOL_WIDTH, F32), tok_sds(wm, BF16), tok_sds(wm, BF16), vt_sds(rm),
                   tok_sds(wd, BF16), tok_sds(wd, BF16), vt_sds(rd)],
        compiler_params=_params(2),
        name="mix_in",
    )(x, mods, g.reshape(1, d), wall, wdvt, qn.reshape(1, -1), kvn.reshape(1, -1), wuq, wk, wvt, *tables)


def _nested_window_sums(load, lane):
    a2 = load(-1) + load(0)
    a4 = a2 + load(-2) + load(1)
    a8 = a4 + load(-4) + load(-3) + load(2) + load(3)
    a16 = a8 + load(-8) + load(-7) + load(-6) + load(-5) + load(4) + load(5) + load(6) + load(7)
    return jnp.where(lane < POOL_GROUP, a2, jnp.where(lane < 2 * POOL_GROUP, a4, jnp.where(lane < 3 * POOL_GROUP, a8, a16)))


def _window_sums_1d(slab, n, lane):
    size = slab.shape[0]
    ahead = lambda x, k: pltpu.roll(x, size - k, 0)
    behind = lambda x, k: pltpu.roll(x, k, 0)[POOL_HALF:POOL_HALF + n]
    p2 = slab + ahead(slab, 1)
    p4 = p2 + ahead(p2, 2)
    p8 = p4 + ahead(p4, 4)
    p16 = p8 + ahead(p8, 8)
    return jnp.where(lane < POOL_GROUP, behind(p2, 1),
                     jnp.where(lane < 2 * POOL_GROUP, behind(p4, 2),
                               jnp.where(lane < 3 * POOL_GROUP, behind(p8, 4), p16[:n])))


def _pool_kernel(ul_ref, uc_ref, icg_ref, icx_ref, bd_ref, ps_ref, al_ref, ac_ref,
                 z_ref, y_ref, dl_ref, zc_ref, *, rows):
    gw, pad = GRID_W, POOL_HALF
    stride = gw + 2 * pad
    n_ctx = uc_ref.shape[1]
    lane = lax.broadcasted_iota(jnp.int32, (gw, POOL_WIDTH), 1)
    zero_rows = jnp.zeros((pad * gw, POOL_WIDTH), F32)
    zero_pad = jnp.zeros((pad, POOL_WIDTH), F32)

    z_ref[0:pad * gw, :] = zero_rows
    z_ref[pad * gw:(rows + pad) * gw, :] = ul_ref[0]
    z_ref[(rows + pad) * gw:(rows + 2 * pad) * gw, :] = zero_rows

    def row_pass(r, carry):
        base = pl.multiple_of((r + pad) * gw, 8)
        s = _nested_window_sums(lambda d: z_ref[pl.ds(base + d * gw, gw), :], lane)
        yb = pl.multiple_of(r * stride, 8)
        y_ref[pl.ds(yb, pad), :] = zero_pad
        y_ref[pl.ds(yb + pad, gw), :] = s * icg_ref[pl.ds(r, 1), :]
        y_ref[pl.ds(yb + pad + gw, pad), :] = zero_pad
        return carry

    lax.fori_loop(0, rows, row_pass, 0)

    def col_pass(r, carry):
        slab = y_ref[pl.ds(pl.multiple_of(r * stride, 8), stride), :]
        s = _window_sums_1d(slab, gw, lane)
        tok = pl.ds(pl.multiple_of(r * gw, 8), gw)
        dl_ref[tok, :] = (s * icg_ref[...] - ul_ref[0, tok, :]).astype(BF16)
        return carry

    lax.fori_loop(0, rows, col_pass, 0)
    al_ref[0] = (_dot(dl_ref[...], bd_ref[...]) * ps_ref[...]).astype(BF16)

    uc = uc_ref[0]
    zc_ref[0:pad, :] = zero_pad
    zc_ref[pad:pad + n_ctx, :] = uc
    zc_ref[pad + n_ctx:pad + n_ctx + pad, :] = zero_pad
    lane_c = lax.broadcasted_iota(jnp.int32, (n_ctx, POOL_WIDTH), 1)
    sc = _window_sums_1d(zc_ref[...], n_ctx, lane_c)
    dc = (sc * icx_ref[...] - uc).astype(BF16)
    ac_ref[0] = (_dot(dc, bd_ref[...]) * ps_ref[...]).astype(BF16)


def _pool(u_lat, u_ctx, icg, icx, bd, ps):
    bsz, s, w = u_lat.shape
    n_ctx = u_ctx.shape[1]
    rows = s // GRID_W
    stride = GRID_W + 2 * POOL_HALF
    return pl.pallas_call(
        functools.partial(_pool_kernel, rows=rows),
        grid=(bsz,),
        in_specs=[pl.BlockSpec((1, s, w), lambda b: (b, 0, 0)), pl.BlockSpec((1, n_ctx, w), lambda b: (b, 0, 0)),
                  _const_spec(icg.shape), _const_spec(icx.shape), _const_spec(bd.shape), _const_spec((1, w))],
        out_specs=[pl.BlockSpec((1, s, w), lambda b: (b, 0, 0)), pl.BlockSpec((1, n_ctx, w), lambda b: (b, 0, 0))],
        out_shape=[jax.ShapeDtypeStruct((bsz, s, w), BF16), jax.ShapeDtypeStruct((bsz, n_ctx, w), BF16)],
        scratch_shapes=[pltpu.VMEM(((rows + 2 * POOL_HALF) * GRID_W, w), F32), pltpu.VMEM((rows * stride, w), F32),
                        pltpu.VMEM((s, w), BF16), pltpu.VMEM((n_ctx + 2 * POOL_HALF, w), F32)],
        compiler_params=_params(1),
        name="pool",
    )(u_lat, u_ctx, icg, icx, bd, ps.reshape(1, w))


def _attn_units(once_ref, heads, vrows, prepare, qt_of, k_refs, vt_refs, s_refs, mx_refs, l_refs, acc_refs, finish,
                first=None):
    n_units = len(heads)
    assert first is None or n_units % 2 == 0
    heads = list(heads) + list(heads[:2])
    n = s_refs[0].shape[1]
    offs, off = [], 0
    for k_ref in k_refs:
        offs.append(off)
        off += k_ref.shape[1]

    def loop_body(fns):
        def whole(t, carry):
            for k_ref, vt_ref, o in zip(k_refs, vt_refs, offs):
                n_chunks, kc = vt_ref.shape[1], vt_ref.shape[3]
                for c in range(n_chunks):
                    for fn in fns["chunk"]:
                        fn(k_ref, vt_ref, c, slice(c * kc, (c + 1) * kc), slice(o + c * kc, o + (c + 1) * kc))
            for fn in fns["once"]:
                fn()
            return carry

        lax.fori_loop(0, once_ref[0], whole, 0)

    def scores(u):
        slot = u % 2
        ksl = slice(heads[u] * HEAD_PAD, (heads[u] + 1) * HEAD_PAD)

        def fn(k_ref, vt_ref, c, krows, srows):
            st = _dot(k_ref[0, krows, ksl], qt_of(u))
            s_refs[slot][srows, :] = st
            mx_refs[slot][...] = jnp.maximum(mx_refs[slot][...], jnp.max(st.reshape(-1, 8, n), axis=0))

        return fn

    def probs(u, m):
        slot = u % 2
        vsl = slice(heads[u] * vrows, (heads[u] + 1) * vrows)

        def fn(k_ref, vt_ref, c, krows, srows):
            p = jnp.exp2(s_refs[slot][srows, :] - m)
            if l_refs is not None:
                l_refs[slot][...] += jnp.sum(p.reshape(-1, 8, n), axis=0)
            acc_refs[slot][...] += _dot(vt_ref[0, c, vsl, :], p.astype(BF16))

        return fn

    def epilogue(u):
        slot = u % 2
        return lambda: finish(u, acc_refs[slot][...],
                              None if l_refs is None else jnp.sum(l_refs[slot][...], axis=0, keepdims=True))

    neg_inf = jnp.full(mx_refs[0].shape, -jnp.inf, F32)
    last_unit = n_units if first is None else n_units + 2

    def fill():
        mx_refs[0][...] = neg_inf
        prepare(0)
        loop_body({"chunk": [scores(0)], "once": [lambda: prepare(1)] if n_units > 1 else []})

    if first is None:
        fill()
    else:
        pl.when(first)(fill)
    for u in range(n_units):
        slot = u % 2
        m = jnp.max(mx_refs[slot][...], axis=0, keepdims=True)
        if l_refs is not None:
            l_refs[slot][...] = jnp.zeros(l_refs[slot].shape, F32)
        acc_refs[slot][...] = jnp.zeros(acc_refs[slot].shape, F32)
        fns = {"chunk": [probs(u, m)], "once": [epilogue(u - 1)] if u > 0 else []}
        if u + 2 < last_unit:
            fns["once"].append(functools.partial(prepare, u + 2))
        if u + 1 < min(last_unit, n_units + 1):
            mx_refs[1 - slot][...] = neg_inf
            fns["chunk"].insert(0, scores(u + 1))
        loop_body(fns)
    epilogue(n_units - 1)()


def _first_step(lookahead):
    return (pl.program_id(1) == 0) if lookahead else None


def _mla_attn_kernel(once_ref, *refs, nseg, sub, lookahead):
    q_ref, qn_ref, k_refs, vt_refs = refs[0], refs[1], refs[2:2 + nseg], refs[2 + nseg:2 + 2 * nseg]
    o_ref, s0_ref, s1_ref, mx0_ref, mx1_ref, acc0_ref, acc1_ref, ot_ref, qt_ref = refs[2 + 2 * nseg:]
    units = [(sb, h) for sb in range(q_ref.shape[1] // sub) for h in range(MLA_HEADS)]
    n_units = len(units)

    def prepare(u):
        sb, h = units[u % n_units]
        qh = (q_ref if u < n_units else qn_ref)[0, sb * sub:(sb + 1) * sub, h * HEAD_PAD:(h + 1) * HEAD_PAD]
        qt_ref[u % n_units] = qh.astype(F32).T.astype(BF16)

    def finish(u, acc, l):
        sb, h = units[u]
        ot_ref[(h % 2) * MLA_V:(h % 2 + 1) * MLA_V, :] = acc[:MLA_V] * (1.0 / acc[MLA_V:MLA_V + 1])
        if h % 2 == 1:
            o_ref[0, sb * sub:(sb + 1) * sub, (h - 1) * MLA_V:(h + 1) * MLA_V] = ot_ref[...].T.astype(BF16)

    _attn_units(once_ref, [h for _, h in units], MLA_VT_ROWS, prepare, lambda u: qt_ref[u % n_units],
                k_refs, vt_refs, (s0_ref, s1_ref), (mx0_ref, mx1_ref), None, (acc0_ref, acc1_ref), finish,
                _first_step(lookahead))


def _diff_attn_kernel(once_ref, *refs, nseg, sub, lookahead, lam_init):
    dl_ref, sub_ref, q_ref, qn_ref = refs[0], refs[1], refs[2], refs[3]
    k_refs, vt_refs = refs[4:4 + nseg], refs[4 + nseg:4 + 2 * nseg]
    o_ref, s0_ref, s1_ref, mx0_ref, mx1_ref, acc0_ref, acc1_ref, l0_ref, l1_ref, qu_ref = refs[4 + 2 * nseg:]
    units = [(sb, h) for sb in range(q_ref.shape[1] // sub) for h in range(DIFF_HEADS)]
    n_units = len(units)
    dl = dl_ref[...]
    lam = (jnp.exp(jnp.sum(dl[0:1] * dl[1:2], axis=-1, keepdims=True))
           - jnp.exp(jnp.sum(dl[2:3] * dl[3:4], axis=-1, keepdims=True)) + lam_init)
    dim = lax.broadcasted_iota(jnp.int32, (HEAD_PAD, sub), 0)

    def prepare(u):
        sb, h = units[u % n_units]
        qh = (q_ref if u < n_units else qn_ref)[0, sb * sub:(sb + 1) * sub, h * HEAD_PAD:(h + 1) * HEAD_PAD]
        qt = qh.astype(F32).T
        qu_ref[u % n_units, :, :sub] = jnp.where(dim < DIFF_DIM, qt, 0.0).astype(BF16)
        qu_ref[u % n_units, :, sub:] = jnp.where(dim >= DIFF_DIM, qt, 0.0).astype(BF16)

    def finish(u, acc, l):
        sb, h = units[u]
        o = acc[:, :sub] * (1.0 / l[:, :sub]) - acc[:, sub:] * (lam / l[:, sub:])
        on = o * lax.rsqrt(jnp.mean(o * o, axis=0, keepdims=True) + EPS)
        o_ref[0, sb * sub:(sb + 1) * sub, h * DIFF_V:(h + 1) * DIFF_V] = (
            on.T * sub_ref[...] * (1.0 - lam_init)).astype(BF16)

    _attn_units(once_ref, [h for _, h in units], DIFF_V, prepare, lambda u: qu_ref[u % n_units],
                k_refs, vt_refs, (s0_ref, s1_ref), (mx0_ref, mx1_ref), (l0_ref, l1_ref), (acc0_ref, acc1_ref),
                finish, _first_step(lookahead))


def _attention(kernel, name, q, ks, vts, extra, *, tq, sub, w_out, maps_per_head):
    bsz, nq, wq = q.shape
    nseg = len(ks)
    n_steps = nq // tq
    lookahead = n_steps > 1
    n_heads = wq // HEAD_PAD
    whole = lambda a: pl.BlockSpec((1,) + a.shape[1:], lambda b, i: (b,) + (0,) * (a.ndim - 1))
    vrows = vts[0].shape[2] // n_heads
    n_keys = sum(k.shape[1] for k in ks)
    n = sub * maps_per_head
    scratch = [pltpu.VMEM((n_keys, n), F32)] * 2 + [pltpu.VMEM((8, n), F32)] * 2 + [pltpu.VMEM((vrows, n), F32)] * 2
    qt_scratch = pltpu.VMEM((n_heads * tq // sub, HEAD_PAD, n), BF16)
    if maps_per_head > 1:
        scratch += [pltpu.VMEM((8, n), F32)] * 2 + [qt_scratch]
    else:
        scratch += [pltpu.VMEM((LANES, n), F32), qt_scratch]
    once = jnp.ones((1,), jnp.int32)
    q_spec = pl.BlockSpec((1, tq, wq), lambda b, i: (b, i, 0))
    q_next_spec = pl.BlockSpec((1, tq, wq), lambda b, i: (b, jnp.minimum(i + 1, n_steps - 1), 0))
    semantics = ("parallel", "arbitrary" if lookahead else "parallel")
    return pl.pallas_call(
        functools.partial(kernel, nseg=nseg, sub=sub, lookahead=lookahead),
        grid=(bsz, n_steps),
        in_specs=[pl.BlockSpec(memory_space=pltpu.SMEM)] + [_const_spec(e.shape) for e in extra]
        + [q_spec, q_next_spec] + [whole(k) for k in ks] + [whole(v) for v in vts],
        out_specs=pl.BlockSpec((1, tq, w_out), lambda b, i: (b, i, 0)),
        out_shape=jax.ShapeDtypeStruct((bsz, nq, w_out), BF16),
        scratch_shapes=scratch,
        compiler_params=pltpu.CompilerParams(dimension_semantics=semantics, vmem_limit_bytes=VMEM_LIMIT),
        name=name,
    )(once, *extra, q, q, *ks, *vts)


def _axial_angles(n_tok, rot_dim):
    t = jnp.arange(n_tok)
    row = (t // GRID_W).astype(F32)
    col = (t % GRID_W).astype(F32)
    n_axis = rot_dim // 4
    inv = ROPE_BASE ** (-jnp.arange(n_axis, dtype=F32) / n_axis)
    return jnp.concatenate([row[:, None] * inv, col[:, None] * inv], axis=-1)


def _rope_tables(n_tok, n_ctx):
    am, ad = _axial_angles(n_tok, MLA_ROPE), _axial_angles(n_tok, DIFF_DIM)
    one = lambda w: jnp.ones((n_tok, w), F32)
    zero = lambda w: jnp.zeros((n_tok, w), F32)
    cm = jnp.concatenate([one(MLA_NOPE), jnp.cos(am), jnp.cos(am), one(HEAD_PAD - MLA_NOPE - MLA_ROPE)], axis=-1)
    sam = jnp.concatenate([zero(MLA_NOPE), -jnp.sin(am), zero(HEAD_PAD - MLA_NOPE - MLA_ROPE // 2)], axis=-1)
    sbm = jnp.concatenate([zero(MLA_NOPE + MLA_ROPE // 2), jnp.sin(am), zero(HEAD_PAD - MLA_NOPE - MLA_ROPE)], axis=-1)
    cd = jnp.tile(jnp.cos(ad), (1, 4))
    sad = jnp.tile(jnp.concatenate([-jnp.sin(ad), zero(DIFF_DIM // 2)], axis=-1), (1, 2))
    sbd = jnp.tile(jnp.concatenate([zero(DIFF_DIM // 2), jnp.sin(ad)], axis=-1), (1, 2))
    lat = (cm, sam, sbm, cd, sad, sbd)
    ident = (jnp.ones((n_ctx, LANES), F32),) + (jnp.zeros((n_ctx, LANES), F32),) * 2
    return lat, ident + ident


def _inv_counts(n):
    idx = jnp.arange(n)
    cols = []
    for k in POOL_WINDOWS:
        lo, hi = k // 2, k - 1 - k // 2
        cnt = (jnp.clip(idx + hi + 1, 0, n) - jnp.clip(idx - lo, 0, n)).astype(F32)
        cols.append(jnp.broadcast_to((1.0 / cnt)[:, None], (n, POOL_GROUP)))
    return jnp.concatenate(cols, axis=-1)


def _layer_weights(w_in, w_out, pool_w, mla_w_uq, mla_w_ukv):
    d = w_in.shape[0]
    s1, s2 = POOL_WIDTH, POOL_WIDTH + MLA_Q_RANK + MLA_KV_RANK + MLA_ROPE
    kr_pad = jnp.zeros((d, HEAD_PAD), F32).at[:, MLA_NOPE:MLA_NOPE + MLA_ROPE].set(w_in[:, s2 - MLA_ROPE:s2])
    s3 = s2 + 2 * DIFF_HEADS * 2 * DIFF_DIM
    wall = jnp.concatenate([w_in[:, :s2 - MLA_ROPE], kr_pad, w_in[:, s2:s3]], axis=-1).astype(BF16)
    wdvt = w_in[:, s3:].T.astype(BF16)
    qd = MLA_NOPE + MLA_ROPE
    wuq = jnp.pad(mla_w_uq.reshape(MLA_Q_RANK, MLA_HEADS, qd), ((0, 0), (0, 0), (0, HEAD_PAD - qd)))
    wuq = wuq.reshape(MLA_Q_RANK, MLA_HEADS * HEAD_PAD).astype(BF16)
    ukv = mla_w_ukv.reshape(MLA_KV_RANK, MLA_HEADS, MLA_NOPE + MLA_V)
    wk = jnp.pad(ukv[:, :, :MLA_NOPE], ((0, 0), (0, 0), (0, HEAD_PAD - MLA_NOPE)))
    wk = wk.reshape(MLA_KV_RANK, MLA_HEADS * HEAD_PAD).astype(BF16)
    wvt = ukv[:, :, MLA_NOPE:].reshape(MLA_KV_RANK, MLA_HEADS * MLA_V).T.astype(BF16)
    bd = jax.scipy.linalg.block_diag(*[pool_w[g] for g in range(len(POOL_WINDOWS))]).astype(BF16)
    return wall, wdvt, wuq, wk, wvt, bd, w_out.astype(BF16)


def kernel(x, c, ctx, c_ctx, w_mod, b_mod, ffn1_norm, ffn1_w1, ffn1_w3, ffn1_w2, mix_norm, w_in, w_out, pool_w, pool_scale, mla_q_norm, mla_w_uq, mla_kv_norm, mla_w_ukv, diff_lambda, diff_subln, ffn2_norm, ffn2_w1, ffn2_w3, ffn2_w2, final_norm):
    bsz, seq, d = x.shape
    n_ctx = ctx.shape[1]
    depth = w_mod.shape[0]
    assert bsz + 1 <= MOD_ROWS and seq % GRID_W == 0
    tm = 1024
    ctx_row = bsz

    cc = jnp.zeros((MOD_ROWS, d), F32).at[:bsz].set(c).at[ctx_row].set(c_ctx)
    mods_all = _modulation(cc, w_mod, b_mod).reshape(depth, MOD_ROWS, N_MOD, d)

    n_flat = bsz * n_ctx
    tc = min(tm, n_flat)
    assert n_flat % tc == 0 and n_ctx % VT_CHUNK == 0
    flat = lambda t: t.reshape(1, n_flat, t.shape[-1])
    unflat = lambda t: t.reshape(bsz, n_ctx, t.shape[-1])
    unflat_vt = lambda t: t.reshape(bsz, n_ctx // VT_CHUNK, t.shape[2], VT_CHUNK)

    tab_lat, tab_ctx = _rope_tables(seq, n_flat)
    icg, icx = _inv_counts(GRID_W), _inv_counts(n_ctx)
    bf = lambda w: w.astype(BF16)
    mla_args = dict(w_out=MLA_HEADS * MLA_V, maps_per_head=1)
    dif_args = dict(w_out=DIFF_HEADS * DIFF_V, maps_per_head=2)

    h, hc = x, flat(ctx)
    for i in range(depth):
        last = i == depth - 1
        mods = mods_all[i]
        lam_init = 0.8 - 0.6 * math.exp(-0.3 * i)
        wall, wdvt, wuq, wk, wvt, bd, wo = _layer_weights(w_in[i], w_out[i], pool_w[i], mla_w_uq[i], mla_w_ukv[i])
        f1 = (ffn1_norm[i], bf(ffn1_w1[i]), bf(ffn1_w3[i]), bf(ffn1_w2[i]))
        f2 = (ffn2_norm[i], bf(ffn2_w1[i]), bf(ffn2_w3[i]), bf(ffn2_w2[i]))

        h = _ffn(h, mods, None, *f1, k0=0, tm=tm)
        hc = _ffn(hc, mods, ctx_row, *f1, k0=0, tm=tc)

        pw = (mix_norm[i], wall, wdvt, mla_q_norm[i], mla_kv_norm[i], wuq, wk, wvt)
        u, qm, km, vm, qd, kd, vd = _proj(h, mods, None, *pw, tab_lat, tm=tm)
        uc, qmc, kmc, vmc, qdc, kdc, vdc = _proj(hc, mods, ctx_row, *pw, tab_ctx, tm=tc)
        uc, qmc, kmc, qdc, kdc = (unflat(t) for t in (uc, qmc, kmc, qdc, kdc))
        vmc, vdc = unflat_vt(vmc), unflat_vt(vdc)

        a, ac = _pool(u, uc, icg, icx, bd, pool_scale[i])
        dif = functools.partial(_diff_attn_kernel, lam_init=lam_init)
        dextra = (diff_lambda[i], diff_subln[i].reshape(1, DIFF_V))
        b = _attention(_mla_attn_kernel, "mla_attn", qm, (km, kmc), (vm, vmc), (),
                       tq=2048, sub=512, **mla_args)
        cdiff = _attention(dif, "diff_attn", qd, (kd, kdc), (vd, vdc), dextra,
                           tq=1024, sub=256, **dif_args)
        h = _ffn(h, mods, None, *f2, k0=6, tm=tm, mix=(a, b, cdiff, wo),
                 final_g=final_norm if last else None)
        if not last:
            bc = _attention(_mla_attn_kernel, "mla_attn_ctx", qmc, (kmc,), (vmc,), (),
                            tq=n_ctx, sub=n_ctx, **mla_args)
            cc_ = _attention(dif, "diff_attn_ctx", qdc, (kdc,), (vdc,), dextra,
                             tq=n_ctx, sub=n_ctx, **dif_args)
            hc = _ffn(hc, mods, ctx_row, *f2, k0=6, tm=tc, mix=(flat(ac), flat(bc), flat(cc_), wo))
    return h
```

```python
import functools
import math

import jax
import jax.numpy as jnp
from jax import lax
from jax.experimental import pallas as pl
from jax.experimental.pallas import tpu as pltpu

F32 = jnp.float32
BF16 = jnp.bfloat16

GRID_W = 64
N_MOD = 9
POOL_WINDOWS = (2, 4, 8, 16)
POOL_GROUP = 64
POOL_WIDTH = POOL_GROUP * len(POOL_WINDOWS)
POOL_HALF = max(POOL_WINDOWS) // 2
MLA_HEADS = 4
MLA_NOPE = 64
MLA_ROPE = 32
MLA_V = 64
MLA_Q_RANK = 384
MLA_KV_RANK = 256
DIFF_HEADS = 4
DIFF_DIM = 64
DIFF_V = 2 * DIFF_DIM
ROPE_BASE = 10000.0
EPS = 1e-6

LANES = 128
HEAD_PAD = LANES
VT_CHUNK = 256
SUM_ROWS = 16
MLA_VT_ROWS = MLA_V + SUM_ROWS
MOD_ROWS = 24
VMEM_LIMIT = 56 * 1024 * 1024

C_POOL = 0
C_CQ = C_POOL + POOL_WIDTH
C_CKV = C_CQ + MLA_Q_RANK
C_KR = C_CKV + MLA_KV_RANK
C_DQ = C_KR + HEAD_PAD
C_DK = C_DQ + DIFF_HEADS * 2 * DIFF_DIM
C_DV = C_DK + DIFF_HEADS * 2 * DIFF_DIM

LOG2E = math.log2(math.e)
MLA_QSCALE = (MLA_NOPE + MLA_ROPE) ** -0.5 * LOG2E
DIFF_QSCALE = DIFF_DIM ** -0.5 * LOG2E


def _dot(a, b):
    return jnp.dot(a, b, preferred_element_type=F32)


def _dot_nt(a, b):
    return lax.dot_general(a, b, (((1,), (1,)), ((), ())), preferred_element_type=F32)


def _rms(x, g):
    return x * lax.rsqrt(jnp.mean(x * x, axis=-1, keepdims=True) + EPS) * g


def _silu(a):
    return a * (1.0 / (1.0 + jnp.exp(-a)))


def _const_spec(shape):
    return pl.BlockSpec(shape, lambda *_: (0,) * len(shape), pipeline_mode=pl.Buffered(1))


def _params(n_axes):
    return pltpu.CompilerParams(dimension_semantics=("parallel",) * n_axes, vmem_limit_bytes=VMEM_LIMIT)


def _mod_kernel(cc_ref, w_ref, b_ref, o_ref):
    sc = _silu(cc_ref[...])
    o_ref[0] = jnp.dot(sc, w_ref[0], preferred_element_type=F32, precision=lax.Precision.HIGHEST) + b_ref[0]


def _modulation(cc, w_mod, b_mod):
    depth, d, width = w_mod.shape
    bn = 9 * LANES
    return pl.pallas_call(
        _mod_kernel,
        grid=(depth, width // bn),
        in_specs=[
            pl.BlockSpec((MOD_ROWS, d), lambda l, j: (0, 0)),
            pl.BlockSpec((1, d, bn), lambda l, j: (l, 0, j)),
            pl.BlockSpec((1, 1, bn), lambda l, j: (l, 0, j)),
        ],
        out_specs=pl.BlockSpec((1, MOD_ROWS, bn), lambda l, j: (l, 0, j)),
        out_shape=jax.ShapeDtypeStruct((depth, MOD_ROWS, width), F32),
        compiler_params=_params(2),
        name="modulation",
    )(cc, w_mod, b_mod.reshape(depth, 1, width))


MXU_COLS = 256
FFN_CHUNK = MXU_COLS


def _ffn_kernel(*refs, pre, final, k0):
    it = iter(refs)
    x_ref, m_ref = next(it), next(it)
    if pre:
        a_ref, b_ref, c_ref, wo_ref = next(it), next(it), next(it), next(it)
    g_ref, w1_ref, w3_ref, w2_ref = next(it), next(it), next(it), next(it)
    if final:
        fg_ref = next(it)
    o_ref, hmid_ref = next(it), next(it)

    x = x_ref[0]
    if pre:
        mix = jnp.concatenate([a_ref[0], b_ref[0], c_ref[0]], axis=-1)
        x = x + m_ref[0, 5:6, :] * _dot(mix, wo_ref[...])
    xm = _rms(x, g_ref[...]) * (1.0 + m_ref[0, k0 + 1:k0 + 2, :]) + m_ref[0, k0:k0 + 1, :]
    xb = xm.astype(BF16)
    dff = w1_ref.shape[1]
    for lo in range(0, dff, FFN_CHUNK):
        cols = slice(lo, min(lo + FFN_CHUNK, dff))
        a = _dot(xb, w1_ref[:, cols])
        b = _dot(xb, w3_ref[:, cols])
        hmid_ref[:, cols] = (_silu(a) * b).astype(BF16)
    y = _dot(hmid_ref[...], w2_ref[...])
    out = x + (0.5 * m_ref[0, k0 + 2:k0 + 3, :]) * y
    if final:
        out = _rms(out, fg_ref[...])
    o_ref[0] = out


def _ffn(x, mods, mod_row, g, w1, w3, w2, *, k0, tm, mix=None, final_g=None):
    bsz, n, d = x.shape
    dff = w1.shape[1]
    pre, final = mix is not None, final_g is not None
    tok = lambda w: pl.BlockSpec((1, tm, w), lambda b, i: (b, i, 0))
    mod_spec = pl.BlockSpec((1, N_MOD, d), (lambda b, i: (b, 0, 0)) if mod_row is None else (lambda b, i: (mod_row, 0, 0)))
    args, specs = [x, mods], [tok(d), mod_spec]
    if pre:
        a, bb, c, wo = mix
        args += [a, bb, c, wo]
        specs += [tok(a.shape[-1]), tok(bb.shape[-1]), tok(c.shape[-1]), _const_spec(wo.shape)]
    args += [g.reshape(1, d), w1, w3, w2]
    specs += [_const_spec((1, d)), _const_spec(w1.shape), _const_spec(w3.shape), _const_spec(w2.shape)]
    if final:
        args.append(final_g.reshape(1, d))
        specs.append(_const_spec((1, d)))
    return pl.pallas_call(
        functools.partial(_ffn_kernel, pre=pre, final=final, k0=k0),
        grid=(bsz, n // tm),
        in_specs=specs,
        out_specs=tok(d),
        out_shape=jax.ShapeDtypeStruct(x.shape, F32),
        scratch_shapes=[pltpu.VMEM((tm, dff), BF16)],
        compiler_params=_params(2),
        name="ffn_mix" if pre else "ffn",
    )(*args)


def _rope(x, c, sa, sb, shift):
    return x * c + pltpu.roll(x, LANES - shift, 1) * sa + pltpu.roll(x, shift, 1) * sb


def _proj_kernel(x_ref, m_ref, g_ref, wall_ref, wdvt_ref, qn_ref, kvn_ref, wuq_ref, wk_ref, wvt_ref,
                 cm_ref, sam_ref, sbm_ref, cd_ref, sad_ref, sbd_ref,
                 u_ref, qm_ref, km_ref, vmt_ref, qd_ref, kd_ref, vdt_ref):
    x = x_ref[0]
    n = _rms(x, g_ref[...]) * (1.0 + m_ref[0, 4:5, :]) + m_ref[0, 3:4, :]
    nb = n.astype(BF16)
    proj = lambda lo, hi: _dot(nb, wall_ref[:, lo:hi])

    pm = proj(C_CQ, C_DQ)
    cqn = _rms(pm[:, :C_CKV - C_CQ], qn_ref[...]).astype(BF16)
    ckvn = _rms(pm[:, C_CKV - C_CQ:C_KR - C_CQ], kvn_ref[...]).astype(BF16)
    q = _dot(cqn, wuq_ref[...])
    k = _dot(ckvn, wk_ref[...])
    vt = _dot_nt(wvt_ref[...], ckvn).astype(BF16)
    ones = jnp.ones((SUM_ROWS, VT_CHUNK), BF16)
    for j in range(vmt_ref.shape[1]):
        cols = slice(j * VT_CHUNK, (j + 1) * VT_CHUNK)
        for h in range(MLA_HEADS):
            vmt_ref[0, j, h * MLA_VT_ROWS:h * MLA_VT_ROWS + MLA_V, :] = vt[h * MLA_V:(h + 1) * MLA_V, cols]
            vmt_ref[0, j, h * MLA_VT_ROWS + MLA_V:(h + 1) * MLA_VT_ROWS, :] = ones
    cm, sam, sbm = cm_ref[...], sam_ref[...], sbm_ref[...]
    half_m = MLA_ROPE // 2
    kr = _rope(pm[:, C_KR - C_CQ:], cm, sam, sbm, half_m)
    for h in range(MLA_HEADS):
        sl = slice(h * HEAD_PAD, (h + 1) * HEAD_PAD)
        qm_ref[0, :, sl] = (_rope(q[:, sl], cm, sam, sbm, half_m) * MLA_QSCALE).astype(BF16)
        km_ref[0, :, sl] = (k[:, sl] + kr).astype(BF16)

    cd, sad, sbd = cd_ref[...], sad_ref[...], sbd_ref[...]
    half_d = DIFF_DIM // 2
    for col0, out_ref, scale in ((C_DQ, qd_ref, DIFF_QSCALE), (C_DK, kd_ref, None)):
        for h0 in range(0, DIFF_HEADS, 2):
            pp = proj(col0 + h0 * HEAD_PAD, col0 + (h0 + 2) * HEAD_PAD)
            for h in (h0, h0 + 1):
                r = _rope(pp[:, (h - h0) * HEAD_PAD:(h - h0 + 1) * HEAD_PAD], cd, sad, sbd, half_d)
                out_ref[0, :, h * HEAD_PAD:(h + 1) * HEAD_PAD] = (r if scale is None else r * scale).astype(BF16)
    u_ref[0] = proj(C_POOL, C_CQ)
    vdt = _dot_nt(wdvt_ref[...], nb).astype(BF16)
    for j in range(vdt_ref.shape[1]):
        vdt_ref[0, j] = vdt[:, j * VT_CHUNK:(j + 1) * VT_CHUNK]


def _proj(x, mods, mod_row, g, wall, wdvt, qn, kvn, wuq, wk, wvt, tables, *, tm):
    bsz, n, d = x.shape
    tok = lambda w: pl.BlockSpec((1, tm, w), lambda b, i: (b, i, 0))
    vt = lambda r: pl.BlockSpec((1, tm // VT_CHUNK, r, VT_CHUNK), lambda b, i: (b, i, 0, 0))
    mod_spec = pl.BlockSpec((1, N_MOD, d), (lambda b, i: (b, 0, 0)) if mod_row is None else (lambda b, i: (mod_row, 0, 0)))
    tab = pl.BlockSpec((tm, LANES), lambda b, i: (i, 0))
    wm, wd = MLA_HEADS * HEAD_PAD, DIFF_HEADS * HEAD_PAD
    rm, rd = MLA_HEADS * MLA_VT_ROWS, DIFF_HEADS * DIFF_V
    tok_sds = lambda w, dt: jax.ShapeDtypeStruct((bsz, n, w), dt)
    vt_sds = lambda r: jax.ShapeDtypeStruct((bsz, n // VT_CHUNK, r, VT_CHUNK), BF16)
    return pl.pallas_call(
        _proj_kernel,
        grid=(bsz, n // tm),
        in_specs=[tok(d), mod_spec, _const_spec((1, d)), _const_spec(wall.shape), _const_spec(wdvt.shape),
                  _const_spec((1, MLA_Q_RANK)), _const_spec((1, MLA_KV_RANK)),
                  _const_spec(wuq.shape), _const_spec(wk.shape), _const_spec(wvt.shape)] + [tab] * 6,
        out_specs=[tok(POOL_WIDTH), tok(wm), tok(wm), vt(rm), tok(wd), tok(wd), vt(rd)],
        out_shape=[tok_sds(POOL_WIDTH, F32), tok_sds(wm, BF16), tok_sds(wm, BF16), vt_sds(rm),
                   tok_sds(wd, BF16), tok_sds(wd, BF16), vt_sds(rd)],
        compiler_params=_params(2),
        name="mix_in",
    )(x, mods, g.reshape(1, d), wall, wdvt, qn.reshape(1, -1), kvn.reshape(1, -1), wuq, wk, wvt, *tables)


def _nested_window_sums(load, lane):
    a2 = load(-1) + load(0)
    a4 = a2 + load(-2) + load(1)
    a8 = a4 + load(-4) + load(-3) + load(2) + load(3)
    a16 = a8 + load(-8) + load(-7) + load(-6) + load(-5) + load(4) + load(5) + load(6) + load(7)
    return jnp.where(lane < POOL_GROUP, a2, jnp.where(lane < 2 * POOL_GROUP, a4, jnp.where(lane < 3 * POOL_GROUP, a8, a16)))


def _window_sums_1d(slab, n, lane):
    size = slab.shape[0]
    ahead = lambda x, k: pltpu.roll(x, size - k, 0)
    behind = lambda x, k: pltpu.roll(x, k, 0)[POOL_HALF:POOL_HALF + n]
    p2 = slab + ahead(slab, 1)
    p4 = p2 + ahead(p2, 2)
    p8 = p4 + ahead(p4, 4)
    p16 = p8 + ahead(p8, 8)
    return jnp.where(lane < POOL_GROUP, behind(p2, 1),
                     jnp.where(lane < 2 * POOL_GROUP, behind(p4, 2),
                               jnp.where(lane < 3 * POOL_GROUP, behind(p8, 4), p16[:n])))


def _pool_kernel(ul_ref, uc_ref, icg_ref, icx_ref, bd_ref, ps_ref, al_ref, ac_ref,
                 z_ref, y_ref, dl_ref, zc_ref, *, rows):
    gw, pad = GRID_W, POOL_HALF
    stride = gw + 2 * pad
    n_ctx = uc_ref.shape[1]
    lane = lax.broadcasted_iota(jnp.int32, (gw, POOL_WIDTH), 1)
    zero_rows = jnp.zeros((pad * gw, POOL_WIDTH), F32)
    zero_pad = jnp.zeros((pad, POOL_WIDTH), F32)

    z_ref[0:pad * gw, :] = zero_rows
    z_ref[pad * gw:(rows + pad) * gw, :] = ul_ref[0]
    z_ref[(rows + pad) * gw:(rows + 2 * pad) * gw, :] = zero_rows

    def row_pass(r, carry):
        base = pl.multiple_of((r + pad) * gw, 8)
        s = _nested_window_sums(lambda d: z_ref[pl.ds(base + d * gw, gw), :], lane)
        yb = pl.multiple_of(r * stride, 8)
        y_ref[pl.ds(yb, pad), :] = zero_pad
        y_ref[pl.ds(yb + pad, gw), :] = s * icg_ref[pl.ds(r, 1), :]
        y_ref[pl.ds(yb + pad + gw, pad), :] = zero_pad
        return carry

    lax.fori_loop(0, rows, row_pass, 0)

    def col_pass(r, carry):
        slab = y_ref[pl.ds(pl.multiple_of(r * stride, 8), stride), :]
        s = _window_sums_1d(slab, gw, lane)
        tok = pl.ds(pl.multiple_of(r * gw, 8), gw)
        dl_ref[tok, :] = (s * icg_ref[...] - ul_ref[0, tok, :]).astype(BF16)
        return carry

    lax.fori_loop(0, rows, col_pass, 0)
    al_ref[0] = (_dot(dl_ref[...], bd_ref[...]) * ps_ref[...]).astype(BF16)

    uc = uc_ref[0]
    zc_ref[0:pad, :] = zero_pad
    zc_ref[pad:pad + n_ctx, :] = uc
    zc_ref[pad + n_ctx:pad + n_ctx + pad, :] = zero_pad
    lane_c = lax.broadcasted_iota(jnp.int32, (n_ctx, POOL_WIDTH), 1)
    sc = _window_sums_1d(zc_ref[...], n_ctx, lane_c)
    dc = (sc * icx_ref[...] - uc).astype(BF16)
    ac_ref[0] = (_dot(dc, bd_ref[...]) * ps_ref[...]).astype(BF16)


def _pool(u_lat, u_ctx, icg, icx, bd, ps):
    bsz, s, w = u_lat.shape
    n_ctx = u_ctx.shape[1]
    rows = s // GRID_W
    stride = GRID_W + 2 * POOL_HALF
    return pl.pallas_call(
        functools.partial(_pool_kernel, rows=rows),
        grid=(bsz,),
        in_specs=[pl.BlockSpec((1, s, w), lambda b: (b, 0, 0)), pl.BlockSpec((1, n_ctx, w), lambda b: (b, 0, 0)),
                  _const_spec(icg.shape), _const_spec(icx.shape), _const_spec(bd.shape), _const_spec((1, w))],
        out_specs=[pl.BlockSpec((1, s, w), lambda b: (b, 0, 0)), pl.BlockSpec((1, n_ctx, w), lambda b: (b, 0, 0))],
        out_shape=[jax.ShapeDtypeStruct((bsz, s, w), BF16), jax.ShapeDtypeStruct((bsz, n_ctx, w), BF16)],
        scratch_shapes=[pltpu.VMEM(((rows + 2 * POOL_HALF) * GRID_W, w), F32), pltpu.VMEM((rows * stride, w), F32),
                        pltpu.VMEM((s, w), BF16), pltpu.VMEM((n_ctx + 2 * POOL_HALF, w), F32)],
        compiler_params=_params(1),
        name="pool",
    )(u_lat, u_ctx, icg, icx, bd, ps.reshape(1, w))


def _attn_units(once_ref, heads, vrows, prepare, qt_of, k_refs, vt_refs, s_refs, mx_refs, l_refs, acc_refs, finish,
                first=None):
    n_units = len(heads)
    assert first is None or n_units % 2 == 0
    heads = list(heads) + list(heads[:2])
    n = s_refs[0].shape[1]
    offs, off = [], 0
    for k_ref in k_refs:
        offs.append(off)
        off += k_ref.shape[1]

    def loop_body(fns):
        def whole(t, carry):
            for k_ref, vt_ref, o in zip(k_refs, vt_refs, offs):
                n_chunks, kc = vt_ref.shape[1], vt_ref.shape[3]
                for c in range(n_chunks):
                    for fn in fns["chunk"]:
                        fn(k_ref, vt_ref, c, slice(c * kc, (c + 1) * kc), slice(o + c * kc, o + (c + 1) * kc))
            for fn in fns["once"]:
                fn()
            return carry

        lax.fori_loop(0, once_ref[0], whole, 0)

    def scores(u):
        slot = u % 2
        ksl = slice(heads[u] * HEAD_PAD, (heads[u] + 1) * HEAD_PAD)

        def fn(k_ref, vt_ref, c, krows, srows):
            st = _dot(k_ref[0, krows, ksl], qt_of(u))
            s_refs[slot][srows, :] = st
            mx_refs[slot][...] = jnp.maximum(mx_refs[slot][...], jnp.max(st.reshape(-1, 8, n), axis=0))

        return fn

    def probs(u, m):
        slot = u % 2
        vsl = slice(heads[u] * vrows, (heads[u] + 1) * vrows)

        def fn(k_ref, vt_ref, c, krows, srows):
            p = jnp.exp2(s_refs[slot][srows, :] - m)
            if l_refs is not None:
                l_refs[slot][...] += jnp.sum(p.reshape(-1, 8, n), axis=0)
            acc_refs[slot][...] += _dot(vt_ref[0, c, vsl, :], p.astype(BF16))

        return fn

    def epilogue(u):
        slot = u % 2
        return lambda: finish(u, acc_refs[slot][...],
                              None if l_refs is None else jnp.sum(l_refs[slot][...], axis=0, keepdims=True))

    neg_inf = jnp.full(mx_refs[0].shape, -jnp.inf, F32)
    last_unit = n_units if first is None else n_units + 2

    def fill():
        mx_refs[0][...] = neg_inf
        prepare(0)
        loop_body({"chunk": [scores(0)], "once": [lambda: prepare(1)] if n_units > 1 else []})

    if first is None:
        fill()
    else:
        pl.when(first)(fill)
    for u in range(n_units):
        slot = u % 2
        m = jnp.max(mx_refs[slot][...], axis=0, keepdims=True)
        if l_refs is not None:
            l_refs[slot][...] = jnp.zeros(l_refs[slot].shape, F32)
        acc_refs[slot][...] = jnp.zeros(acc_refs[slot].shape, F32)
        fns = {"chunk": [probs(u, m)], "once": [epilogue(u - 1)] if u > 0 else []}
        if u + 2 < last_unit:
            fns["once"].append(functools.partial(prepare, u + 2))
        if u + 1 < min(last_unit, n_units + 1):
            mx_refs[1 - slot][...] = neg_inf
            fns["chunk"].insert(0, scores(u + 1))
        loop_body(fns)
    epilogue(n_units - 1)()


def _first_step(lookahead):
    return (pl.program_id(1) == 0) if lookahead else None


def _mla_attn_kernel(once_ref, *refs, nseg, sub, lookahead):
    q_ref, qn_ref, k_refs, vt_refs = refs[0], refs[1], refs[2:2 + nseg], refs[2 + nseg:2 + 2 * nseg]
    o_ref, s0_ref, s1_ref, mx0_ref, mx1_ref, acc0_ref, acc1_ref, ot_ref, qt_ref = refs[2 + 2 * nseg:]
    units = [(sb, h) for sb in range(q_ref.shape[1] // sub) for h in range(MLA_HEADS)]
    n_units = len(units)

    def prepare(u):
        sb, h = units[u % n_units]
        qh = (q_ref if u < n_units else qn_ref)[0, sb * sub:(sb + 1) * sub, h * HEAD_PAD:(h + 1) * HEAD_PAD]
        qt_ref[u % n_units] = qh.astype(F32).T.astype(BF16)

    def finish(u, acc, l):
        sb, h = units[u]
        ot_ref[(h % 2) * MLA_V:(h % 2 + 1) * MLA_V, :] = acc[:MLA_V] * (1.0 / acc[MLA_V:MLA_V + 1])
        if h % 2 == 1:
            o_ref[0, sb * sub:(sb + 1) * sub, (h - 1) * MLA_V:(h + 1) * MLA_V] = ot_ref[...].T.astype(BF16)

    _attn_units(once_ref, [h for _, h in units], MLA_VT_ROWS, prepare, lambda u: qt_ref[u % n_units],
                k_refs, vt_refs, (s0_ref, s1_ref), (mx0_ref, mx1_ref), None, (acc0_ref, acc1_ref), finish,
                _first_step(lookahead))


def _diff_attn_kernel(once_ref, *refs, nseg, sub, lookahead, lam_init):
    dl_ref, sub_ref, q_ref, qn_ref = refs[0], refs[1], refs[2], refs[3]
    k_refs, vt_refs = refs[4:4 + nseg], refs[4 + nseg:4 + 2 * nseg]
    o_ref, s0_ref, s1_ref, mx0_ref, mx1_ref, acc0_ref, acc1_ref, l0_ref, l1_ref, qu_ref = refs[4 + 2 * nseg:]
    units = [(sb, h) for sb in range(q_ref.shape[1] // sub) for h in range(DIFF_HEADS)]
    n_units = len(units)
    dl = dl_ref[...]
    lam = (jnp.exp(jnp.sum(dl[0:1] * dl[1:2], axis=-1, keepdims=True))
           - jnp.exp(jnp.sum(dl[2:3] * dl[3:4], axis=-1, keepdims=True)) + lam_init)
    dim = lax.broadcasted_iota(jnp.int32, (HEAD_PAD, sub), 0)

    def prepare(u):
        sb, h = units[u % n_units]
        qh = (q_ref if u < n_units else qn_ref)[0, sb * sub:(sb + 1) * sub, h * HEAD_PAD:(h + 1) * HEAD_PAD]
        qt = qh.astype(F32).T
        qu_ref[u % n_units, :, :sub] = jnp.where(dim < DIFF_DIM, qt, 0.0).astype(BF16)
        qu_ref[u % n_units, :, sub:] = jnp.where(dim >= DIFF_DIM, qt, 0.0).astype(BF16)

    def finish(u, acc, l):
        sb, h = units[u]
        o = acc[:, :sub] * (1.0 / l[:, :sub]) - acc[:, sub:] * (lam / l[:, sub:])
        on = o * lax.rsqrt(jnp.mean(o * o, axis=0, keepdims=True) + EPS)
        o_ref[0, sb * sub:(sb + 1) * sub, h * DIFF_V:(h + 1) * DIFF_V] = (
            on.T * sub_ref[...] * (1.0 - lam_init)).astype(BF16)

    _attn_units(once_ref, [h for _, h in units], DIFF_V, prepare, lambda u: qu_ref[u % n_units],
                k_refs, vt_refs, (s0_ref, s1_ref), (mx0_ref, mx1_ref), (l0_ref, l1_ref), (acc0_ref, acc1_ref),
                finish, _first_step(lookahead))


def _attention(kernel, name, q, ks, vts, extra, *, tq, sub, w_out, maps_per_head):
    bsz, nq, wq = q.shape
    nseg = len(ks)
    n_steps = nq // tq
    lookahead = n_steps > 1
    n_heads = wq // HEAD_PAD
    whole = lambda a: pl.BlockSpec((1,) + a.shape[1:], lambda b, i: (b,) + (0,) * (a.ndim - 1))
    vrows = vts[0].shape[2] // n_heads
    n_keys = sum(k.shape[1] for k in ks)
    n = sub * maps_per_head
    scratch = [pltpu.VMEM((n_keys, n), F32)] * 2 + [pltpu.VMEM((8, n), F32)] * 2 + [pltpu.VMEM((vrows, n), F32)] * 2
    qt_scratch = pltpu.VMEM((n_heads * tq // sub, HEAD_PAD, n), BF16)
    if maps_per_head > 1:
        scratch += [pltpu.VMEM((8, n), F32)] * 2 + [qt_scratch]
    else:
        scratch += [pltpu.VMEM((LANES, n), F32), qt_scratch]
    once = jnp.ones((1,), jnp.int32)
    q_spec = pl.BlockSpec((1, tq, wq), lambda b, i: (b, i, 0))
    q_next_spec = pl.BlockSpec((1, tq, wq), lambda b, i: (b, jnp.minimum(i + 1, n_steps - 1), 0))
    semantics = ("parallel", "arbitrary" if lookahead else "parallel")
    return pl.pallas_call(
        functools.partial(kernel, nseg=nseg, sub=sub, lookahead=lookahead),
        grid=(bsz, n_steps),
        in_specs=[pl.BlockSpec(memory_space=pltpu.SMEM)] + [_const_spec(e.shape) for e in extra]
        + [q_spec, q_next_spec] + [whole(k) for k in ks] + [whole(v) for v in vts],
        out_specs=pl.BlockSpec((1, tq, w_out), lambda b, i: (b, i, 0)),
        out_shape=jax.ShapeDtypeStruct((bsz, nq, w_out), BF16),
        scratch_shapes=scratch,
        compiler_params=pltpu.CompilerParams(dimension_semantics=semantics, vmem_limit_bytes=VMEM_LIMIT),
        name=name,
    )(once, *extra, q, q, *ks, *vts)


def _axial_angles(n_tok, rot_dim):
    t = jnp.arange(n_tok)
    row = (t // GRID_W).astype(F32)
    col = (t % GRID_W).astype(F32)
    n_axis = rot_dim // 4
    inv = ROPE_BASE ** (-jnp.arange(n_axis, dtype=F32) / n_axis)
    return jnp.concatenate([row[:, None] * inv, col[:, None] * inv], axis=-1)


def _rope_tables(n_tok, n_ctx):
    am, ad = _axial_angles(n_tok, MLA_ROPE), _axial_angles(n_tok, DIFF_DIM)
    one = lambda w: jnp.ones((n_tok, w), F32)
    zero = lambda w: jnp.zeros((n_tok, w), F32)
    cm = jnp.concatenate([one(MLA_NOPE), jnp.cos(am), jnp.cos(am), one(HEAD_PAD - MLA_NOPE - MLA_ROPE)], axis=-1)
    sam = jnp.concatenate([zero(MLA_NOPE), -jnp.sin(am), zero(HEAD_PAD - MLA_NOPE - MLA_ROPE // 2)], axis=-1)
    sbm = jnp.concatenate([zero(MLA_NOPE + MLA_ROPE // 2), jnp.sin(am), zero(HEAD_PAD - MLA_NOPE - MLA_ROPE)], axis=-1)
    cd = jnp.tile(jnp.cos(ad), (1, 4))
    sad = jnp.tile(jnp.concatenate([-jnp.sin(ad), zero(DIFF_DIM // 2)], axis=-1), (1, 2))
    sbd = jnp.tile(jnp.concatenate([zero(DIFF_DIM // 2), jnp.sin(ad)], axis=-1), (1, 2))
    lat = (cm, sam, sbm, cd, sad, sbd)
    ident = (jnp.ones((n_ctx, LANES), F32),) + (jnp.zeros((n_ctx, LANES), F32),) * 2
    return lat, ident + ident


def _inv_counts(n):
    idx = jnp.arange(n)
    cols = []
    for k in POOL_WINDOWS:
        lo, hi = k // 2, k - 1 - k // 2
        cnt = (jnp.clip(idx + hi + 1, 0, n) - jnp.clip(idx - lo, 0, n)).astype(F32)
        cols.append(jnp.broadcast_to((1.0 / cnt)[:, None], (n, POOL_GROUP)))
    return jnp.concatenate(cols, axis=-1)


def _layer_weights(w_in, w_out, pool_w, mla_w_uq, mla_w_ukv):
    d = w_in.shape[0]
    s1, s2 = POOL_WIDTH, POOL_WIDTH + MLA_Q_RANK + MLA_KV_RANK + MLA_ROPE
    kr_pad = jnp.zeros((d, HEAD_PAD), F32).at[:, MLA_NOPE:MLA_NOPE + MLA_ROPE].set(w_in[:, s2 - MLA_ROPE:s2])
    s3 = s2 + 2 * DIFF_HEADS * 2 * DIFF_DIM
    wall = jnp.concatenate([w_in[:, :s2 - MLA_ROPE], kr_pad, w_in[:, s2:s3]], axis=-1).astype(BF16)
    wdvt = w_in[:, s3:].T.astype(BF16)
    qd = MLA_NOPE + MLA_ROPE
    wuq = jnp.pad(mla_w_uq.reshape(MLA_Q_RANK, MLA_HEADS, qd), ((0, 0), (0, 0), (0, HEAD_PAD - qd)))
    wuq = wuq.reshape(MLA_Q_RANK, MLA_HEADS * HEAD_PAD).astype(BF16)
    ukv = mla_w_ukv.reshape(MLA_KV_RANK, MLA_HEADS, MLA_NOPE + MLA_V)
    wk = jnp.pad(ukv[:, :, :MLA_NOPE], ((0, 0), (0, 0), (0, HEAD_PAD - MLA_NOPE)))
    wk = wk.reshape(MLA_KV_RANK, MLA_HEADS * HEAD_PAD).astype(BF16)
    wvt = ukv[:, :, MLA_NOPE:].reshape(MLA_KV_RANK, MLA_HEADS * MLA_V).T.astype(BF16)
    bd = jax.scipy.linalg.block_diag(*[pool_w[g] for g in range(len(POOL_WINDOWS))]).astype(BF16)
    return wall, wdvt, wuq, wk, wvt, bd, w_out.astype(BF16)


def kernel(x, c, ctx, c_ctx, w_mod, b_mod, ffn1_norm, ffn1_w1, ffn1_w3, ffn1_w2, mix_norm, w_in, w_out, pool_w, pool_scale, mla_q_norm, mla_w_uq, mla_kv_norm, mla_w_ukv, diff_lambda, diff_subln, ffn2_norm, ffn2_w1, ffn2_w3, ffn2_w2, final_norm):
    bsz, seq, d = x.shape
    n_ctx = ctx.shape[1]
    depth = w_mod.shape[0]
    assert bsz + 1 <= MOD_ROWS and seq % GRID_W == 0
    tm = 1024
    ctx_row = bsz

    cc = jnp.zeros((MOD_ROWS, d), F32).at[:bsz].set(c).at[ctx_row].set(c_ctx)
    mods_all = _modulation(cc, w_mod, b_mod).reshape(depth, MOD_ROWS, N_MOD, d)

    tab_lat, tab_ctx = _rope_tables(seq, n_ctx)
    icg, icx = _inv_counts(GRID_W), _inv_counts(n_ctx)
    bf = lambda w: w.astype(BF16)
    mla_args = dict(w_out=MLA_HEADS * MLA_V, maps_per_head=1)
    dif_args = dict(w_out=DIFF_HEADS * DIFF_V, maps_per_head=2)

    h, hc = x, ctx
    for i in range(depth):
        last = i == depth - 1
        mods = mods_all[i]
        lam_init = 0.8 - 0.6 * math.exp(-0.3 * i)
        wall, wdvt, wuq, wk, wvt, bd, wo = _layer_weights(w_in[i], w_out[i], pool_w[i], mla_w_uq[i], mla_w_ukv[i])
        f1 = (ffn1_norm[i], bf(ffn1_w1[i]), bf(ffn1_w3[i]), bf(ffn1_w2[i]))
        f2 = (ffn2_norm[i], bf(ffn2_w1[i]), bf(ffn2_w3[i]), bf(ffn2_w2[i]))

        h = _ffn(h, mods, None, *f1, k0=0, tm=tm)
        hc = _ffn(hc, mods, ctx_row, *f1, k0=0, tm=n_ctx)

        pw = (mix_norm[i], wall, wdvt, mla_q_norm[i], mla_kv_norm[i], wuq, wk, wvt)
        u, qm, km, vm, qd, kd, vd = _proj(h, mods, None, *pw, tab_lat, tm=tm)
        uc, qmc, kmc, vmc, qdc, kdc, vdc = _proj(hc, mods, ctx_row, *pw, tab_ctx, tm=n_ctx)

        a, ac = _pool(u, uc, icg, icx, bd, pool_scale[i])
        dif = functools.partial(_diff_attn_kernel, lam_init=lam_init)
        dextra = (diff_lambda[i], diff_subln[i].reshape(1, DIFF_V))
        b = _attention(_mla_attn_kernel, "mla_attn", qm, (km, kmc), (vm, vmc), (),
                       tq=2048, sub=512, **mla_args)
        cdiff = _attention(dif, "diff_attn", qd, (kd, kdc), (vd, vdc), dextra,
                           tq=2048, sub=256, **dif_args)
        h = _ffn(h, mods, None, *f2, k0=6, tm=tm, mix=(a, b, cdiff, wo),
                 final_g=final_norm if last else None)
        if not last:
            bc = _attention(_mla_attn_kernel, "mla_attn_ctx", qmc, (kmc,), (vmc,), (),
                            tq=n_ctx, sub=n_ctx, **mla_args)
            cc_ = _attention(dif, "diff_attn_ctx", qdc, (kdc,), (vdc,), dextra,
                             tq=n_ctx, sub=n_ctx, **dif_args)
            hc = _ffn(hc, mods, ctx_row, *f2, k0=6, tm=n_ctx, mix=(ac, bc, cc_, wo))
    return h
```
